```python
import functools
import jax, jax.numpy as jnp
from jax import lax
import numpy as np

D_MODEL = 2048
BATCH = 2
SEQ = 4096
DEPTH = 1
DEC_BATCH = 32
DEC_SEQ = 8
PAST_LEN = 16384
PAGE_SIZE = 128

N_HEADS = 8
HEAD_DIM = 128
D_ATTN = N_HEADS * HEAD_DIM
D_CONV = D_MODEL - D_ATTN
D_MIX = D_ATTN + D_CONV
D_IN = 3 * D_ATTN + 2 * D_CONV
CONV_GROUPS = 8
CONV_WIDTH = 31
MOBA_BLOCK = 256
MOBA_TOPK = 3
Q_BLOCK = 128
N_EXPERT_GROUPS = 4
EXPERTS_PER_GROUP = 8
N_EXPERTS = N_EXPERT_GROUPS * EXPERTS_PER_GROUP
TOP_E = 2
D_EXPERT = 512
EPS = 1e-6
NEG_INF = -1e30

kernel_name = "hymba_moba_conformer_hmoe_step"


def rmsnorm(x, g):
    xf = x.astype(jnp.float32)
    y = xf * lax.rsqrt(jnp.mean(xf * xf, axis=-1, keepdims=True) + EPS)
    return (y * g.astype(jnp.float32)).astype(x.dtype)


def alibi_slopes():
    return jnp.exp2(-8.0 * jnp.arange(1, N_HEADS + 1, dtype=jnp.float32) / N_HEADS)


def combined_projection(h, w_in):
    B, T, _ = h.shape
    z = h @ w_in
    q, k, v, a, g = jnp.split(z, [D_ATTN, 2 * D_ATTN, 3 * D_ATTN, 3 * D_ATTN + D_CONV], axis=-1)
    shp = (B, T, N_HEADS, HEAD_DIM)
    return q.reshape(shp), k.reshape(shp), v.reshape(shp), a * jax.nn.sigmoid(g)


def conformer_conv(glu, conv_state, w_dw, b_dw, gn_g, gn_b):
    B, T, _ = glu.shape
    xin = jnp.concatenate([conv_state.astype(glu.dtype), glu], axis=1)
    y = lax.conv_general_dilated(xin, w_dw[:, None, :], window_strides=(1,), padding='VALID',
                                 dimension_numbers=('NWC', 'WIO', 'NWC'),
                                 feature_group_count=D_CONV) + b_dw
    yg = y.astype(jnp.float32).reshape(B, T, CONV_GROUPS, D_CONV // CONV_GROUPS)
    mu = jnp.mean(yg, axis=-1, keepdims=True)
    var = jnp.mean(jnp.square(yg - mu), axis=-1, keepdims=True)
    yn = ((yg - mu) * lax.rsqrt(var + EPS)).reshape(B, T, D_CONV)
    yn = yn * gn_g.astype(jnp.float32) + gn_b.astype(jnp.float32)
    out = jax.nn.silu(yn).astype(glu.dtype)
    return out, xin[:, -(CONV_WIDTH - 1):]


def moba_softmax(q, k_sel, v_sel, pos_sel, valid_sel, k_own, v_own, pos_own, q_pos, slopes):
    scale = HEAD_DIM ** -0.5
    qpf = q_pos.astype(jnp.float32)
    s_sel = jnp.einsum('bqhd,bqhnd->bqhn', q, k_sel, preferred_element_type=jnp.float32) * scale
    s_sel = s_sel - slopes[:, None] * (qpf[:, None, None] - pos_sel.astype(jnp.float32))
    s_sel = jnp.where(valid_sel, s_sel, NEG_INF)
    dist = qpf[:, None] - pos_own.astype(jnp.float32)[None, :]
    s_own = jnp.einsum('bqhd,bnhd->bqhn', q, k_own, preferred_element_type=jnp.float32) * scale
    s_own = s_own - slopes[:, None] * dist[:, None, :]
    s_own = jnp.where(dist[:, None, :] >= 0, s_own, NEG_INF)
    p = jax.nn.softmax(jnp.concatenate([s_sel, s_own], axis=-1), axis=-1).astype(v_own.dtype)
    ns = k_sel.shape[3]
    return (jnp.einsum('bqhn,bqhnd->bqhd', p[..., :ns], v_sel)
            + jnp.einsum('bqhn,bnhd->bqhd', p[..., ns:], v_own))


def moba_prompt(q, k, v, slopes):
    B, T, H, D = q.shape
    nb = -(-T // MOBA_BLOCK)
    pad = nb * MOBA_BLOCK - T
    kpad = jnp.pad(k, ((0, 0), (0, pad), (0, 0), (0, 0)))
    vpad = jnp.pad(v, ((0, 0), (0, pad), (0, 0), (0, 0)))
    kb = kpad.reshape(B, nb, MOBA_BLOCK, H, D)
    vb = vpad.reshape(B, nb, MOBA_BLOCK, H, D)
    k_mean = jnp.mean(kb, axis=2, dtype=jnp.float32)
    q_blk = jnp.arange(T, dtype=jnp.int32) // MOBA_BLOCK
    gate = jnp.einsum('bthd,bnhd->bthn', q.astype(jnp.float32), k_mean)
    fully_past = jnp.arange(nb, dtype=jnp.int32)[None, :] < q_blk[:, None]
    gate = jnp.where(fully_past[None, :, None, :], gate, NEG_INF)
    n_sel = min(MOBA_TOPK, nb)
    _, idx = lax.top_k(gate, n_sel)
    bi = jnp.arange(B)[:, None, None, None]
    hi = jnp.arange(H)[None, None, :, None]
    offs = jnp.arange(MOBA_BLOCK, dtype=jnp.int32)

    def one_query_block(c):
        start = c * Q_BLOCK
        own = start // MOBA_BLOCK
        q_c = lax.dynamic_slice_in_dim(q, start, Q_BLOCK, axis=1)
        i_c = lax.dynamic_slice_in_dim(idx, start, Q_BLOCK, axis=1)
        k_sel = kb[bi, i_c, :, hi].reshape(B, Q_BLOCK, H, n_sel * MOBA_BLOCK, D)
        v_sel = vb[bi, i_c, :, hi].reshape(B, Q_BLOCK, H, n_sel * MOBA_BLOCK, D)
        pos_sel = (i_c[..., None] * MOBA_BLOCK + offs).reshape(B, Q_BLOCK, H, n_sel * MOBA_BLOCK)
        valid_sel = jnp.repeat(i_c < own, MOBA_BLOCK, axis=-1)
        k_own = lax.dynamic_slice_in_dim(kpad, own * MOBA_BLOCK, MOBA_BLOCK, axis=1)
        v_own = lax.dynamic_slice_in_dim(vpad, own * MOBA_BLOCK, MOBA_BLOCK, axis=1)
        q_pos = start + jnp.arange(Q_BLOCK, dtype=jnp.int32)
        return moba_softmax(q_c, k_sel, v_sel, pos_sel, valid_sel, k_own, v_own,
                            own * MOBA_BLOCK + offs, q_pos, slopes)

    o = lax.map(one_query_block, jnp.arange(T // Q_BLOCK, dtype=jnp.int32))
    return o.transpose(1, 0, 2, 3, 4).reshape(B, T, H, D)


def moba_sample(q, k_new, v_new, cache_k, cache_v, page_table, slopes):
    DB, S, H, D = q.shape
    n_pages = page_table.shape[1]
    past = n_pages * PAGE_SIZE
    ppb = MOBA_BLOCK // PAGE_SIZE
    nbf = past // MOBA_BLOCK
    own_start = nbf * MOBA_BLOCK
    q_pos = past + jnp.arange(S, dtype=jnp.int32)
    k_past = cache_k[page_table].reshape(DB, past, H, D)
    n_sel = min(MOBA_TOPK, nbf)
    if n_sel > 0:
        k_mean = jnp.mean(k_past[:, :own_start].reshape(DB, nbf, MOBA_BLOCK, H, D),
                          axis=2, dtype=jnp.float32)
        gate = jnp.einsum('bshd,bnhd->bshn', q.astype(jnp.float32), k_mean)
        _, idx = lax.top_k(gate, n_sel)
        logical = idx[..., None] * ppb + jnp.arange(ppb, dtype=jnp.int32)
        phys = page_table[jnp.arange(DB)[:, None, None, None, None], logical]
        hi = jnp.arange(H)[None, None, :, None, None]
        k_sel = cache_k[phys, :, hi].reshape(DB, S, H, n_sel * MOBA_BLOCK, D)
        v_sel = cache_v[phys, :, hi].reshape(DB, S, H, n_sel * MOBA_BLOCK, D)
        pos_sel = (idx[..., None] * MOBA_BLOCK
                   + jnp.arange(MOBA_BLOCK, dtype=jnp.int32)).reshape(DB, S, H, n_sel * MOBA_BLOCK)
    else:
        k_sel = jnp.zeros((DB, S, H, 0, D), q.dtype)
        v_sel = jnp.zeros((DB, S, H, 0, D), q.dtype)
        pos_sel = jnp.zeros((DB, S, H, 0), jnp.int32)
    valid_sel = jnp.ones(pos_sel.shape, dtype=bool)
    r = past - own_start
    k_own = jnp.concatenate([k_past[:, own_start:], k_new], axis=1)
    v_own_past = cache_v[page_table[:, own_start // PAGE_SIZE:]].reshape(DB, r, H, D)
    v_own = jnp.concatenate([v_own_past, v_new], axis=1)
    pos_own = own_start + jnp.arange(r + S, dtype=jnp.int32)
    return moba_softmax(q, k_sel, v_sel, pos_sel, valid_sel, k_own, v_own, pos_own, q_pos, slopes)


def hier_moe(h, w_rg, b_rg, w_re, b_re, w_gate, w_up, w_down):
    B, T, Dm = h.shape
    t = h.reshape(B * T, Dm)
    g_logits = jnp.dot(t, w_rg, preferred_element_type=jnp.float32) + b_rg.astype(jnp.float32)
    g_prob = jax.nn.softmax(g_logits, axis=-1)
    g_idx = jnp.argmax(g_logits, axis=-1)
    g_gate = jnp.take_along_axis(g_prob, g_idx[:, None], axis=-1)
    e_logits = (jnp.dot(t, w_re, preferred_element_type=jnp.float32)
                + b_re.astype(jnp.float32)).reshape(-1, N_EXPERT_GROUPS, EXPERTS_PER_GROUP)
    e_in = jnp.take_along_axis(e_logits, g_idx[:, None, None], axis=1)[:, 0]
    top_v, top_i = lax.top_k(e_in, TOP_E)
    top_w = jax.nn.softmax(top_v, axis=-1)
    e_w = jnp.sum(jax.nn.one_hot(top_i, EXPERTS_PER_GROUP, dtype=jnp.float32) * top_w[..., None], axis=1)
    gates = (jax.nn.one_hot(g_idx, N_EXPERT_GROUPS, dtype=jnp.float32)[:, :, None]
             * (g_gate[:, :, None] * e_w[:, None, :])).reshape(-1, N_EXPERTS)
    hid = jax.nn.silu(jnp.einsum('nd,edf->nef', t, w_gate)) * jnp.einsum('nd,edf->nef', t, w_up)
    hid = hid * gates.astype(hid.dtype)[:, :, None]
    y = jnp.einsum('nef,efd->nd', hid, w_down)
    return y.reshape(B, T, Dm)


def block(x, conv_state, attn_fn, norm_mix, w_in, w_dw, b_dw, conv_norm_g, conv_norm_b,
          attn_out_norm, w_out, norm_ffn, w_rg, b_rg, w_re, b_re, w_gate, w_up, w_down):
    B, T, _ = x.shape
    h = rmsnorm(x, norm_mix)
    q, k, v, glu = combined_projection(h, w_in)
    o = attn_fn(q, k, v)
    o = rmsnorm(o, attn_out_norm.reshape(N_HEADS, HEAD_DIM)).reshape(B, T, D_ATTN)
    c, conv_new = conformer_conv(glu, conv_state, w_dw, b_dw, conv_norm_g, conv_norm_b)
    x = x + jnp.concatenate([o, c], axis=-1) @ w_out
    x = x + hier_moe(rmsnorm(x, norm_ffn), w_rg, b_rg, w_re, b_re, w_gate, w_up, w_down)
    return x, k, v, conv_new


def setup_inputs(seed: int = 0) -> dict:
    key = jax.random.key(seed)
    ks = jax.random.split(key, 24)
    f32 = jnp.float32
    n_pages = PAST_LEN // PAGE_SIZE
    n_used = DEC_BATCH * n_pages
    n_phys = n_used + -(-n_used // 4)
    page_table = jax.random.permutation(ks[0], n_phys)[:n_used].reshape(DEC_BATCH, n_pages).astype(jnp.int32)

    def nrm(k, shape, s):
        return jax.random.normal(k, shape, f32) * s

    def gain(k, shape):
        return 1.0 + nrm(k, shape, 0.02)

    return {
        'x_prompt': nrm(ks[1], (BATCH, SEQ, D_MODEL), 1.0),
        'x_sample': nrm(ks[2], (DEC_BATCH, DEC_SEQ, D_MODEL), 1.0),
        'cache_k': nrm(ks[3], (DEPTH, n_phys, PAGE_SIZE, N_HEADS, HEAD_DIM), 1.0),
        'cache_v': nrm(ks[4], (DEPTH, n_phys, PAGE_SIZE, N_HEADS, HEAD_DIM), 1.0),
        'state_conv': nrm(ks[5], (DEPTH, DEC_BATCH, CONV_WIDTH - 1, D_CONV), 0.5),
        'page_table': page_table,
        'norm_mix': gain(ks[6], (DEPTH, D_MODEL)),
        'w_in': nrm(ks[7], (DEPTH, D_MODEL, D_IN), D_MODEL ** -0.5),
        'w_dw': nrm(ks[8], (DEPTH, CONV_WIDTH, D_CONV), CONV_WIDTH ** -0.5),
        'b_dw': nrm(ks[9], (DEPTH, D_CONV), 0.02),
        'conv_norm_g': gain(ks[10], (DEPTH, D_CONV)),
        'conv_norm_b': nrm(ks[11], (DEPTH, D_CONV), 0.02),
        'attn_out_norm': gain(ks[12], (DEPTH, D_ATTN)),
        'w_out': nrm(ks[13], (DEPTH, D_MIX, D_MODEL), D_MIX ** -0.5),
        'norm_ffn': gain(ks[14], (DEPTH, D_MODEL)),
        'w_router_group': nrm(ks[15], (DEPTH, D_MODEL, N_EXPERT_GROUPS), D_MODEL ** -0.5),
        'b_router_group': nrm(ks[16], (DEPTH, N_EXPERT_GROUPS), 0.01),
        'w_router_expert': nrm(ks[17], (DEPTH, D_MODEL, N_EXPERTS), D_MODEL ** -0.5),
        'b_router_expert': nrm(ks[18], (DEPTH, N_EXPERTS), 0.01),
        'w_gate': nrm(ks[19], (DEPTH, N_EXPERTS, D_MODEL, D_EXPERT), D_MODEL ** -0.5),
        'w_up': nrm(ks[20], (DEPTH, N_EXPERTS, D_MODEL, D_EXPERT), D_MODEL ** -0.5),
        'w_down': nrm(ks[21], (DEPTH, N_EXPERTS, D_EXPERT, D_MODEL), D_EXPERT ** -0.5),
        'norm_final': gain(ks[22], (D_MODEL,)),
    }


def reference(x_prompt, x_sample, cache_k, cache_v, state_conv, page_table, norm_mix, w_in, w_dw, b_dw,
              conv_norm_g, conv_norm_b, attn_out_norm, w_out, norm_ffn, w_router_group, b_router_group,
              w_router_expert, b_router_expert, w_gate, w_up, w_down, norm_final):
    slopes = alibi_slopes()
    hp, hs = x_prompt, x_sample
    B, T, _ = x_prompt.shape
    kp_list, vp_list, cp_list, ks_list, vs_list, cs_list = [], [], [], [], [], []
    for l in range(DEPTH):
        lw = (norm_mix[l], w_in[l], w_dw[l], b_dw[l], conv_norm_g[l], conv_norm_b[l], attn_out_norm[l],
              w_out[l], norm_ffn[l], w_router_group[l], b_router_group[l], w_router_expert[l],
              b_router_expert[l], w_gate[l], w_up[l], w_down[l])
        zero_conv = jnp.zeros((B, CONV_WIDTH - 1, D_CONV), x_prompt.dtype)
        hp, kp, vp, cp = block(hp, zero_conv, functools.partial(moba_prompt, slopes=slopes), *lw)
        hs, ksm, vsm, csm = block(hs, state_conv[l],
                                  functools.partial(moba_sample, cache_k=cache_k[l], cache_v=cache_v[l],
                                                    page_table=page_table, slopes=slopes), *lw)
        kp_list.append(kp.reshape(B, T // PAGE_SIZE, PAGE_SIZE, N_HEADS, HEAD_DIM))
        vp_list.append(vp.reshape(B, T // PAGE_SIZE, PAGE_SIZE, N_HEADS, HEAD_DIM))
        cp_list.append(cp)
        ks_list.append(ksm)
        vs_list.append(vsm)
        cs_list.append(csm)
    y_prompt = rmsnorm(hp, norm_final)
    y_sample = rmsnorm(hs, norm_final)
    k_prompt = jnp.stack(kp_list)
    v_prompt = jnp.stack(vp_list)
    conv_prompt = jnp.stack(cp_list)
    k_sample = jnp.stack(ks_list)
    v_sample = jnp.stack(vs_list)
    conv_sample = jnp.stack(cs_list)
    return (y_prompt, y_sample, k_prompt, v_prompt, conv_prompt, k_sample, v_sample, conv_sample)
```

```python
import functools

import jax
import jax.numpy as jnp
from jax import lax
from jax.experimental import pallas as pl
from jax.experimental.pallas import tpu as pltpu

F32 = jnp.float32
BF16 = jnp.bfloat16

N_HEADS = 8
HEAD_DIM = 128
D_ATTN = N_HEADS * HEAD_DIM
CONV_GROUPS = 8
CONV_WIDTH = 31
MOBA_BLOCK = 256
MOBA_TOPK = 3
N_EXPERT_GROUPS = 4
EXPERTS_PER_GROUP = 8
N_EXPERTS = N_EXPERT_GROUPS * EXPERTS_PER_GROUP
EPS = 1e-6
NEG_INF = -1e30

LANES = 128
HALO = 32
VMEM_LIMIT = 56 * 1024 * 1024


def _cparams(sem=None):
    return pltpu.CompilerParams(dimension_semantics=sem, vmem_limit_bytes=VMEM_LIMIT)


def _nt_dot(a, b, precision=None):
    return lax.dot_general(a, b, (((1,), (1,)), ((), ())), precision=precision,
                           preferred_element_type=F32)


def _inproj_kernel(x_ref, g_ref, w_ref, wgate_ref, q_ref, k_ref, v_ref, glu_ref, h_ref):
    j = pl.program_id(1)

    @pl.when(j == 0)
    def _():
        x = x_ref[...]
        r = lax.rsqrt(jnp.mean(x * x, axis=-1, keepdims=True) + EPS)
        h_ref[...] = ((x * r) * g_ref[...]).astype(BF16)

    z = jnp.dot(h_ref[...], w_ref[...], preferred_element_type=F32)

    @pl.when(j == 0)
    def _():
        q_ref[...] = z

    @pl.when(j == 1)
    def _():
        k_ref[...] = z

    @pl.when(j == 2)
    def _():
        v_ref[...] = z

    @pl.when(j == 3)
    def _():
        gate = jnp.dot(h_ref[...], wgate_ref[...], preferred_element_type=F32)
        glu_ref[...] = z * (1.0 / (1.0 + jnp.exp(-gate)))


def _in_projection(x, norm_g, w_in_bf, tm):
    n, d = x.shape
    dc = D_ATTN
    out = jax.ShapeDtypeStruct((n, dc), F32)
    row = pl.BlockSpec((tm, dc), lambda i, j: (i, 0))
    return pl.pallas_call(
        _inproj_kernel,
        grid=(n // tm, 4),
        in_specs=[
            pl.BlockSpec((tm, d), lambda i, j: (i, 0)),
            pl.BlockSpec((1, d), lambda i, j: (0, 0)),
            pl.BlockSpec((d, dc), lambda i, j: (0, j)),
            pl.BlockSpec((d, dc), lambda i, j: (0, 4)),
        ],
        out_specs=[row, row, row, row],
        out_shape=[out, out, out, out],
        scratch_shapes=[pltpu.VMEM((tm, d), BF16)],
        compiler_params=_cparams(("arbitrary", "arbitrary")),
        name="in_projection",
    )(x, norm_g, w_in_bf, w_in_bf)


def _conv_kernel(glu_ref, st_ref, w_ref, b_ref, gg_ref, gb_ref, c_ref, xin_ref, *, tt, rc, carry):
    t = pl.program_id(1)

    @pl.when(t == 0)
    def _():
        xin_ref[0:HALO, :] = st_ref[0]

    xin_ref[HALO:HALO + tt, :] = glu_ref[0]
    off = HALO - (CONV_WIDTH - 1)

    def lane_block(cb, _):
        cs = pl.ds(pl.multiple_of(cb * LANES, LANES), LANES)
        for r0 in range(0, tt, rc):
            acc = jnp.broadcast_to(b_ref[:, cs], (rc, LANES))
            for j in range(CONV_WIDTH):
                acc = acc + xin_ref[r0 + off + j:r0 + off + j + rc, cs] * w_ref[j:j + 1, cs]
            mu = jnp.mean(acc, axis=-1, keepdims=True)
            dlt = acc - mu
            var = jnp.mean(dlt * dlt, axis=-1, keepdims=True)
            yn = dlt * lax.rsqrt(var + EPS)
            yn = yn * gg_ref[:, cs] + gb_ref[:, cs]
            c_ref[0, r0:r0 + rc, cs] = (yn * (1.0 / (1.0 + jnp.exp(-yn)))).astype(c_ref.dtype)
        return 0

    lax.fori_loop(0, CONV_GROUPS, lane_block, 0)

    if carry:
        xin_ref[0:HALO, :] = xin_ref[tt:tt + HALO, :]


def _conformer_conv(glu, state_pad, w_dw, b_dw, gn_g, gn_b, tt, out_dtype):
    b, t, dc = glu.shape
    assert dc == CONV_GROUPS * LANES
    nt = t // tt
    rc = min(tt, 64)
    vec = pl.BlockSpec((1, dc), lambda i, j: (0, 0))
    return pl.pallas_call(
        functools.partial(_conv_kernel, tt=tt, rc=rc, carry=nt > 1),
        grid=(b, nt),
        in_specs=[
            pl.BlockSpec((1, tt, dc), lambda i, j: (i, j, 0)),
            pl.BlockSpec((1, HALO, dc), lambda i, j: (i, 0, 0)),
            pl.BlockSpec((CONV_WIDTH, dc), lambda i, j: (0, 0)),
            vec, vec, vec,
        ],
        out_specs=pl.BlockSpec((1, tt, dc), lambda i, j: (i, j, 0)),
        out_shape=jax.ShapeDtypeStruct((b, t, dc), out_dtype),
        scratch_shapes=[pltpu.VMEM((HALO + tt, dc), F32)],
        compiler_params=_cparams(("arbitrary", "arbitrary")),
        name="conformer_conv",
    )(glu, state_pad, w_dw, b_dw, gn_g, gn_b)


def _top3_mask(gate, valid, lane):
    g = jnp.where(valid, gate, NEG_INF)
    sel = jnp.zeros(gate.shape, jnp.bool_)
    for _ in range(MOBA_TOPK):
        m = jnp.max(g, axis=-1, keepdims=True)
        idx = jnp.min(jnp.where(g == m, lane, LANES), axis=-1, keepdims=True)
        pick = lane == idx
        sel = jnp.logical_or(sel, pick)
        g = jnp.where(pick, -jnp.inf, g)
    return jnp.logical_and(sel, valid)


def _moba_prompt_kernel(slopes_ref, q_ref, k_ref, v_ref, nrm_ref, o_ref, kb_ref, vb_ref, km_ref, rc_ref,
                        *, nblk):
    h = pl.program_id(1)
    qi = pl.program_id(2)
    blk = MOBA_BLOCK
    slope = slopes_ref[h]

    @pl.when(qi == 0)
    def _():
        kf = k_ref[0]
        kb_ref[...] = kf.astype(BF16)
        vb_ref[...] = v_ref[0].astype(BF16)
        km_ref[...] = jnp.zeros(km_ref.shape, F32)
        km_ref[0:nblk, :] = jnp.sum(kf.reshape(nblk, blk, HEAD_DIM), axis=1) * (1.0 / blk)
        r = lax.broadcasted_iota(jnp.int32, (blk, blk), 0)
        c = lax.broadcasted_iota(jnp.int32, (blk, blk), 1)
        rc_ref[...] = slope * (r - c).astype(F32)

    q = q_ref[0]
    gate = _nt_dot(q, km_ref[...], precision=lax.Precision.HIGHEST)
    lane = lax.broadcasted_iota(jnp.int32, gate.shape, 1)
    selv = jnp.where(_top3_mask(gate, lane < qi, lane), 1.0, 0.0)

    qs = (q * (HEAD_DIM ** -0.5)).astype(BF16)
    r = lax.broadcasted_iota(jnp.int32, (blk, blk), 0)
    c = lax.broadcasted_iota(jnp.int32, (blk, blk), 1)

    own = pl.ds(pl.multiple_of(qi * blk, blk), blk)
    s = _nt_dot(qs, kb_ref[own, :]) - rc_ref[...]
    s = jnp.where(r >= c, s, NEG_INF)
    m0 = jnp.max(s, axis=-1, keepdims=True)
    p = jnp.exp(s - m0)
    l0 = jnp.sum(p, axis=-1, keepdims=True)
    acc0 = jnp.dot(p.astype(BF16), vb_ref[own, :], preferred_element_type=F32)

    def past_block(j, carry):
        m, l, acc = carry
        rows = pl.ds(pl.multiple_of(j * blk, blk), blk)
        s = _nt_dot(qs, kb_ref[rows, :]) - rc_ref[...] - slope * ((qi - j) * blk).astype(F32)
        keep = jnp.sum(jnp.where(lane == j, selv, 0.0), axis=-1, keepdims=True) > 0.5
        s = jnp.where(keep, s, NEG_INF)
        m_new = jnp.maximum(m, jnp.max(s, axis=-1, keepdims=True))
        alpha = jnp.exp(m - m_new)
        p = jnp.exp(s - m_new)
        l = alpha * l + jnp.sum(p, axis=-1, keepdims=True)
        acc = alpha * acc + jnp.dot(p.astype(BF16), vb_ref[rows, :], preferred_element_type=F32)
        return m_new, l, acc

    _, l, acc = lax.fori_loop(0, qi, past_block, (m0, l0, acc0))
    o = acc / l
    o = (o * lax.rsqrt(jnp.mean(o * o, axis=-1, keepdims=True) + EPS)) * nrm_ref[...]
    o_ref[0] = o.astype(o_ref.dtype)


def _moba_prompt(q, k, v, slopes, attn_norm):
    b, t, _ = q.shape
    blk = MOBA_BLOCK
    nblk = t // blk
    assert t % blk == 0 and nblk <= LANES
    seq = pl.BlockSpec((1, t, HEAD_DIM), lambda bi, h, qi: (bi, 0, h))
    tile = pl.BlockSpec((1, blk, HEAD_DIM), lambda bi, h, qi: (bi, qi, h))
    return pl.pallas_call(
        functools.partial(_moba_prompt_kernel, nblk=nblk),
        grid=(b, N_HEADS, nblk),
        in_specs=[
            pl.BlockSpec(memory_space=pltpu.SMEM),
            tile, seq, seq,
            pl.BlockSpec((1, HEAD_DIM), lambda bi, h, qi: (0, h)),
        ],
        out_specs=tile,
        out_shape=jax.ShapeDtypeStruct((b, t, D_ATTN), BF16),
        scratch_shapes=[
            pltpu.VMEM((t, HEAD_DIM), BF16),
            pltpu.VMEM((t, HEAD_DIM), BF16),
            pltpu.VMEM((LANES, HEAD_DIM), F32),
            pltpu.VMEM((blk, blk), F32),
        ],
        compiler_params=_cparams(("arbitrary", "arbitrary", "arbitrary")),
        name="moba_prompt",
    )(slopes, q, k, v, attn_norm)


def _kmean_kernel(pt_ref, ck_ref, out_ref, buf, sem, *, n_chunks, chunks_per_b, pg):
    ppb = MOBA_BLOCK // buf.shape[2]

    def copy(page, slot, i):
        return pltpu.make_async_copy(ck_ref.at[page], buf.at[slot, i], sem.at[slot])

    def start(c, slot):
        for i in range(pg):
            copy(pt_ref[c * pg + i], slot, i).start()

    start(0, 0)

    def body(c, _):
        slot = c % 2

        @pl.when(c + 1 < n_chunks)
        def _():
            start(c + 1, 1 - slot)

        for i in range(pg):
            copy(0, slot, i).wait()
        bi = c // chunks_per_b
        blk0 = (c % chunks_per_b) * (pg // ppb)
        for i in range(pg // ppb):
            s = jnp.sum(buf[slot, ppb * i], axis=0)
            for p in range(1, ppb):
                s = s + jnp.sum(buf[slot, ppb * i + p], axis=0)
            out_ref[bi, blk0 + i] = s * (1.0 / MOBA_BLOCK)
        return 0

    lax.fori_loop(0, n_chunks, body, 0)


def _cache_block_means(page_table, cache_k):
    db, n_pages = page_table.shape
    _, page, nh, hd = cache_k.shape
    ppb = MOBA_BLOCK // page
    pg = 8
    assert n_pages % pg == 0 and pg % ppb == 0
    chunks_per_b = n_pages // pg
    return pl.pallas_call(
        functools.partial(_kmean_kernel, n_chunks=db * chunks_per_b, chunks_per_b=chunks_per_b, pg=pg),
        grid_spec=pltpu.PrefetchScalarGridSpec(
            num_scalar_prefetch=1,
            grid=(1,),
            in_specs=[pl.BlockSpec(memory_space=pl.ANY)],
            out_specs=pl.BlockSpec((db, n_pages // ppb, nh, hd), lambda i, pt: (0, 0, 0, 0)),
            scratch_shapes=[
                pltpu.VMEM((2, pg, page, nh, hd), F32),
                pltpu.SemaphoreType.DMA((2,)),
            ],
        ),
        out_shape=jax.ShapeDtypeStruct((db, n_pages // ppb, nh, hd), F32),
        compiler_params=_cparams(("arbitrary",)),
        name="cache_block_means",
    )(page_table.reshape(-1), cache_k)


def _sample_topk_kernel(q_ref, km_ref, idx_ref, *, nbf):
    for h in range(N_HEADS):
        gate = _nt_dot(q_ref[0, h], km_ref[0, h], precision=lax.Precision.HIGHEST)
        lane = lax.broadcasted_iota(jnp.int32, gate.shape, 1)
        g = jnp.where(lane < nbf, gate, -jnp.inf)
        out = jnp.zeros(gate.shape, jnp.int32)
        for r in range(MOBA_TOPK):
            m = jnp.max(g, axis=-1, keepdims=True)
            idx = jnp.min(jnp.where(g == m, lane, LANES), axis=-1, keepdims=True)
            out = jnp.where(lane == r, idx, out)
            g = jnp.where(lane == idx, -jnp.inf, g)
        idx_ref[0, h] = out


def _sample_topk(q_t, kmean_t, nbf):
    db, nh, s, hd = q_t.shape
    return pl.pallas_call(
        functools.partial(_sample_topk_kernel, nbf=nbf),
        grid=(db,),
        in_specs=[
            pl.BlockSpec((1, nh, s, hd), lambda i: (i, 0, 0, 0)),
            pl.BlockSpec((1, nh, LANES, hd), lambda i: (i, 0, 0, 0)),
        ],
        out_specs=pl.BlockSpec((1, nh, s, LANES), lambda i: (i, 0, 0, 0)),
        out_shape=jax.ShapeDtypeStruct((db, nh, s, LANES), jnp.int32),
        compiler_params=_cparams(("arbitrary",)),
        name="sample_topk",
    )(q_t, kmean_t)


def _moba_sample_kernel(pages_ref, blk_ref, slopes_ref, q_ref, kn_ref, vn_ref, nrm_ref, ck_ref, cv_ref, o_ref,
                        kbuf, vbuf, sem, *, n_units, n_sel, ppb, past):
    page = kbuf.shape[2]
    n_slab = n_sel * ppb
    n_keys = n_slab * page
    s_len = q_ref.shape[2]
    rows = 16

    def copies(u, slot, i, h, pg):
        return (pltpu.make_async_copy(ck_ref.at[pg, :, h, :], kbuf.at[slot, i], sem.at[0, slot]),
                pltpu.make_async_copy(cv_ref.at[pg, :, h, :], vbuf.at[slot, i], sem.at[1, slot]))

    def start(u, slot):
        h = u % N_HEADS
        for i in range(n_slab):
            ck, cv = copies(u, slot, i, h, pages_ref[u * n_slab + i])
            ck.start()
            cv.start()

    start(0, 0)

    def unit(u, _):
        slot = u % 2
        bi = u // N_HEADS
        h = u % N_HEADS

        @pl.when(u + 1 < n_units)
        def _():
            start(u + 1, 1 - slot)

        for i in range(n_slab):
            ck, cv = copies(u, slot, i, 0, 0)
            ck.wait()
            cv.wait()

        slope = slopes_ref[h]
        q = q_ref[bi, h]
        qs = jnp.concatenate([q * (HEAD_DIM ** -0.5), jnp.zeros((rows - s_len, HEAD_DIM), F32)], axis=0)
        qs = qs.astype(BF16)
        kb = kbuf[slot].reshape(n_keys, HEAD_DIM).astype(BF16)
        vb = vbuf[slot].reshape(n_keys, HEAD_DIM).astype(BF16)
        s = _nt_dot(qs, kb)

        per_blk = page * ppb
        offs = lax.broadcasted_iota(jnp.int32, (1, per_blk), 1)
        kpos = jnp.concatenate(
            [(blk_ref[u * n_sel + i] * per_blk + offs).astype(F32) for i in range(n_sel)], axis=1)
        row = lax.broadcasted_iota(jnp.int32, (rows, 1), 0)
        qpos = (past + row).astype(F32)
        s = s - slope * (qpos - kpos)
        col = lax.broadcasted_iota(jnp.int32, (rows, n_keys), 1)
        per_q = MOBA_TOPK * per_blk
        mine = jnp.logical_and(col >= row * per_q, col < (row + 1) * per_q)
        s = jnp.where(mine, s, NEG_INF)

        kn = jnp.concatenate([kn_ref[bi, h], jnp.zeros((rows - s_len, HEAD_DIM), F32)], axis=0).astype(BF16)
        vn = jnp.concatenate([vn_ref[bi, h], jnp.zeros((rows - s_len, HEAD_DIM), F32)], axis=0).astype(BF16)
        ri = lax.broadcasted_iota(jnp.int32, (rows, rows), 0)
        ci = lax.broadcasted_iota(jnp.int32, (rows, rows), 1)
        so = _nt_dot(qs, kn) - slope * (ri - ci).astype(F32)
        so = jnp.where(jnp.logical_and(ri >= ci, ci < s_len), so, NEG_INF)

        m = jnp.maximum(jnp.max(s, axis=-1, keepdims=True), jnp.max(so, axis=-1, keepdims=True))
        p = jnp.exp(s - m)
        po = jnp.exp(so - m)
        l = jnp.sum(p, axis=-1, keepdims=True) + jnp.sum(po, axis=-1, keepdims=True)
        acc = (jnp.dot(p.astype(BF16), vb, preferred_element_type=F32)
               + jnp.dot(po.astype(BF16), vn, preferred_element_type=F32))
        o = (acc / l)[0:s_len]
        o = (o * lax.rsqrt(jnp.mean(o * o, axis=-1, keepdims=True) + EPS)) * nrm_ref[pl.ds(h, 1), :]
        o_ref[bi, h] = o
        return 0

    lax.fori_loop(0, n_units, unit, 0)


def _moba_sample(pages, blks, slopes, q_t, kn_t, vn_t, attn_norm_hd, cache_k, cache_v, past):
    db, nh, s, hd = q_t.shape
    _, page, _, _ = cache_k.shape
    ppb = MOBA_BLOCK // page
    n_sel = s * MOBA_TOPK
    whole = lambda shp: pl.BlockSpec(shp, lambda i, *_: (0,) * len(shp))
    return pl.pallas_call(
        functools.partial(_moba_sample_kernel, n_units=db * nh, n_sel=n_sel, ppb=ppb, past=past),
        grid_spec=pltpu.PrefetchScalarGridSpec(
            num_scalar_prefetch=3,
            grid=(1,),
            in_specs=[
                whole(q_t.shape), whole(kn_t.shape), whole(vn_t.shape), whole(attn_norm_hd.shape),
                pl.BlockSpec(memory_space=pl.ANY),
                pl.BlockSpec(memory_space=pl.ANY),
            ],
            out_specs=whole(q_t.shape),
            scratch_shapes=[
                pltpu.VMEM((2, n_sel * ppb, page, hd), F32),
                pltpu.VMEM((2, n_sel * ppb, page, hd), F32),
                pltpu.SemaphoreType.DMA((2, 2)),
            ],
        ),
        out_shape=jax.ShapeDtypeStruct(q_t.shape, F32),
        compiler_params=_cparams(("arbitrary",)),
        name="moba_sample",
    )(pages, blks, slopes, q_t, kn_t, vn_t, attn_norm_hd, cache_k, cache_v)


def _outproj_kernel(o_ref, c_ref, x_ref, wo_ref, wc_ref, g_ref, wr_ref, br_ref, x2_ref, h2_ref, route_ref):
    mix = (jnp.dot(o_ref[...].astype(BF16), wo_ref[...], preferred_element_type=F32)
           + jnp.dot(c_ref[...].astype(BF16), wc_ref[...], preferred_element_type=F32))
    x2 = x_ref[...] + mix
    x2_ref[...] = x2
    r = lax.rsqrt(jnp.mean(x2 * x2, axis=-1, keepdims=True) + EPS)
    h2 = (x2 * r) * g_ref[...]
    h2_ref[...] = h2

    h_hi = h2.astype(BF16)
    h_lo = (h2 - h_hi.astype(F32)).astype(BF16)
    r1 = jnp.dot(h_hi, wr_ref[...], preferred_element_type=F32)
    r2 = jnp.dot(h_lo, wr_ref[:, 0:LANES], preferred_element_type=F32)
    logits = r1[:, 0:LANES] + r1[:, LANES:2 * LANES] + r2 + br_ref[...]

    lane = lax.broadcasted_iota(jnp.int32, logits.shape, 1)
    is_g = lane < N_EXPERT_GROUPS
    gl = jnp.where(is_g, logits, -jnp.inf)
    gmax = jnp.max(gl, axis=-1, keepdims=True)
    gidx = jnp.min(jnp.where(gl == gmax, lane, LANES), axis=-1, keepdims=True)
    gsum = jnp.sum(jnp.where(is_g, jnp.exp(gl - gmax), 0.0), axis=-1, keepdims=True)
    g_gate = 1.0 / gsum

    lo = N_EXPERT_GROUPS + gidx * EXPERTS_PER_GROUP
    el = jnp.where(jnp.logical_and(lane >= lo, lane < lo + EXPERTS_PER_GROUP), logits, -jnp.inf)
    v0 = jnp.max(el, axis=-1, keepdims=True)
    i0 = jnp.min(jnp.where(el == v0, lane, LANES), axis=-1, keepdims=True)
    el = jnp.where(lane == i0, -jnp.inf, el)
    v1 = jnp.max(el, axis=-1, keepdims=True)
    i1 = jnp.min(jnp.where(el == v1, lane, LANES), axis=-1, keepdims=True)
    e = jnp.exp(v1 - v0)
    w0 = g_gate / (1.0 + e)
    w1 = g_gate * e / (1.0 + e)

    out = jnp.where(lane == 0, (i0 - N_EXPERT_GROUPS).astype(F32), 0.0)
    out = jnp.where(lane == 1, (i1 - N_EXPERT_GROUPS).astype(F32), out)
    out = jnp.where(lane == 2, w0, out)
    out = jnp.where(lane == 3, w1, out)
    route_ref[...] = out


def _out_projection(o, c, x, wo, wc, norm_g, wr, br, tm):
    n, d = x.shape
    row = lambda w: pl.BlockSpec((tm, w), lambda i: (i, 0))
    full = lambda a: pl.BlockSpec(a.shape, lambda i: (0, 0))
    return pl.pallas_call(
        _outproj_kernel,
        grid=(n // tm,),
        in_specs=[row(o.shape[1]), row(c.shape[1]), row(d), full(wo), full(wc), full(norm_g), full(wr), full(br)],
        out_specs=[row(d), row(d), row(LANES)],
        out_shape=[jax.ShapeDtypeStruct((n, d), F32), jax.ShapeDtypeStruct((n, d), F32),
                   jax.ShapeDtypeStruct((n, LANES), F32)],
        compiler_params=_cparams(("arbitrary",)),
        name="out_projection_router",
    )(o, c, x, wo, wc, norm_g, wr, br)


def _moe_kernel(order_ref, offs_ref, h_ref, wg_ref, wu_ref, wd_ref, out_ref,
                xbuf, obuf, wgb, wub, wdb, sem, *, ch):
    e = pl.program_id(0)

    @pl.when(e == 0)
    def _():
        xbuf[...] = jnp.zeros(xbuf.shape, xbuf.dtype)

    wgb[...] = wg_ref[0].astype(BF16)
    wub[...] = wu_ref[0].astype(BF16)
    wdb[...] = wd_ref[0].astype(BF16)

    start = offs_ref[e]
    cnt = offs_ref[e + 1] - start
    n_ch = (cnt + ch - 1) // ch

    def gather(i, base):
        pair = order_ref[base + i]
        return pltpu.make_async_copy(h_ref.at[pl.ds(pair // 2, 1)], xbuf.at[pl.ds(i, 1)], sem.at[0])

    def scatter(i, base):
        pair = order_ref[base + i]
        return pltpu.make_async_copy(obuf.at[pl.ds(i, 1)], out_ref.at[pl.ds(pair, 1)], sem.at[1])

    def chunk(c, _):
        base = start + c * ch
        nv = jnp.minimum(ch, cnt - c * ch)

        def g_start(i, _):
            gather(i, base).start()
            return 0

        def g_wait(i, _):
            gather(i, base).wait()
            return 0

        lax.fori_loop(0, nv, g_start, 0)
        lax.fori_loop(0, nv, g_wait, 0)

        x = xbuf[...].astype(BF16)
        g = jnp.dot(x, wgb[...], preferred_element_type=F32)
        u = jnp.dot(x, wub[...], preferred_element_type=F32)
        hid = (g * (1.0 / (1.0 + jnp.exp(-g)))) * u
        obuf[...] = jnp.dot(hid.astype(BF16), wdb[...], preferred_element_type=F32)

        def s_start(i, _):
            scatter(i, base).start()
            return 0

        def s_wait(i, _):
            scatter(i, base).wait()
            return 0

        lax.fori_loop(0, nv, s_start, 0)
        lax.fori_loop(0, nv, s_wait, 0)
        return 0

    lax.fori_loop(0, n_ch, chunk, 0)


def _moe(order, offs, h2, w_gate, w_up, w_down, ch):
    n, d = h2.shape
    ne, _, de = w_gate.shape
    return pl.pallas_call(
        functools.partial(_moe_kernel, ch=ch),
        grid_spec=pltpu.PrefetchScalarGridSpec(
            num_scalar_prefetch=2,
            grid=(ne,),
            in_specs=[
                pl.BlockSpec(memory_space=pl.ANY),
                pl.BlockSpec((1, d, de), lambda e, *_: (e, 0, 0)),
                pl.BlockSpec((1, d, de), lambda e, *_: (e, 0, 0)),
                pl.BlockSpec((1, de, d), lambda e, *_: (e, 0, 0)),
            ],
            out_specs=pl.BlockSpec(memory_space=pl.ANY),
            scratch_shapes=[
                pltpu.VMEM((ch, d), F32),
                pltpu.VMEM((ch, d), F32),
                pltpu.VMEM((d, de), BF16),
                pltpu.VMEM((d, de), BF16),
                pltpu.VMEM((de, d), BF16),
                pltpu.SemaphoreType.DMA((2,)),
            ],
        ),
        out_shape=jax.ShapeDtypeStruct((2 * n, d), F32),
        compiler_params=_cparams(("arbitrary",)),
        name="expert_mlp",
    )(order, offs, h2, w_gate, w_up, w_down)


def _final_kernel(x2_ref, y2_ref, route_ref, g_ref, y_ref):
    d = x2_ref.shape[1]
    rt = route_ref[...]
    x3 = x2_ref[...] + rt[:, 2:3] * y2_ref[:, 0:d] + rt[:, 3:4] * y2_ref[:, d:2 * d]
    r = lax.rsqrt(jnp.mean(x3 * x3, axis=-1, keepdims=True) + EPS)
    y_ref[...] = (x3 * r) * g_ref[...]


def _final(x2, y2, route, norm_g, row0, tm):
    n, d = x2.shape
    b0 = row0 // tm
    return pl.pallas_call(
        _final_kernel,
        grid=(n // tm,),
        in_specs=[
            pl.BlockSpec((tm, d), lambda i: (i, 0)),
            pl.BlockSpec((tm, 2 * d), lambda i: (i + b0, 0)),
            pl.BlockSpec((tm, LANES), lambda i: (i, 0)),
            pl.BlockSpec((1, d), lambda i: (0, 0)),
        ],
        out_specs=pl.BlockSpec((tm, d), lambda i: (i, 0)),
        out_shape=jax.ShapeDtypeStruct((n, d), F32),
        compiler_params=_cparams(("arbitrary",)),
        name="combine_final_norm",
    )(x2, y2, route, norm_g)


def _pick_tile(n, pref):
    t = min(n, pref)
    while n % t:
        t //= 2
    return t


def kernel(x_prompt, x_sample, cache_k, cache_v, state_conv, page_table, norm_mix, w_in, w_dw, b_dw,
           conv_norm_g, conv_norm_b, attn_out_norm, w_out, norm_ffn, w_router_group, b_router_group,
           w_router_expert, b_router_expert, w_gate, w_up, w_down, norm_final):
    depth = norm_mix.shape[0]
    assert depth == 1
    b, t, d = x_prompt.shape
    db, s, _ = x_sample.shape
    n_pages = page_table.shape[1]
    page = cache_k.shape[2]
    past = n_pages * page
    nbf = past // MOBA_BLOCK
    assert past % MOBA_BLOCK == 0 and nbf >= MOBA_TOPK and nbf <= LANES
    ppb = MOBA_BLOCK // page
    d_conv = d - D_ATTN
    assert d_conv == D_ATTN
    np_, ns_ = b * t, db * s
    ntot = np_ + ns_

    slopes = jnp.exp2(-8.0 * jnp.arange(1, N_HEADS + 1, dtype=F32) / N_HEADS)
    w_in_bf = w_in[0].astype(BF16)
    wo_bf = w_out[0, :D_ATTN].astype(BF16)
    wc_bf = w_out[0, D_ATTN:].astype(BF16)
    wr = jnp.zeros((d, LANES), F32)
    wr = wr.at[:, :N_EXPERT_GROUPS].set(w_router_group[0])
    wr = wr.at[:, N_EXPERT_GROUPS:N_EXPERT_GROUPS + N_EXPERTS].set(w_router_expert[0])
    wr_hi = wr.astype(BF16)
    wr_lo = (wr - wr_hi.astype(F32)).astype(BF16)
    wr_cat = jnp.concatenate([wr_hi, wr_lo], axis=1)
    br = jnp.zeros((1, LANES), F32)
    br = br.at[0, :N_EXPERT_GROUPS].set(b_router_group[0])
    br = br.at[0, N_EXPERT_GROUPS:N_EXPERT_GROUPS + N_EXPERTS].set(b_router_expert[0])
    g_mix = norm_mix[0].reshape(1, d)
    g_ffn = norm_ffn[0].reshape(1, d)
    g_fin = norm_final.reshape(1, d)
    attn_norm = attn_out_norm[0].reshape(1, D_ATTN)
    conv_vec = lambda a: a[0].reshape(1, d_conv)

    xp = x_prompt.reshape(np_, d)
    xs = x_sample.reshape(ns_, d)
    qp, kp, vp, glu_p = _in_projection(xp, g_mix, w_in_bf, _pick_tile(np_, 512))
    qs, ks, vs, glu_s = _in_projection(xs, g_mix, w_in_bf, _pick_tile(ns_, 256))

    glu_p3 = glu_p.reshape(b, t, d_conv)
    glu_s3 = glu_s.reshape(db, s, d_conv)
    pad = HALO - (CONV_WIDTH - 1)
    st_p = jnp.zeros((b, HALO, d_conv), F32)
    st_s = jnp.pad(state_conv[0], ((0, 0), (pad, 0), (0, 0)))
    conv_args = (w_dw[0], conv_vec(b_dw), conv_vec(conv_norm_g), conv_vec(conv_norm_b))
    c_p = _conformer_conv(glu_p3, st_p, *conv_args, tt=_pick_tile(t, 256), out_dtype=BF16)
    c_s = _conformer_conv(glu_s3, st_s, *conv_args, tt=s, out_dtype=F32)
    conv_prompt = glu_p3[:, t - (CONV_WIDTH - 1):][None]
    conv_sample = jnp.concatenate([state_conv[0], glu_s3], axis=1)[:, -(CONV_WIDTH - 1):][None]

    o_p = _moba_prompt(qp.reshape(b, t, D_ATTN), kp.reshape(b, t, D_ATTN), vp.reshape(b, t, D_ATTN),
                       slopes, attn_norm)

    ck = cache_k[0]
    cv = cache_v[0]
    kmean = _cache_block_means(page_table, ck)
    kmean_t = jnp.pad(kmean.transpose(0, 2, 1, 3), ((0, 0), (0, 0), (0, LANES - nbf), (0, 0)))
    to_heads = lambda a: a.reshape(db, s, N_HEADS, HEAD_DIM).transpose(0, 2, 1, 3)
    q_t, kn_t, vn_t = to_heads(qs), to_heads(ks), to_heads(vs)
    idx = _sample_topk(q_t, kmean_t, nbf)[..., :MOBA_TOPK]
    logical = idx[..., None] * ppb + jnp.arange(ppb, dtype=jnp.int32)
    pages = jnp.take_along_axis(page_table, logical.reshape(db, -1), axis=1)
    o_t = _moba_sample(pages.reshape(-1), idx.reshape(-1), slopes, q_t, kn_t, vn_t,
                       attn_norm.reshape(N_HEADS, HEAD_DIM), ck, cv, past)
    o_s = o_t.transpose(0, 2, 1, 3).reshape(ns_, D_ATTN)

    x2_p, h2_p, route_p = _out_projection(o_p.reshape(np_, D_ATTN), c_p.reshape(np_, d_conv), xp,
                                          wo_bf, wc_bf, g_ffn, wr_cat, br, _pick_tile(np_, 512))
    x2_s, h2_s, route_s = _out_projection(o_s, c_s.reshape(ns_, d_conv), xs,
                                          wo_bf, wc_bf, g_ffn, wr_cat, br, _pick_tile(ns_, 256))

    ch = 256
    h2 = jnp.concatenate([h2_p, h2_s], axis=0)
    eid = jnp.concatenate([route_p[:, 0:2], route_s[:, 0:2]], axis=0).astype(jnp.int32).reshape(-1)
    order = jnp.argsort(eid, stable=True).astype(jnp.int32)
    counts = jnp.sum(eid[:, None] == jnp.arange(N_EXPERTS, dtype=jnp.int32)[None, :], axis=0, dtype=jnp.int32)
    offs = jnp.concatenate([jnp.zeros((1,), jnp.int32), jnp.cumsum(counts, dtype=jnp.int32)])
    y2 = _moe(order, offs, h2, w_gate[0], w_up[0], w_down[0], ch).reshape(ntot, 2 * d)

    tm_f = _pick_tile(ns_, 256)
    assert np_ % tm_f == 0
    y_p = _final(x2_p, y2, route_p, g_fin, 0, tm_f)
    y_s = _final(x2_s, y2, route_s, g_fin, np_, tm_f)

    page_shape = (depth, b, t // page, page, N_HEADS, HEAD_DIM)
    return (y_p.reshape(b, t, d), y_s.reshape(db, s, d),
            kp.reshape(page_shape), vp.reshape(page_shape), conv_prompt,
            ks.reshape(depth, db, s, N_HEADS, HEAD_DIM), vs.reshape(depth, db, s, N_HEADS, HEAD_DIM),
            conv_sample)
```

```python
import functools

import jax
import jax.numpy as jnp
from jax import lax
from jax.experimental import pallas as pl
from jax.experimental.pallas import tpu as pltpu

F32 = jnp.float32
BF16 = jnp.bfloat16

N_HEADS = 8
HEAD_DIM = 128
D_ATTN = N_HEADS * HEAD_DIM
CONV_GROUPS = 8
CONV_WIDTH = 31
MOBA_BLOCK = 256
MOBA_TOPK = 3
N_EXPERT_GROUPS = 4
EXPERTS_PER_GROUP = 8
N_EXPERTS = N_EXPERT_GROUPS * EXPERTS_PER_GROUP
EPS = 1e-6
NEG_INF = -1e30

LANES = 128
SUBLANES = 8
HALO = 32
VMEM_LIMIT = 56 * 1024 * 1024


def _cparams(sem=None):
    return pltpu.CompilerParams(dimension_semantics=sem, vmem_limit_bytes=VMEM_LIMIT)


def _nt_dot(a, b, precision=None):
    return lax.dot_general(a, b, (((1,), (1,)), ((), ())), precision=precision,
                           preferred_element_type=F32)


def _inproj_kernel(x_ref, g_ref, w_ref, wgate_ref, q_ref, k_ref, v_ref, glu_ref, h_ref):
    j = pl.program_id(1)

    @pl.when(j == 0)
    def _():
        x = x_ref[...]
        r = lax.rsqrt(jnp.mean(x * x, axis=-1, keepdims=True) + EPS)
        h_ref[...] = ((x * r) * g_ref[...]).astype(BF16)

    z = jnp.dot(h_ref[...], w_ref[...], preferred_element_type=F32)

    @pl.when(j == 0)
    def _():
        q_ref[...] = z

    @pl.when(j == 1)
    def _():
        k_ref[...] = z

    @pl.when(j == 2)
    def _():
        v_ref[...] = z

    @pl.when(j == 3)
    def _():
        gate = jnp.dot(h_ref[...], wgate_ref[...], preferred_element_type=F32)
        glu_ref[...] = z * (1.0 / (1.0 + jnp.exp(-gate)))


def _in_projection(x, norm_g, w_in_bf, tm):
    n, d = x.shape
    dc = D_ATTN
    out = jax.ShapeDtypeStruct((n, dc), F32)
    row = pl.BlockSpec((tm, dc), lambda i, j: (i, 0))
    return pl.pallas_call(
        _inproj_kernel,
        grid=(n // tm, 4),
        in_specs=[
            pl.BlockSpec((tm, d), lambda i, j: (i, 0)),
            pl.BlockSpec((1, d), lambda i, j: (0, 0)),
            pl.BlockSpec((d, dc), lambda i, j: (0, j)),
            pl.BlockSpec((d, dc), lambda i, j: (0, 4)),
        ],
        out_specs=[row, row, row, row],
        out_shape=[out, out, out, out],
        scratch_shapes=[pltpu.VMEM((tm, d), BF16)],
        compiler_params=_cparams(("arbitrary", "arbitrary")),
        name="in_projection",
    )(x, norm_g, w_in_bf, w_in_bf)


def _conv_kernel(glu_ref, st_ref, w_ref, b_ref, gg_ref, gb_ref, c_ref, xs_ref, *, tt, rc, nseq, carry, preshift):
    t = pl.program_id(1)
    off = HALO - (CONV_WIDTH - 1)
    span = HALO + tt - SUBLANES

    def lane_block(sq, cs):
        for r0 in range(0, tt, rc):
            acc = jnp.broadcast_to(b_ref[:, cs], (rc, LANES))
            for j in range(CONV_WIDTH):
                o = off + j
                if preshift:
                    lo = r0 + (o // SUBLANES) * SUBLANES
                    x = xs_ref[o % SUBLANES, lo:lo + rc, cs]
                else:
                    x = xs_ref[0, r0 + o:r0 + o + rc, cs]
                acc = acc + x * w_ref[j:j + 1, cs]
            mu = jnp.mean(acc, axis=-1, keepdims=True)
            dlt = acc - mu
            var = jnp.mean(dlt * dlt, axis=-1, keepdims=True)
            yn = dlt * lax.rsqrt(var + EPS)
            yn = yn * gg_ref[:, cs] + gb_ref[:, cs]
            c_ref[sq, r0:r0 + rc, cs] = (yn * (1.0 / (1.0 + jnp.exp(-yn)))).astype(c_ref.dtype)

    def one_seq(sq):
        if carry:
            @pl.when(t == 0)
            def _():
                xs_ref[0, 0:HALO, :] = st_ref[sq]
        else:
            xs_ref[0, 0:HALO, :] = st_ref[sq]
        xs_ref[0, HALO:HALO + tt, :] = glu_ref[sq]
        if preshift:
            for r in range(1, SUBLANES):
                xs_ref[r, 0:span, :] = xs_ref[0, r:r + span, :]

            def group(cb, carry_):
                lane_block(sq, pl.ds(pl.multiple_of(cb * LANES, LANES), LANES))
                return carry_

            lax.fori_loop(0, CONV_GROUPS, group, 0)
        else:
            for cb in range(CONV_GROUPS):
                lane_block(sq, pl.ds(cb * LANES, LANES))
        if carry:
            xs_ref[0, 0:HALO, :] = xs_ref[0, tt:tt + HALO, :]

    if nseq == 1:
        one_seq(0)
    else:
        def seq(sq, carry_):
            one_seq(sq)
            return carry_

        lax.fori_loop(0, nseq, seq, 0)


def _conformer_conv(glu, state_pad, w_dw, b_dw, gn_g, gn_b, tt, nseq, out_dtype):
    b, t, dc = glu.shape
    assert dc == CONV_GROUPS * LANES and b % nseq == 0 and t % tt == 0
    nt = t // tt
    assert nt == 1 or nseq == 1
    preshift = tt >= 4 * SUBLANES
    rc = min(tt, 64)
    vec = pl.BlockSpec((1, dc), lambda i, j: (0, 0))
    return pl.pallas_call(
        functools.partial(_conv_kernel, tt=tt, rc=rc, nseq=nseq, carry=nt > 1, preshift=preshift),
        grid=(b // nseq, nt),
        in_specs=[
            pl.BlockSpec((nseq, tt, dc), lambda i, j: (i, j, 0)),
            pl.BlockSpec((nseq, HALO, dc), lambda i, j: (i, 0, 0)),
            pl.BlockSpec((CONV_WIDTH, dc), lambda i, j: (0, 0)),
            vec, vec, vec,
        ],
        out_specs=pl.BlockSpec((nseq, tt, dc), lambda i, j: (i, j, 0)),
        out_shape=jax.ShapeDtypeStruct((b, t, dc), out_dtype),
        scratch_shapes=[pltpu.VMEM((SUBLANES if preshift else 1, HALO + tt, dc), F32)],
        compiler_params=_cparams(("arbitrary", "arbitrary")),
        name="conformer_conv",
    )(glu, state_pad, w_dw, b_dw, gn_g, gn_b)


def _moba_prompt_kernel(slopes_ref, q_ref, k_ref, v_ref, nrm_ref, o_ref,
                        qb_ref, kb_ref, vt_ref, g_ref, biasp_ref, biasd_ref, *, nblk, kpi):
    h = pl.program_id(1)
    qi = pl.program_id(2)
    blk = MOBA_BLOCK
    t = q_ref.shape[1]
    nb8 = -(-nblk // SUBLANES) * SUBLANES
    slope = slopes_ref[h]

    @pl.when(qi == 0)
    def _():
        q = q_ref[0]
        kf = k_ref[0]
        qb_ref[...] = (q * (HEAD_DIM ** -0.5)).astype(BF16)
        kb_ref[...] = kf.astype(BF16)
        vt_ref[...] = v_ref[0].T.astype(BF16)
        km = jnp.sum(kf.reshape(nblk, blk, HEAD_DIM), axis=1) * (1.0 / blk)
        if nb8 > nblk:
            km = jnp.concatenate([km, jnp.zeros((nb8 - nblk, HEAD_DIM), F32)], axis=0)
        gate = _nt_dot(km, q, precision=lax.Precision.HIGHEST)
        j_io = lax.broadcasted_iota(jnp.int32, gate.shape, 0)
        t_io = lax.broadcasted_iota(jnp.int32, gate.shape, 1)
        valid = j_io * blk + (blk - 1) < t_io - (t_io % blk)
        g = jnp.where(valid, gate, NEG_INF)
        sel = jnp.zeros(gate.shape, jnp.bool_)
        for _ in range(MOBA_TOPK):
            m = jnp.max(g, axis=0, keepdims=True)
            idx = jnp.min(jnp.where(g == m, j_io, nb8), axis=0, keepdims=True)
            pick = j_io == idx
            sel = jnp.logical_or(sel, pick)
            g = jnp.where(pick, -jnp.inf, g)
        gb = (slope * blk) * j_io.astype(F32) + jnp.where(jnp.logical_and(sel, valid), 0.0, NEG_INF)
        for j in range(nblk):
            g_ref[j] = jnp.broadcast_to(gb[j:j + 1, :], (SUBLANES, t))
        c = lax.broadcasted_iota(jnp.int32, (kpi * blk, blk), 0)
        biasp_ref[...] = slope * (c % blk).astype(F32)
        c = lax.broadcasted_iota(jnp.int32, (blk, blk), 0)
        r = lax.broadcasted_iota(jnp.int32, (blk, blk), 1)
        biasd_ref[...] = slope * c.astype(F32) + jnp.where(c > r, NEG_INF, 0.0)

    own = pl.ds(pl.multiple_of(qi * blk, blk), blk)
    qs = qb_ref[own, :]

    g_own = (slope * blk) * qi.astype(F32)
    st = _nt_dot(kb_ref[own, :], qs) + biasd_ref[...]
    m0 = jnp.max(st, axis=0, keepdims=True) + g_own
    p = jnp.exp(st - (m0 - g_own))
    l0 = jnp.sum(p, axis=0, keepdims=True)
    acc0 = jnp.dot(vt_ref[:, own], p.astype(BF16), preferred_element_type=F32)

    def past_blocks(jp, carry):
        m, l, acc = carry
        rows = pl.ds(pl.multiple_of(jp * (kpi * blk), kpi * blk), kpi * blk)
        st = _nt_dot(kb_ref[rows, :], qs) + biasp_ref[...]
        segs = [st[i * blk:(i + 1) * blk] for i in range(kpi)]
        gs = [g_ref[jp * kpi + i, 0:1, own] for i in range(kpi)]
        m_new = m
        for seg, g in zip(segs, gs):
            m_new = jnp.maximum(m_new, jnp.max(seg, axis=0, keepdims=True) + g)
        alpha = jnp.exp(m - m_new)
        p = jnp.concatenate([jnp.exp(seg - (m_new - g)) for seg, g in zip(segs, gs)], axis=0)
        l = alpha * l + jnp.sum(p, axis=0, keepdims=True)
        acc = alpha * acc + jnp.dot(vt_ref[:, rows], p.astype(BF16), preferred_element_type=F32)
        return m_new, l, acc

    _, l, acc = lax.fori_loop(0, (qi + kpi - 1) // kpi, past_blocks, (m0, l0, acc0))
    ot = acc * (1.0 / l)
    ot = ot * lax.rsqrt(jnp.mean(ot * ot, axis=0, keepdims=True) + EPS)
    o_ref[0] = (ot.T * nrm_ref[...]).astype(o_ref.dtype)


def _moba_prompt(q, k, v, slopes, attn_norm, kpi):
    b, t, _ = q.shape
    blk = MOBA_BLOCK
    nblk = t // blk
    assert t % blk == 0 and nblk % kpi == 0
    seq = pl.BlockSpec((1, t, HEAD_DIM), lambda bi, h, qi: (bi, 0, h))
    tile = pl.BlockSpec((1, blk, HEAD_DIM), lambda bi, h, qi: (bi, qi, h))
    return pl.pallas_call(
        functools.partial(_moba_prompt_kernel, nblk=nblk, kpi=kpi),
        grid=(b, N_HEADS, nblk),
        in_specs=[
            pl.BlockSpec(memory_space=pltpu.SMEM),
            seq, seq, seq,
            pl.BlockSpec((1, HEAD_DIM), lambda bi, h, qi: (0, h)),
        ],
        out_specs=tile,
        out_shape=jax.ShapeDtypeStruct((b, t, D_ATTN), BF16),
        scratch_shapes=[
            pltpu.VMEM((t, HEAD_DIM), BF16),
            pltpu.VMEM((t, HEAD_DIM), BF16),
            pltpu.VMEM((HEAD_DIM, t), BF16),
            pltpu.VMEM((nblk, SUBLANES, t), F32),
            pltpu.VMEM((kpi * blk, blk), F32),
            pltpu.VMEM((blk, blk), F32),
        ],
        compiler_params=_cparams(("arbitrary", "arbitrary", "arbitrary")),
        name="moba_prompt",
    )(slopes, q, k, v, attn_norm)


def _kmean_kernel(pt_ref, ck_ref, out_ref, buf, sem, *, n_chunks, chunks_per_b, pg):
    ppb = MOBA_BLOCK // buf.shape[2]

    def copy(page, slot, i):
        return pltpu.make_async_copy(ck_ref.at[page], buf.at[slot, i], sem.at[slot])

    def start(c, slot):
        for i in range(pg):
            copy(pt_ref[c * pg + i], slot, i).start()

    start(0, 0)

    def body(c, _):
        slot = c % 2

        @pl.when(c + 1 < n_chunks)
        def _():
            start(c + 1, 1 - slot)

        for i in range(pg):
            copy(0, slot, i).wait()
        bi = c // chunks_per_b
        blk0 = (c % chunks_per_b) * (pg // ppb)
        for i in range(pg // ppb):
            s = jnp.sum(buf[slot, ppb * i], axis=0)
            for p in range(1, ppb):
                s = s + jnp.sum(buf[slot, ppb * i + p], axis=0)
            out_ref[bi, blk0 + i] = s * (1.0 / MOBA_BLOCK)
        return 0

    lax.fori_loop(0, n_chunks, body, 0)


def _cache_block_means(page_table, cache_k):
    db, n_pages = page_table.shape
    _, page, nh, hd = cache_k.shape
    ppb = MOBA_BLOCK // page
    pg = 8
    assert n_pages % pg == 0 and pg % ppb == 0
    chunks_per_b = n_pages // pg
    return pl.pallas_call(
        functools.partial(_kmean_kernel, n_chunks=db * chunks_per_b, chunks_per_b=chunks_per_b, pg=pg),
        grid_spec=pltpu.PrefetchScalarGridSpec(
            num_scalar_prefetch=1,
            grid=(1,),
            in_specs=[pl.BlockSpec(memory_space=pl.ANY)],
            out_specs=pl.BlockSpec((db, n_pages // ppb, nh, hd), lambda i, pt: (0, 0, 0, 0)),
            scratch_shapes=[
                pltpu.VMEM((2, pg, page, nh, hd), F32),
                pltpu.SemaphoreType.DMA((2,)),
            ],
        ),
        out_shape=jax.ShapeDtypeStruct((db, n_pages // ppb, nh, hd), F32),
        compiler_params=_cparams(("arbitrary",)),
        name="cache_block_means",
    )(page_table.reshape(-1), cache_k)


def _sample_topk_kernel(q_ref, km_ref, idx_ref, *, nbf):
    for h in range(N_HEADS):
        gate = _nt_dot(q_ref[0, h], km_ref[0, h], precision=lax.Precision.HIGHEST)
        lane = lax.broadcasted_iota(jnp.int32, gate.shape, 1)
        g = jnp.where(lane < nbf, gate, -jnp.inf)
        out = jnp.zeros(gate.shape, jnp.int32)
        for r in range(MOBA_TOPK):
            m = jnp.max(g, axis=-1, keepdims=True)
            idx = jnp.min(jnp.where(g == m, lane, LANES), axis=-1, keepdims=True)
            out = jnp.where(lane == r, idx, out)
            g = jnp.where(lane == idx, -jnp.inf, g)
        idx_ref[0, h] = out


def _sample_topk(q_t, kmean_t, nbf):
    db, nh, s, hd = q_t.shape
    return pl.pallas_call(
        functools.partial(_sample_topk_kernel, nbf=nbf),
        grid=(db,),
        in_specs=[
            pl.BlockSpec((1, nh, s, hd), lambda i: (i, 0, 0, 0)),
            pl.BlockSpec((1, nh, LANES, hd), lambda i: (i, 0, 0, 0)),
        ],
        out_specs=pl.BlockSpec((1, nh, s, LANES), lambda i: (i, 0, 0, 0)),
        out_shape=jax.ShapeDtypeStruct((db, nh, s, LANES), jnp.int32),
        compiler_params=_cparams(("arbitrary",)),
        name="sample_topk",
    )(q_t, kmean_t)


def _moba_sample_kernel(pt_ref, blk_ref, slopes_ref, q_ref, kn_ref, vn_ref, nrm_ref, ck_ref, cv_ref, o_ref,
                        kbuf, vbuf, sem, *, n_units, n_sel, ppb, past):
    page = ck_ref.shape[1]
    n_slab = n_sel * ppb
    n_keys = n_slab * page
    n_pages = past // page
    s_len = q_ref.shape[2]
    rows = 16

    def copies(u, slot, i, h, pg):
        dst = pl.ds(pl.multiple_of(i * page, page), page)
        return (pltpu.make_async_copy(ck_ref.at[pg, :, h, :], kbuf.at[slot, dst], sem.at[0, slot]),
                pltpu.make_async_copy(cv_ref.at[pg, :, h, :], vbuf.at[slot, dst], sem.at[1, slot]))

    def start(u, slot):
        h = u % N_HEADS
        row0 = (u // N_HEADS) * n_pages

        def sel(n, carry):
            logical = blk_ref[u * n_sel + n] * ppb
            for p in range(ppb):
                ck, cv = copies(u, slot, n * ppb + p, h, pt_ref[row0 + logical + p])
                ck.start()
                cv.start()
            return carry

        lax.fori_loop(0, n_sel, sel, 0)

    u = pl.program_id(0)

    @pl.when(u == 0)
    def _():
        start(0, 0)

    slot = u % 2
    bi = u // N_HEADS
    h = u % N_HEADS

    @pl.when(u + 1 < n_units)
    def _():
        start(u + 1, 1 - slot)

    def slab_done(i, carry):
        ck, cv = copies(u, slot, i, 0, 0)
        ck.wait()
        cv.wait()
        return carry

    lax.fori_loop(0, n_slab, slab_done, 0)

    slope = slopes_ref[h]
    q = q_ref[bi, h]
    qs = jnp.concatenate([q * (HEAD_DIM ** -0.5), jnp.zeros((rows - s_len, HEAD_DIM), F32)], axis=0)
    qs = qs.astype(BF16)
    kb = kbuf[slot].astype(BF16)
    vb = vbuf[slot].astype(BF16)
    s = _nt_dot(qs, kb)

    per_blk = page * ppb
    offs = lax.broadcasted_iota(jnp.int32, (1, per_blk), 1)
    kpos = jnp.concatenate(
        [(blk_ref[u * n_sel + i] * per_blk + offs).astype(F32) for i in range(n_sel)], axis=1)
    row = lax.broadcasted_iota(jnp.int32, (rows, 1), 0)
    qpos = (past + row).astype(F32)
    s = s - slope * (qpos - kpos)
    col = lax.broadcasted_iota(jnp.int32, (rows, n_keys), 1)
    per_q = MOBA_TOPK * per_blk
    mine = jnp.logical_and(col >= row * per_q, col < (row + 1) * per_q)
    s = jnp.where(mine, s, NEG_INF)

    kn = jnp.concatenate([kn_ref[bi, h], jnp.zeros((rows - s_len, HEAD_DIM), F32)], axis=0).astype(BF16)
    vn = jnp.concatenate([vn_ref[bi, h], jnp.zeros((rows - s_len, HEAD_DIM), F32)], axis=0).astype(BF16)
    ri = lax.broadcasted_iota(jnp.int32, (rows, rows), 0)
    ci = lax.broadcasted_iota(jnp.int32, (rows, rows), 1)
    so = _nt_dot(qs, kn) - slope * (ri - ci).astype(F32)
    so = jnp.where(jnp.logical_and(ri >= ci, ci < s_len), so, NEG_INF)

    m = jnp.maximum(jnp.max(s, axis=-1, keepdims=True), jnp.max(so, axis=-1, keepdims=True))
    p = jnp.exp(s - m)
    po = jnp.exp(so - m)
    l = jnp.sum(p, axis=-1, keepdims=True) + jnp.sum(po, axis=-1, keepdims=True)
    acc = (jnp.dot(p.astype(BF16), vb, preferred_element_type=F32)
           + jnp.dot(po.astype(BF16), vn, preferred_element_type=F32))
    o = (acc / l)[0:s_len]
    o = (o * lax.rsqrt(jnp.mean(o * o, axis=-1, keepdims=True) + EPS)) * nrm_ref[pl.ds(h, 1), :]
    o_ref[bi, h] = o


def _moba_sample(page_table_flat, blks, slopes, q_t, kn_t, vn_t, attn_norm_hd, cache_k, cache_v, past):
    db, nh, s, hd = q_t.shape
    _, page, _, _ = cache_k.shape
    ppb = MOBA_BLOCK // page
    n_sel = s * MOBA_TOPK
    whole = lambda shp: pl.BlockSpec(shp, lambda i, *_: (0,) * len(shp))
    return pl.pallas_call(
        functools.partial(_moba_sample_kernel, n_units=db * nh, n_sel=n_sel, ppb=ppb, past=past),
        grid_spec=pltpu.PrefetchScalarGridSpec(
            num_scalar_prefetch=3,
            grid=(db * nh,),
            in_specs=[
                whole(q_t.shape), whole(kn_t.shape), whole(vn_t.shape), whole(attn_norm_hd.shape),
                pl.BlockSpec(memory_space=pl.ANY),
                pl.BlockSpec(memory_space=pl.ANY),
            ],
            out_specs=whole(q_t.shape),
            scratch_shapes=[
                pltpu.VMEM((2, n_sel * ppb * page, hd), F32),
                pltpu.VMEM((2, n_sel * ppb * page, hd), F32),
                pltpu.SemaphoreType.DMA((2, 2)),
            ],
        ),
        out_shape=jax.ShapeDtypeStruct(q_t.shape, F32),
        compiler_params=_cparams(("arbitrary",)),
        name="moba_sample",
    )(page_table_flat, blks, slopes, q_t, kn_t, vn_t, attn_norm_hd, cache_k, cache_v)


def _outproj_kernel(op_ref, cp_ref, xp_ref, os_ref, cs_ref, xs_ref, wo_ref, wc_ref, g_ref, wr_ref, br_ref,
                    x2_ref, h2_ref, route_ref, *, nb_p):
    i = pl.program_id(0)
    tail = (wo_ref, wc_ref, g_ref, wr_ref, br_ref, x2_ref, h2_ref, route_ref)

    @pl.when(i < nb_p)
    def _():
        _outproj_tile(op_ref, cp_ref, xp_ref, *tail)

    @pl.when(i >= nb_p)
    def _():
        _outproj_tile(os_ref, cs_ref, xs_ref, *tail)


def _outproj_tile(o_ref, c_ref, x_ref, wo_ref, wc_ref, g_ref, wr_ref, br_ref, x2_ref, h2_ref, route_ref):
    mix = (jnp.dot(o_ref[...].astype(BF16), wo_ref[...], preferred_element_type=F32)
           + jnp.dot(c_ref[...].astype(BF16), wc_ref[...], preferred_element_type=F32))
    x2 = x_ref[...] + mix
    x2_ref[...] = x2
    r = lax.rsqrt(jnp.mean(x2 * x2, axis=-1, keepdims=True) + EPS)
    h2 = (x2 * r) * g_ref[...]
    h2_ref[...] = h2

    h_hi = h2.astype(BF16)
    h_lo = (h2 - h_hi.astype(F32)).astype(BF16)
    r1 = jnp.dot(h_hi, wr_ref[...], preferred_element_type=F32)
    r2 = jnp.dot(h_lo, wr_ref[:, 0:LANES], preferred_element_type=F32)
    logits = r1[:, 0:LANES] + r1[:, LANES:2 * LANES] + r2 + br_ref[...]

    lane = lax.broadcasted_iota(jnp.int32, logits.shape, 1)
    is_g = lane < N_EXPERT_GROUPS
    gl = jnp.where(is_g, logits, -jnp.inf)
    gmax = jnp.max(gl, axis=-1, keepdims=True)
    gidx = jnp.min(jnp.where(gl == gmax, lane, LANES), axis=-1, keepdims=True)
    gsum = jnp.sum(jnp.where(is_g, jnp.exp(gl - gmax), 0.0), axis=-1, keepdims=True)
    g_gate = 1.0 / gsum

    lo = N_EXPERT_GROUPS + gidx * EXPERTS_PER_GROUP
    el = jnp.where(jnp.logical_and(lane >= lo, lane < lo + EXPERTS_PER_GROUP), logits, -jnp.inf)
    v0 = jnp.max(el, axis=-1, keepdims=True)
    i0 = jnp.min(jnp.where(el == v0, lane, LANES), axis=-1, keepdims=True)
    el = jnp.where(lane == i0, -jnp.inf, el)
    v1 = jnp.max(el, axis=-1, keepdims=True)
    i1 = jnp.min(jnp.where(el == v1, lane, LANES), axis=-1, keepdims=True)
    e = jnp.exp(v1 - v0)
    w0 = g_gate / (1.0 + e)
    w1 = g_gate * e / (1.0 + e)

    out = jnp.where(lane == 0, (i0 - N_EXPERT_GROUPS).astype(F32), 0.0)
    out = jnp.where(lane == 1, (i1 - N_EXPERT_GROUPS).astype(F32), out)
    out = jnp.where(lane == 2, w0, out)
    out = jnp.where(lane == 3, w1, out)
    route_ref[...] = out


def _out_projection(o_p, c_p, x_p, o_s, c_s, x_s, wo, wc, norm_g, wr, br, tm):
    np_, d = x_p.shape
    ns_ = x_s.shape[0]
    assert np_ % tm == 0 and ns_ % tm == 0
    nb_p, nb_s = np_ // tm, ns_ // tm
    n = np_ + ns_
    prow = lambda w: pl.BlockSpec((tm, w), lambda i: (jnp.minimum(i, nb_p - 1), 0))
    srow = lambda w: pl.BlockSpec((tm, w), lambda i: (jnp.maximum(i - nb_p, 0), 0))
    row = lambda w: pl.BlockSpec((tm, w), lambda i: (i, 0))
    full = lambda a: pl.BlockSpec(a.shape, lambda i: (0, 0))
    return pl.pallas_call(
        functools.partial(_outproj_kernel, nb_p=nb_p),
        grid=(nb_p + nb_s,),
        in_specs=[prow(o_p.shape[1]), prow(c_p.shape[1]), prow(d), srow(o_s.shape[1]), srow(c_s.shape[1]), srow(d),
                  full(wo), full(wc), full(norm_g), full(wr), full(br)],
        out_specs=[row(d), row(d), row(LANES)],
        out_shape=[jax.ShapeDtypeStruct((n, d), F32), jax.ShapeDtypeStruct((n, d), F32),
                   jax.ShapeDtypeStruct((n, LANES), F32)],
        compiler_params=_cparams(("arbitrary",)),
        name="out_projection_router",
    )(o_p, c_p, x_p, o_s, c_s, x_s, wo, wc, norm_g, wr, br)


DMA_UNROLL = 8


def _for_rows(n, fn):
    def group(gi, carry):
        for k in range(DMA_UNROLL):
            fn(gi * DMA_UNROLL + k)
        return carry

    def single(i, carry):
        fn(i)
        return carry

    n_grp = lax.div(n, DMA_UNROLL)
    lax.fori_loop(0, n_grp, group, 0)
    lax.fori_loop(n_grp * DMA_UNROLL, n, single, 0)


def _moe_kernel(order_ref, cexp_ref, cbase_ref, cnv_ref, h_ref, wg_ref, wu_ref, wd_ref, out_ref,
                xbuf, obuf, wgb, wub, wdb, gsem, ssem, *, n_tok, n_chunks):
    g = pl.program_id(0)
    slot = g % 2

    def gather(chunk, buf_slot):
        base = cbase_ref[chunk]

        def desc(i):
            tok = lax.shift_right_logical(order_ref[base + i], 1)
            return pltpu.make_async_copy(h_ref.at[pl.ds(tok, 1)], xbuf.at[buf_slot, pl.ds(i, 1)], gsem.at[buf_slot])
        return desc

    def scatter(chunk, buf_slot):
        base = cbase_ref[chunk]

        def desc(i):
            pair = order_ref[base + i]
            row = (pair & 1) * n_tok + lax.shift_right_logical(pair, 1)
            return pltpu.make_async_copy(obuf.at[buf_slot, pl.ds(i, 1)], out_ref.at[pl.ds(row, 1)], ssem.at[buf_slot])
        return desc

    def gather_done(buf_slot):
        pltpu.make_async_copy(h_ref.at[pl.ds(0, 1)], xbuf.at[buf_slot, pl.ds(0, 1)], gsem.at[buf_slot]).wait()

    def scatter_done(buf_slot):
        pltpu.make_async_copy(obuf.at[buf_slot, pl.ds(0, 1)], out_ref.at[pl.ds(0, 1)], ssem.at[buf_slot]).wait()

    @pl.when(g == 0)
    def _():
        xbuf[...] = jnp.zeros(xbuf.shape, xbuf.dtype)
        _for_rows(cnv_ref[0], lambda i: gather(0, 0)(i).start())

    @pl.when(jnp.logical_or(g == 0, cexp_ref[g] != cexp_ref[jnp.maximum(g - 1, 0)]))
    def _():
        wgb[...] = wg_ref[0].astype(BF16)
        wub[...] = wu_ref[0].astype(BF16)
        wdb[...] = wd_ref[0].astype(BF16)

    @pl.when(g + 1 < n_chunks)
    def _():
        _for_rows(cnv_ref[g + 1], lambda i: gather(g + 1, 1 - slot)(i).start())

    nv = cnv_ref[g]
    _for_rows(nv, lambda i: gather_done(slot))

    @pl.when(g >= 2)
    def _():
        _for_rows(cnv_ref[g - 2], lambda i: scatter_done(slot))

    @pl.when(nv > 0)
    def _():
        x = xbuf[slot].astype(BF16)
        gt = jnp.dot(x, wgb[...], preferred_element_type=F32)
        up = jnp.dot(x, wub[...], preferred_element_type=F32)
        hid = (gt * (1.0 / (1.0 + jnp.exp(-gt)))) * up
        obuf[slot] = jnp.dot(hid.astype(BF16), wdb[...], preferred_element_type=F32)

    _for_rows(nv, lambda i: scatter(g, slot)(i).start())

    @pl.when(g == n_chunks - 1)
    def _():
        if n_chunks >= 2:
            _for_rows(cnv_ref[g - 1], lambda i: scatter_done(1 - slot))
        _for_rows(nv, lambda i: scatter_done(slot))


def _moe(order, cexp, cbase, cnv, h2, w_gate, w_up, w_down, ch):
    n, d = h2.shape
    ne, _, de = w_gate.shape
    n_chunks = cexp.shape[0]
    wspec = lambda shp: pl.BlockSpec((1,) + shp, lambda g, order, cexp, cbase, cnv: (cexp[g], 0, 0))
    return pl.pallas_call(
        functools.partial(_moe_kernel, n_tok=n, n_chunks=n_chunks),
        grid_spec=pltpu.PrefetchScalarGridSpec(
            num_scalar_prefetch=4,
            grid=(n_chunks,),
            in_specs=[
                pl.BlockSpec(memory_space=pl.ANY),
                wspec((d, de)), wspec((d, de)), wspec((de, d)),
            ],
            out_specs=pl.BlockSpec(memory_space=pl.ANY),
            scratch_shapes=[
                pltpu.VMEM((2, ch, d), F32),
                pltpu.VMEM((2, ch, d), F32),
                pltpu.VMEM((d, de), BF16),
                pltpu.VMEM((d, de), BF16),
                pltpu.VMEM((de, d), BF16),
                pltpu.SemaphoreType.DMA((2,)),
                pltpu.SemaphoreType.DMA((2,)),
            ],
        ),
        out_shape=jax.ShapeDtypeStruct((2 * n, d), F32),
        compiler_params=_cparams(("arbitrary",)),
        name="expert_mlp",
    )(order, cexp, cbase, cnv, h2, w_gate, w_up, w_down)


def _expert_chunks(eid, ch):
    n_pairs = eid.shape[0]
    n_chunks = N_EXPERTS + n_pairs // ch
    order = jnp.argsort(eid, stable=True).astype(jnp.int32)
    counts = jnp.sum(eid[:, None] == jnp.arange(N_EXPERTS, dtype=jnp.int32)[None, :], axis=0, dtype=jnp.int32)
    offs = jnp.cumsum(counts, dtype=jnp.int32) - counts
    nch = (counts + ch - 1) // ch
    cum = jnp.cumsum(nch, dtype=jnp.int32)
    gidx = jnp.arange(n_chunks, dtype=jnp.int32)
    cexp_raw = jnp.sum(gidx[:, None] >= cum[None, :], axis=1, dtype=jnp.int32)
    last = jnp.maximum(jnp.max(jnp.where(nch > 0, jnp.arange(N_EXPERTS, dtype=jnp.int32), 0)), 0)
    used = gidx < cum[-1]
    cexp = jnp.where(used, jnp.minimum(cexp_raw, N_EXPERTS - 1), last)
    local = gidx - (cum[cexp] - nch[cexp])
    cbase = jnp.where(used, offs[cexp] + local * ch, 0)
    cnv = jnp.where(used, jnp.clip(counts[cexp] - local * ch, 0, ch), 0)
    return order, cexp.astype(jnp.int32), cbase.astype(jnp.int32), cnv.astype(jnp.int32)


def _final_kernel(x2_ref, ya_ref, yb_ref, route_ref, g_ref, yp_ref, ys_ref, *, nb_p):
    i = pl.program_id(0)
    rt = route_ref[...]
    x3 = x2_ref[...] + rt[:, 2:3] * ya_ref[...] + rt[:, 3:4] * yb_ref[...]
    r = lax.rsqrt(jnp.mean(x3 * x3, axis=-1, keepdims=True) + EPS)
    y = (x3 * r) * g_ref[...]

    @pl.when(i < nb_p)
    def _():
        yp_ref[...] = y

    @pl.when(i >= nb_p)
    def _():
        ys_ref[...] = y


def _final(x2, y2, route, norm_g, np_, tm):
    n, d = x2.shape
    assert np_ % tm == 0 and n % tm == 0
    nb, nb_p = n // tm, np_ // tm
    return pl.pallas_call(
        functools.partial(_final_kernel, nb_p=nb_p),
        grid=(nb,),
        in_specs=[
            pl.BlockSpec((tm, d), lambda i: (i, 0)),
            pl.BlockSpec((tm, d), lambda i: (i, 0)),
            pl.BlockSpec((tm, d), lambda i: (i + nb, 0)),
            pl.BlockSpec((tm, LANES), lambda i: (i, 0)),
            pl.BlockSpec((1, d), lambda i: (0, 0)),
        ],
        out_specs=[
            pl.BlockSpec((tm, d), lambda i: (jnp.minimum(i, nb_p - 1), 0)),
            pl.BlockSpec((tm, d), lambda i: (jnp.maximum(i - nb_p, 0), 0)),
        ],
        out_shape=[jax.ShapeDtypeStruct((np_, d), F32), jax.ShapeDtypeStruct((n - np_, d), F32)],
        compiler_params=_cparams(("arbitrary",)),
        name="combine_final_norm",
    )(x2, y2, y2, route, norm_g)


def _pick_tile(n, pref):
    t = min(n, pref)
    while n % t:
        t //= 2
    return t


def kernel(x_prompt, x_sample, cache_k, cache_v, state_conv, page_table, norm_mix, w_in, w_dw, b_dw,
           conv_norm_g, conv_norm_b, attn_out_norm, w_out, norm_ffn, w_router_group, b_router_group,
           w_router_expert, b_router_expert, w_gate, w_up, w_down, norm_final):
    depth = norm_mix.shape[0]
    assert depth == 1
    b, t, d = x_prompt.shape
    db, s, _ = x_sample.shape
    n_pages = page_table.shape[1]
    page = cache_k.shape[2]
    past = n_pages * page
    nbf = past // MOBA_BLOCK
    assert past % MOBA_BLOCK == 0 and nbf >= MOBA_TOPK and nbf <= LANES
    ppb = MOBA_BLOCK // page
    d_conv = d - D_ATTN
    assert d_conv == D_ATTN
    np_, ns_ = b * t, db * s
    ntot = np_ + ns_

    slopes = jnp.exp2(-8.0 * jnp.arange(1, N_HEADS + 1, dtype=F32) / N_HEADS)
    w_in_bf = w_in[0].astype(BF16)
    wo_bf = w_out[0, :D_ATTN].astype(BF16)
    wc_bf = w_out[0, D_ATTN:].astype(BF16)
    wr = jnp.zeros((d, LANES), F32)
    wr = wr.at[:, :N_EXPERT_GROUPS].set(w_router_group[0])
    wr = wr.at[:, N_EXPERT_GROUPS:N_EXPERT_GROUPS + N_EXPERTS].set(w_router_expert[0])
    wr_hi = wr.astype(BF16)
    wr_lo = (wr - wr_hi.astype(F32)).astype(BF16)
    wr_cat = jnp.concatenate([wr_hi, wr_lo], axis=1)
    br = jnp.zeros((1, LANES), F32)
    br = br.at[0, :N_EXPERT_GROUPS].set(b_router_group[0])
    br = br.at[0, N_EXPERT_GROUPS:N_EXPERT_GROUPS + N_EXPERTS].set(b_router_expert[0])
    g_mix = norm_mix[0].reshape(1, d)
    g_ffn = norm_ffn[0].reshape(1, d)
    g_fin = norm_final.reshape(1, d)
    attn_norm = attn_out_norm[0].reshape(1, D_ATTN)
    conv_vec = lambda a: a[0].reshape(1, d_conv)

    xp = x_prompt.reshape(np_, d)
    xs = x_sample.reshape(ns_, d)
    qp, kp, vp, glu_p = _in_projection(xp, g_mix, w_in_bf, _pick_tile(np_, 512))
    qs, ks, vs, glu_s = _in_projection(xs, g_mix, w_in_bf, _pick_tile(ns_, 256))

    glu_p3 = glu_p.reshape(b, t, d_conv)
    glu_s3 = glu_s.reshape(db, s, d_conv)
    pad = HALO - (CONV_WIDTH - 1)
    st_p = jnp.zeros((b, HALO, d_conv), F32)
    st_s = jnp.pad(state_conv[0], ((0, 0), (pad, 0), (0, 0)))
    conv_args = (w_dw[0], conv_vec(b_dw), conv_vec(conv_norm_g), conv_vec(conv_norm_b))
    c_p = _conformer_conv(glu_p3, st_p, *conv_args, tt=_pick_tile(t, 256), nseq=1, out_dtype=BF16)
    c_s = _conformer_conv(glu_s3, st_s, *conv_args, tt=s, nseq=db, out_dtype=F32)
    conv_prompt = glu_p3[:, t - (CONV_WIDTH - 1):][None]
    conv_sample = jnp.concatenate([state_conv[0], glu_s3], axis=1)[:, -(CONV_WIDTH - 1):][None]

    kpi = max(k for k in (4, 2, 1) if (t // MOBA_BLOCK) % k == 0)
    o_p = _moba_prompt(qp.reshape(b, t, D_ATTN), kp.reshape(b, t, D_ATTN), vp.reshape(b, t, D_ATTN),
                       slopes, attn_norm, kpi)

    ck = cache_k[0]
    cv = cache_v[0]
    kmean = _cache_block_means(page_table, ck)
    kmean_t = jnp.pad(kmean.transpose(0, 2, 1, 3), ((0, 0), (0, 0), (0, LANES - nbf), (0, 0)))
    to_heads = lambda a: a.reshape(db, s, N_HEADS, HEAD_DIM).transpose(0, 2, 1, 3)
    q_t, kn_t, vn_t = to_heads(qs), to_heads(ks), to_heads(vs)
    idx = _sample_topk(q_t, kmean_t, nbf)[..., :MOBA_TOPK]
    o_t = _moba_sample(page_table.reshape(-1), idx.reshape(-1), slopes, q_t, kn_t, vn_t,
                       attn_norm.reshape(N_HEADS, HEAD_DIM), ck, cv, past)
    o_s = o_t.transpose(0, 2, 1, 3).reshape(ns_, D_ATTN)

    tm = _pick_tile(ns_, 256)
    x2, h2, route = _out_projection(o_p.reshape(np_, D_ATTN), c_p.reshape(np_, d_conv), xp,
                                    o_s, c_s.reshape(ns_, d_conv), xs,
                                    wo_bf, wc_bf, g_ffn, wr_cat, br, tm)

    ch = 256
    eid = route[:, 0:2].astype(jnp.int32).reshape(-1)
    order, cexp, cbase, cnv = _expert_chunks(eid, ch)
    y2 = _moe(order, cexp, cbase, cnv, h2, w_gate[0], w_up[0], w_down[0], ch)

    y_p, y_s = _final(x2, y2, route, g_fin, np_, tm)

    page_shape = (depth, b, t // page, page, N_HEADS, HEAD_DIM)
    return (y_p.reshape(b, t, d), y_s.reshape(db, s, d),
            kp.reshape(page_shape), vp.reshape(page_shape), conv_prompt,
            ks.reshape(depth, db, s, N_HEADS, HEAD_DIM), vs.reshape(depth, db, s, N_HEADS, HEAD_DIM),
            conv_sample)
```

```python
import functools

import jax
import jax.numpy as jnp
from jax import lax
from jax.experimental import pallas as pl
from jax.experimental.pallas import tpu as pltpu

F32 = jnp.float32
BF16 = jnp.bfloat16

N_HEADS = 8
HEAD_DIM = 128
D_ATTN = N_HEADS * HEAD_DIM
CONV_GROUPS = 8
CONV_WIDTH = 31
MOBA_BLOCK = 256
MOBA_TOPK = 3
N_EXPERT_GROUPS = 4
EXPERTS_PER_GROUP = 8
N_EXPERTS = N_EXPERT_GROUPS * EXPERTS_PER_GROUP
EPS = 1e-6
NEG_INF = -1e30

LANES = 128
SUBLANES = 8
HALO = 32
VMEM_LIMIT = 56 * 1024 * 1024


def _cparams(sem=None):
    return pltpu.CompilerParams(dimension_semantics=sem, vmem_limit_bytes=VMEM_LIMIT)


def _nt_dot(a, b, precision=None):
    return lax.dot_general(a, b, (((1,), (1,)), ((), ())), precision=precision,
                           preferred_element_type=F32)


def _inproj_kernel(x_ref, g_ref, w_ref, wgate_ref, q_ref, k_ref, v_ref, glu_ref, h_ref):
    j = pl.program_id(1)

    @pl.when(j == 0)
    def _():
        x = x_ref[...]
        r = lax.rsqrt(jnp.mean(x * x, axis=-1, keepdims=True) + EPS)
        h_ref[...] = ((x * r) * g_ref[...]).astype(BF16)

    z = jnp.dot(h_ref[...], w_ref[...], preferred_element_type=F32)

    @pl.when(j == 0)
    def _():
        q_ref[...] = z

    @pl.when(j == 1)
    def _():
        k_ref[...] = z

    @pl.when(j == 2)
    def _():
        v_ref[...] = z

    @pl.when(j == 3)
    def _():
        gate = jnp.dot(h_ref[...], wgate_ref[...], preferred_element_type=F32)
        glu_ref[...] = z * (1.0 / (1.0 + jnp.exp(-gate)))


def _in_projection(x, norm_g, w_in_bf, tm):
    n, d = x.shape
    dc = D_ATTN
    out = jax.ShapeDtypeStruct((n, dc), F32)
    row = pl.BlockSpec((tm, dc), lambda i, j: (i, 0))
    return pl.pallas_call(
        _inproj_kernel,
        grid=(n // tm, 4),
        in_specs=[
            pl.BlockSpec((tm, d), lambda i, j: (i, 0)),
            pl.BlockSpec((1, d), lambda i, j: (0, 0)),
            pl.BlockSpec((d, dc), lambda i, j: (0, j)),
            pl.BlockSpec((d, dc), lambda i, j: (0, 4)),
        ],
        out_specs=[row, row, row, row],
        out_shape=[out, out, out, out],
        scratch_shapes=[pltpu.VMEM((tm, d), BF16)],
        compiler_params=_cparams(("arbitrary", "arbitrary")),
        name="in_projection",
    )(x, norm_g, w_in_bf, w_in_bf)


def _conv_kernel(glu_ref, st_ref, w_ref, b_ref, gg_ref, gb_ref, c_ref, xs_ref, *, tt, rc, nseq, carry, preshift):
    t = pl.program_id(1)
    off = HALO - (CONV_WIDTH - 1)
    span = HALO + tt - SUBLANES

    def lane_block(sq, cs):
        for r0 in range(0, tt, rc):
            acc = jnp.broadcast_to(b_ref[:, cs], (rc, LANES))
            for j in range(CONV_WIDTH):
                o = off + j
                if preshift:
                    lo = r0 + (o // SUBLANES) * SUBLANES
                    x = xs_ref[o % SUBLANES, lo:lo + rc, cs]
                else:
                    x = xs_ref[0, r0 + o:r0 + o + rc, cs]
                acc = acc + x * w_ref[j:j + 1, cs]
            mu = jnp.mean(acc, axis=-1, keepdims=True)
            dlt = acc - mu
            var = jnp.mean(dlt * dlt, axis=-1, keepdims=True)
            yn = dlt * lax.rsqrt(var + EPS)
            yn = yn * gg_ref[:, cs] + gb_ref[:, cs]
            c_ref[sq, r0:r0 + rc, cs] = (yn * (1.0 / (1.0 + jnp.exp(-yn)))).astype(c_ref.dtype)

    def one_seq(sq):
        if carry:
            @pl.when(t == 0)
            def _():
                xs_ref[0, 0:HALO, :] = st_ref[sq]
        else:
            xs_ref[0, 0:HALO, :] = st_ref[sq]
        xs_ref[0, HALO:HALO + tt, :] = glu_ref[sq]
        if preshift:
            for r in range(1, SUBLANES):
                xs_ref[r, 0:span, :] = xs_ref[0, r:r + span, :]

            def group(cb, carry_):
                lane_block(sq, pl.ds(pl.multiple_of(cb * LANES, LANES), LANES))
                return carry_

            lax.fori_loop(0, CONV_GROUPS, group, 0)
        else:
            for cb in range(CONV_GROUPS):
                lane_block(sq, pl.ds(cb * LANES, LANES))
        if carry:
            xs_ref[0, 0:HALO, :] = xs_ref[0, tt:tt + HALO, :]

    if nseq == 1:
        one_seq(0)
    else:
        def seq(sq, carry_):
            one_seq(sq)
            return carry_

        lax.fori_loop(0, nseq, seq, 0)


def _conformer_conv(glu, state_pad, w_dw, b_dw, gn_g, gn_b, tt, nseq, out_dtype):
    b, t, dc = glu.shape
    assert dc == CONV_GROUPS * LANES and b % nseq == 0 and t % tt == 0
    nt = t // tt
    assert nt == 1 or nseq == 1
    preshift = tt >= 4 * SUBLANES
    rc = min(tt, 64)
    vec = pl.BlockSpec((1, dc), lambda i, j: (0, 0))
    return pl.pallas_call(
        functools.partial(_conv_kernel, tt=tt, rc=rc, nseq=nseq, carry=nt > 1, preshift=preshift),
        grid=(b // nseq, nt),
        in_specs=[
            pl.BlockSpec((nseq, tt, dc), lambda i, j: (i, j, 0)),
            pl.BlockSpec((nseq, HALO, dc), lambda i, j: (i, 0, 0)),
            pl.BlockSpec((CONV_WIDTH, dc), lambda i, j: (0, 0)),
            vec, vec, vec,
        ],
        out_specs=pl.BlockSpec((nseq, tt, dc), lambda i, j: (i, j, 0)),
        out_shape=jax.ShapeDtypeStruct((b, t, dc), out_dtype),
        scratch_shapes=[pltpu.VMEM((SUBLANES if preshift else 1, HALO + tt, dc), F32)],
        compiler_params=_cparams(("arbitrary", "arbitrary")),
        name="conformer_conv",
    )(glu, state_pad, w_dw, b_dw, gn_g, gn_b)


def _moba_prompt_kernel(pt_ref, slopes_ref, q_ref, k_ref, v_ref, nrm_ref, ck_ref, o_ref, kmean_ref,
                        qb_ref, kb_ref, vt_ref, g_ref, biasp_ref, biasd_ref, pbuf, kmbuf, psem, osem,
                        *, nblk, kpi, n_steps, n_chunks, pps, n_pages):
    h = pl.program_id(1)
    qi = pl.program_id(2)
    blk = MOBA_BLOCK
    t = q_ref.shape[1]
    nb8 = -(-nblk // SUBLANES) * SUBLANES
    slope = slopes_ref[h]

    step = (pl.program_id(0) * N_HEADS + h) * nblk + qi
    ppb = blk // pbuf.shape[2]

    def page_copy(chunk, i):
        slot = chunk % 2
        return pltpu.make_async_copy(ck_ref.at[pt_ref[chunk * pps + i]], pbuf.at[slot, i], psem.at[slot])

    def start_chunk(chunk):
        for i in range(pps):
            page_copy(chunk, i).start()

    def finish_chunk(chunk):
        slot = chunk % 2
        for i in range(pps):
            pltpu.make_async_copy(ck_ref.at[0], pbuf.at[slot, i], psem.at[slot]).wait()
        bi = (chunk * pps) // n_pages
        blk0 = ((chunk * pps) % n_pages) // ppb
        for i in range(pps // ppb):
            s = jnp.sum(pbuf[slot, ppb * i], axis=0)
            for p in range(1, ppb):
                s = s + jnp.sum(pbuf[slot, ppb * i + p], axis=0)
            kmbuf[bi, blk0 + i] = s * (1.0 / blk)

    @pl.when(step == 0)
    def _():
        start_chunk(0)

    @pl.when(step + 1 < n_chunks)
    def _():
        start_chunk(step + 1)

    @pl.when(qi == 0)
    def _():
        q = q_ref[0]
        kf = k_ref[0]
        qb_ref[...] = (q * (HEAD_DIM ** -0.5)).astype(BF16)
        kb_ref[...] = kf.astype(BF16)
        vt_ref[...] = v_ref[0].T.astype(BF16)
        km = jnp.sum(kf.reshape(nblk, blk, HEAD_DIM), axis=1) * (1.0 / blk)
        if nb8 > nblk:
            km = jnp.concatenate([km, jnp.zeros((nb8 - nblk, HEAD_DIM), F32)], axis=0)
        gate = _nt_dot(km, q, precision=lax.Precision.HIGHEST)
        j_io = lax.broadcasted_iota(jnp.int32, gate.shape, 0)
        t_io = lax.broadcasted_iota(jnp.int32, gate.shape, 1)
        valid = j_io * blk + (blk - 1) < t_io - (t_io % blk)
        g = jnp.where(valid, gate, NEG_INF)
        sel = jnp.zeros(gate.shape, jnp.bool_)
        for _ in range(MOBA_TOPK):
            m = jnp.max(g, axis=0, keepdims=True)
            idx = jnp.min(jnp.where(g == m, j_io, nb8), axis=0, keepdims=True)
            pick = j_io == idx
            sel = jnp.logical_or(sel, pick)
            g = jnp.where(pick, -jnp.inf, g)
        gb = (slope * blk) * j_io.astype(F32) + jnp.where(jnp.logical_and(sel, valid), 0.0, NEG_INF)
        for j in range(nblk):
            g_ref[j] = jnp.broadcast_to(gb[j:j + 1, :], (SUBLANES, t))
        c = lax.broadcasted_iota(jnp.int32, (kpi * blk, blk), 0)
        biasp_ref[...] = slope * (c % blk).astype(F32)
        c = lax.broadcasted_iota(jnp.int32, (blk, blk), 0)
        r = lax.broadcasted_iota(jnp.int32, (blk, blk), 1)
        biasd_ref[...] = slope * c.astype(F32) + jnp.where(c > r, NEG_INF, 0.0)

    own = pl.ds(pl.multiple_of(qi * blk, blk), blk)
    qs = qb_ref[own, :]

    g_own = (slope * blk) * qi.astype(F32)
    st = _nt_dot(kb_ref[own, :], qs) + biasd_ref[...]
    m0 = jnp.max(st, axis=0, keepdims=True) + g_own
    p = jnp.exp(st - (m0 - g_own))
    l0 = jnp.sum(p, axis=0, keepdims=True)
    acc0 = jnp.dot(vt_ref[:, own], p.astype(BF16), preferred_element_type=F32)

    def past_blocks(jp, carry):
        m, l, acc = carry
        rows = pl.ds(pl.multiple_of(jp * (kpi * blk), kpi * blk), kpi * blk)
        st = _nt_dot(kb_ref[rows, :], qs) + biasp_ref[...]
        segs = [st[i * blk:(i + 1) * blk] for i in range(kpi)]
        gs = [g_ref[jp * kpi + i, 0:1, own] for i in range(kpi)]
        m_new = m
        for seg, g in zip(segs, gs):
            m_new = jnp.maximum(m_new, jnp.max(seg, axis=0, keepdims=True) + g)
        alpha = jnp.exp(m - m_new)
        p = jnp.concatenate([jnp.exp(seg - (m_new - g)) for seg, g in zip(segs, gs)], axis=0)
        l = alpha * l + jnp.sum(p, axis=0, keepdims=True)
        acc = alpha * acc + jnp.dot(vt_ref[:, rows], p.astype(BF16), preferred_element_type=F32)
        return m_new, l, acc

    _, l, acc = lax.fori_loop(0, (qi + kpi - 1) // kpi, past_blocks, (m0, l0, acc0))
    ot = acc * (1.0 / l)
    ot = ot * lax.rsqrt(jnp.mean(ot * ot, axis=0, keepdims=True) + EPS)
    o_ref[0] = (ot.T * nrm_ref[...]).astype(o_ref.dtype)

    if n_chunks == n_steps:
        finish_chunk(step)
    else:
        @pl.when(step < n_chunks)
        def _():
            finish_chunk(step)

    @pl.when(step == n_steps - 1)
    def _():
        out = pltpu.make_async_copy(kmbuf, kmean_ref, osem.at[0])
        out.start()
        out.wait()


def _moba_prompt(q, k, v, slopes, attn_norm, kpi, page_table, cache_k):
    b, t, _ = q.shape
    blk = MOBA_BLOCK
    nblk = t // blk
    assert t % blk == 0 and nblk % kpi == 0
    db, n_pages = page_table.shape
    _, page, nh, hd = cache_k.shape
    ppb = blk // page
    n_steps = b * N_HEADS * nblk
    pps = -(-(db * n_pages) // n_steps)
    pps = -(-pps // ppb) * ppb
    assert n_pages % pps == 0
    n_chunks = db * n_pages // pps
    assert n_chunks <= n_steps
    seq = pl.BlockSpec((1, t, HEAD_DIM), lambda bi, h, qi, pt: (bi, 0, h))
    tile = pl.BlockSpec((1, blk, HEAD_DIM), lambda bi, h, qi, pt: (bi, qi, h))
    return pl.pallas_call(
        functools.partial(_moba_prompt_kernel, nblk=nblk, kpi=kpi, n_steps=n_steps, n_chunks=n_chunks, pps=pps,
                          n_pages=n_pages),
        grid_spec=pltpu.PrefetchScalarGridSpec(
            num_scalar_prefetch=1,
            grid=(b, N_HEADS, nblk),
            in_specs=[
                pl.BlockSpec(memory_space=pltpu.SMEM),
                seq, seq, seq,
                pl.BlockSpec((1, HEAD_DIM), lambda bi, h, qi, pt: (0, h)),
                pl.BlockSpec(memory_space=pl.ANY),
            ],
            out_specs=[tile, pl.BlockSpec(memory_space=pl.ANY)],
            scratch_shapes=[
                pltpu.VMEM((t, HEAD_DIM), BF16),
                pltpu.VMEM((t, HEAD_DIM), BF16),
                pltpu.VMEM((HEAD_DIM, t), BF16),
                pltpu.VMEM((nblk, SUBLANES, t), F32),
                pltpu.VMEM((kpi * blk, blk), F32),
                pltpu.VMEM((blk, blk), F32),
                pltpu.VMEM((2, pps, page, nh, hd), F32),
                pltpu.VMEM((db, n_pages // ppb, nh, hd), F32),
                pltpu.SemaphoreType.DMA((2,)),
                pltpu.SemaphoreType.DMA((1,)),
            ],
        ),
        out_shape=[jax.ShapeDtypeStruct((b, t, D_ATTN), BF16),
                   jax.ShapeDtypeStruct((db, n_pages // ppb, nh, hd), F32)],
        compiler_params=_cparams(("arbitrary", "arbitrary", "arbitrary")),
        name="moba_prompt",
    )(page_table.reshape(-1), slopes, q, k, v, attn_norm, cache_k)


def _sample_topk_kernel(q_ref, km_ref, idx_ref, *, nbf):
    for h in range(N_HEADS):
        gate = _nt_dot(q_ref[0, h], km_ref[0, h], precision=lax.Precision.HIGHEST)
        lane = lax.broadcasted_iota(jnp.int32, gate.shape, 1)
        g = jnp.where(lane < nbf, gate, -jnp.inf)
        out = jnp.zeros(gate.shape, jnp.int32)
        for r in range(MOBA_TOPK):
            m = jnp.max(g, axis=-1, keepdims=True)
            idx = jnp.min(jnp.where(g == m, lane, LANES), axis=-1, keepdims=True)
            out = jnp.where(lane == r, idx, out)
            g = jnp.where(lane == idx, -jnp.inf, g)
        idx_ref[0, h] = out


def _sample_topk(q_t, kmean_t, nbf):
    db, nh, s, hd = q_t.shape
    return pl.pallas_call(
        functools.partial(_sample_topk_kernel, nbf=nbf),
        grid=(db,),
        in_specs=[
            pl.BlockSpec((1, nh, s, hd), lambda i: (i, 0, 0, 0)),
            pl.BlockSpec((1, nh, LANES, hd), lambda i: (i, 0, 0, 0)),
        ],
        out_specs=pl.BlockSpec((1, nh, s, LANES), lambda i: (i, 0, 0, 0)),
        out_shape=jax.ShapeDtypeStruct((db, nh, s, LANES), jnp.int32),
        compiler_params=_cparams(("arbitrary",)),
        name="sample_topk",
    )(q_t, kmean_t)


def _moba_sample_kernel(pt_ref, blk_ref, slopes_ref, q_ref, kn_ref, vn_ref, nrm_ref, ck_ref, cv_ref, o_ref,
                        kbuf, vbuf, sem, *, n_units, n_sel, ppb, past):
    page = ck_ref.shape[1]
    n_slab = n_sel * ppb
    n_keys = n_slab * page
    n_pages = past // page
    s_len = q_ref.shape[2]
    rows = 16

    def copies(u, slot, i, h, pg):
        dst = pl.ds(pl.multiple_of(i * page, page), page)
        return (pltpu.make_async_copy(ck_ref.at[pg, :, h, :], kbuf.at[slot, dst], sem.at[0, slot]),
                pltpu.make_async_copy(cv_ref.at[pg, :, h, :], vbuf.at[slot, dst], sem.at[1, slot]))

    def start(u, slot):
        h = u % N_HEADS
        row0 = (u // N_HEADS) * n_pages

        def sel(n, carry):
            logical = blk_ref[u * n_sel + n] * ppb
            for p in range(ppb):
                ck, cv = copies(u, slot, n * ppb + p, h, pt_ref[row0 + logical + p])
                ck.start()
                cv.start()
            return carry

        lax.fori_loop(0, n_sel, sel, 0)

    u = pl.program_id(0)

    @pl.when(u == 0)
    def _():
        start(0, 0)

    slot = u % 2
    bi = u // N_HEADS
    h = u % N_HEADS

    @pl.when(u + 1 < n_units)
    def _():
        start(u + 1, 1 - slot)

    def slab_done(i, carry):
        ck, cv = copies(u, slot, i, 0, 0)
        ck.wait()
        cv.wait()
        return carry

    lax.fori_loop(0, n_slab, slab_done, 0)

    slope = slopes_ref[h]
    q = q_ref[bi, h]
    qs = jnp.concatenate([q * (HEAD_DIM ** -0.5), jnp.zeros((rows - s_len, HEAD_DIM), F32)], axis=0)
    qs = qs.astype(BF16)
    kb = kbuf[slot].astype(BF16)
    vb = vbuf[slot].astype(BF16)
    s = _nt_dot(qs, kb)

    per_blk = page * ppb
    offs = lax.broadcasted_iota(jnp.int32, (1, per_blk), 1)
    kpos = jnp.concatenate(
        [(blk_ref[u * n_sel + i] * per_blk + offs).astype(F32) for i in range(n_sel)], axis=1)
    row = lax.broadcasted_iota(jnp.int32, (rows, 1), 0)
    qpos = (past + row).astype(F32)
    s = s - slope * (qpos - kpos)
    col = lax.broadcasted_iota(jnp.int32, (rows, n_keys), 1)
    per_q = MOBA_TOPK * per_blk
    mine = jnp.logical_and(col >= row * per_q, col < (row + 1) * per_q)
    s = jnp.where(mine, s, NEG_INF)

    kn = jnp.concatenate([kn_ref[bi, h], jnp.zeros((rows - s_len, HEAD_DIM), F32)], axis=0).astype(BF16)
    vn = jnp.concatenate([vn_ref[bi, h], jnp.zeros((rows - s_len, HEAD_DIM), F32)], axis=0).astype(BF16)
    ri = lax.broadcasted_iota(jnp.int32, (rows, rows), 0)
    ci = lax.broadcasted_iota(jnp.int32, (rows, rows), 1)
    so = _nt_dot(qs, kn) - slope * (ri - ci).astype(F32)
    so = jnp.where(jnp.logical_and(ri >= ci, ci < s_len), so, NEG_INF)

    m = jnp.maximum(jnp.max(s, axis=-1, keepdims=True), jnp.max(so, axis=-1, keepdims=True))
    p = jnp.exp(s - m)
    po = jnp.exp(so - m)
    l = jnp.sum(p, axis=-1, keepdims=True) + jnp.sum(po, axis=-1, keepdims=True)
    acc = (jnp.dot(p.astype(BF16), vb, preferred_element_type=F32)
           + jnp.dot(po.astype(BF16), vn, preferred_element_type=F32))
    o = (acc / l)[0:s_len]
    o = (o * lax.rsqrt(jnp.mean(o * o, axis=-1, keepdims=True) + EPS)) * nrm_ref[pl.ds(h, 1), :]
    o_ref[bi, h] = o


def _moba_sample(page_table_flat, blks, slopes, q_t, kn_t, vn_t, attn_norm_hd, cache_k, cache_v, past):
    db, nh, s, hd = q_t.shape
    _, page, _, _ = cache_k.shape
    ppb = MOBA_BLOCK // page
    n_sel = s * MOBA_TOPK
    whole = lambda shp: pl.BlockSpec(shp, lambda i, *_: (0,) * len(shp))
    return pl.pallas_call(
        functools.partial(_moba_sample_kernel, n_units=db * nh, n_sel=n_sel, ppb=ppb, past=past),
        grid_spec=pltpu.PrefetchScalarGridSpec(
            num_scalar_prefetch=3,
            grid=(db * nh,),
            in_specs=[
                whole(q_t.shape), whole(kn_t.shape), whole(vn_t.shape), whole(attn_norm_hd.shape),
                pl.BlockSpec(memory_space=pl.ANY),
                pl.BlockSpec(memory_space=pl.ANY),
            ],
            out_specs=whole(q_t.shape),
            scratch_shapes=[
                pltpu.VMEM((2, n_sel * ppb * page, hd), F32),
                pltpu.VMEM((2, n_sel * ppb * page, hd), F32),
                pltpu.SemaphoreType.DMA((2, 2)),
            ],
        ),
        out_shape=jax.ShapeDtypeStruct(q_t.shape, F32),
        compiler_params=_cparams(("arbitrary",)),
        name="moba_sample",
    )(page_table_flat, blks, slopes, q_t, kn_t, vn_t, attn_norm_hd, cache_k, cache_v)


def _outproj_kernel(op_ref, cp_ref, xp_ref, os_ref, cs_ref, xs_ref, wo_ref, wc_ref, g_ref, wr_ref, br_ref,
                    x2_ref, h2_ref, route_ref, *, nb_p):
    i = pl.program_id(0)
    tail = (wo_ref, wc_ref, g_ref, wr_ref, br_ref, x2_ref, h2_ref, route_ref)

    @pl.when(i < nb_p)
    def _():
        _outproj_tile(op_ref, cp_ref, xp_ref, *tail)

    @pl.when(i >= nb_p)
    def _():
        _outproj_tile(os_ref, cs_ref, xs_ref, *tail)


def _outproj_tile(o_ref, c_ref, x_ref, wo_ref, wc_ref, g_ref, wr_ref, br_ref, x2_ref, h2_ref, route_ref):
    mix = (jnp.dot(o_ref[...].astype(BF16), wo_ref[...], preferred_element_type=F32)
           + jnp.dot(c_ref[...].astype(BF16), wc_ref[...], preferred_element_type=F32))
    x2 = x_ref[...] + mix
    x2_ref[...] = x2
    r = lax.rsqrt(jnp.mean(x2 * x2, axis=-1, keepdims=True) + EPS)
    h2 = (x2 * r) * g_ref[...]
    h2_ref[...] = h2

    h_hi = h2.astype(BF16)
    h_lo = (h2 - h_hi.astype(F32)).astype(BF16)
    r1 = jnp.dot(h_hi, wr_ref[...], preferred_element_type=F32)
    r2 = jnp.dot(h_lo, wr_ref[:, 0:LANES], preferred_element_type=F32)
    logits = r1[:, 0:LANES] + r1[:, LANES:2 * LANES] + r2 + br_ref[...]

    lane = lax.broadcasted_iota(jnp.int32, logits.shape, 1)
    is_g = lane < N_EXPERT_GROUPS
    gl = jnp.where(is_g, logits, -jnp.inf)
    gmax = jnp.max(gl, axis=-1, keepdims=True)
    gidx = jnp.min(jnp.where(gl == gmax, lane, LANES), axis=-1, keepdims=True)
    gsum = jnp.sum(jnp.where(is_g, jnp.exp(gl - gmax), 0.0), axis=-1, keepdims=True)
    g_gate = 1.0 / gsum

    lo = N_EXPERT_GROUPS + gidx * EXPERTS_PER_GROUP
    el = jnp.where(jnp.logical_and(lane >= lo, lane < lo + EXPERTS_PER_GROUP), logits, -jnp.inf)
    v0 = jnp.max(el, axis=-1, keepdims=True)
    i0 = jnp.min(jnp.where(el == v0, lane, LANES), axis=-1, keepdims=True)
    el = jnp.where(lane == i0, -jnp.inf, el)
    v1 = jnp.max(el, axis=-1, keepdims=True)
    i1 = jnp.min(jnp.where(el == v1, lane, LANES), axis=-1, keepdims=True)
    e = jnp.exp(v1 - v0)
    w0 = g_gate / (1.0 + e)
    w1 = g_gate * e / (1.0 + e)

    out = jnp.where(lane == 0, (i0 - N_EXPERT_GROUPS).astype(F32), 0.0)
    out = jnp.where(lane == 1, (i1 - N_EXPERT_GROUPS).astype(F32), out)
    out = jnp.where(lane == 2, w0, out)
    out = jnp.where(lane == 3, w1, out)
    route_ref[...] = out


def _out_projection(o_p, c_p, x_p, o_s, c_s, x_s, wo, wc, norm_g, wr, br, tm):
    np_, d = x_p.shape
    ns_ = x_s.shape[0]
    assert np_ % tm == 0 and ns_ % tm == 0
    nb_p, nb_s = np_ // tm, ns_ // tm
    n = np_ + ns_
    prow = lambda w: pl.BlockSpec((tm, w), lambda i: (jnp.minimum(i, nb_p - 1), 0))
    srow = lambda w: pl.BlockSpec((tm, w), lambda i: (jnp.maximum(i - nb_p, 0), 0))
    row = lambda w: pl.BlockSpec((tm, w), lambda i: (i, 0))
    full = lambda a: pl.BlockSpec(a.shape, lambda i: (0, 0))
    return pl.pallas_call(
        functools.partial(_outproj_kernel, nb_p=nb_p),
        grid=(nb_p + nb_s,),
        in_specs=[prow(o_p.shape[1]), prow(c_p.shape[1]), prow(d), srow(o_s.shape[1]), srow(c_s.shape[1]), srow(d),
                  full(wo), full(wc), full(norm_g), full(wr), full(br)],
        out_specs=[row(d), row(d), row(LANES)],
        out_shape=[jax.ShapeDtypeStruct((n, d), F32), jax.ShapeDtypeStruct((n, d), F32),
                   jax.ShapeDtypeStruct((n, LANES), F32)],
        compiler_params=_cparams(("arbitrary",)),
        name="out_projection_router",
    )(o_p, c_p, x_p, o_s, c_s, x_s, wo, wc, norm_g, wr, br)


DMA_UNROLL = 8


def _for_rows(n, fn):
    assert n % DMA_UNROLL == 0

    def group(gi, carry):
        for k in range(DMA_UNROLL):
            fn(gi * DMA_UNROLL + k)
        return carry

    lax.fori_loop(0, n // DMA_UNROLL, group, 0)


def _moe_kernel(order_ref, cexp_ref, cbase_ref, cnv_ref, h_ref, wg_ref, wu_ref, wd_ref, out_ref,
                xbuf, obuf, wgb, wub, wdb, gsem, ssem, *, n_tok, n_chunks, ch):
    g = pl.program_id(0)
    slot = g % 2

    def for_rows(fn, unrolled):
        if unrolled:
            for i in range(ch):
                fn(i)
        else:
            _for_rows(ch, fn)

    def issue_gather(chunk, nv_c, buf_slot, unrolled):
        base = cbase_ref[chunk]

        def one(i):
            tok = jnp.where(i < nv_c, lax.shift_right_logical(order_ref[base + i], 1), 0)
            pltpu.make_async_copy(h_ref.at[pl.ds(tok, 1)], xbuf.at[buf_slot, pl.ds(i, 1)], gsem.at[buf_slot]).start()
        for_rows(one, unrolled)

    def issue_scatter(chunk, nv_c, buf_slot, unrolled):
        base = cbase_ref[chunk]

        def one(i):
            pair = order_ref[base + i]
            row = jnp.where(i < nv_c, (pair & 1) * n_tok + lax.shift_right_logical(pair, 1), 2 * n_tok + i)
            pltpu.make_async_copy(obuf.at[buf_slot, pl.ds(i, 1)], out_ref.at[pl.ds(row, 1)], ssem.at[buf_slot]).start()
        for_rows(one, unrolled)

    def gather_wait(buf_slot):
        pltpu.make_async_copy(h_ref.at[pl.ds(0, ch)], xbuf.at[buf_slot], gsem.at[buf_slot]).wait()

    def scatter_wait(buf_slot):
        pltpu.make_async_copy(obuf.at[buf_slot], out_ref.at[pl.ds(0, ch)], ssem.at[buf_slot]).wait()

    def compute(buf_slot):
        x = xbuf[buf_slot].astype(BF16)
        gt = jnp.dot(x, wgb[...], preferred_element_type=F32)
        up = jnp.dot(x, wub[...], preferred_element_type=F32)
        hid = (gt * (1.0 / (1.0 + jnp.exp(-gt)))) * up
        obuf[buf_slot] = jnp.dot(hid.astype(BF16), wdb[...], preferred_element_type=F32)

    nv = cnv_ref[g]
    nv_prev = cnv_ref[jnp.maximum(g - 1, 0)]
    nv_next = cnv_ref[jnp.minimum(g + 1, n_chunks - 1)]
    real = nv > 0
    real_prev = jnp.logical_and(g >= 1, nv_prev > 0)
    real_next = jnp.logical_and(g + 1 < n_chunks, nv_next > 0)
    interior = jnp.logical_and(real_prev, real_next)

    @pl.when(jnp.logical_and(real, jnp.logical_or(g == 0, cexp_ref[g] != cexp_ref[jnp.maximum(g - 1, 0)])))
    def _():
        wgb[...] = wg_ref[0].astype(BF16)
        wub[...] = wu_ref[0].astype(BF16)
        wdb[...] = wd_ref[0].astype(BF16)

    @pl.when(interior)
    def _():
        gather_wait(slot)
        issue_gather(g + 1, nv_next, 1 - slot, True)
        issue_scatter(g - 1, nv_prev, 1 - slot, True)
        compute(slot)
        scatter_wait(1 - slot)

    @pl.when(jnp.logical_not(interior))
    def _():
        @pl.when(g == 0)
        def _():
            obuf[1] = jnp.zeros(obuf.shape[1:], obuf.dtype)
            fill = pltpu.make_async_copy(obuf.at[1], out_ref.at[pl.ds(2 * n_tok, ch)], ssem.at[1])
            fill.start()
            fill.wait()

        @pl.when(jnp.logical_and(g == 0, real))
        def _():
            issue_gather(0, nv, 0, False)

        @pl.when(real)
        def _():
            gather_wait(slot)

        @pl.when(real_next)
        def _():
            issue_gather(g + 1, nv_next, 1 - slot, False)

        @pl.when(real_prev)
        def _():
            issue_scatter(g - 1, nv_prev, 1 - slot, False)

        @pl.when(real)
        def _():
            compute(slot)

        @pl.when(real_prev)
        def _():
            scatter_wait(1 - slot)


def _moe(order, cexp, cbase, cnv, h2, w_gate, w_up, w_down, ch):
    n, d = h2.shape
    ne, _, de = w_gate.shape
    n_chunks = cexp.shape[0]
    wspec = lambda shp: pl.BlockSpec((1,) + shp, lambda g, order, cexp, cbase, cnv: (cexp[g], 0, 0))
    return pl.pallas_call(
        functools.partial(_moe_kernel, n_tok=n, n_chunks=n_chunks, ch=ch),
        grid_spec=pltpu.PrefetchScalarGridSpec(
            num_scalar_prefetch=4,
            grid=(n_chunks,),
            in_specs=[
                pl.BlockSpec(memory_space=pl.ANY),
                wspec((d, de)), wspec((d, de)), wspec((de, d)),
            ],
            out_specs=pl.BlockSpec(memory_space=pl.ANY),
            scratch_shapes=[
                pltpu.VMEM((2, ch, d), F32),
                pltpu.VMEM((2, ch, d), F32),
                pltpu.VMEM((d, de), BF16),
                pltpu.VMEM((d, de), BF16),
                pltpu.VMEM((de, d), BF16),
                pltpu.SemaphoreType.DMA((2,)),
                pltpu.SemaphoreType.DMA((2,)),
            ],
        ),
        out_shape=jax.ShapeDtypeStruct((2 * n + ch, d), F32),
        compiler_params=_cparams(("arbitrary",)),
        name="expert_mlp",
    )(order, cexp, cbase, cnv, h2, w_gate, w_up, w_down)


def _expert_chunks(eid, ch):
    n_pairs = eid.shape[0]
    n_chunks = N_EXPERTS + n_pairs // ch + 1
    order = jnp.argsort(eid, stable=True).astype(jnp.int32)
    counts = jnp.sum(eid[:, None] == jnp.arange(N_EXPERTS, dtype=jnp.int32)[None, :], axis=0, dtype=jnp.int32)
    offs = jnp.cumsum(counts, dtype=jnp.int32) - counts
    nch = (counts + ch - 1) // ch
    cum = jnp.cumsum(nch, dtype=jnp.int32)
    gidx = jnp.arange(n_chunks, dtype=jnp.int32)
    cexp_raw = jnp.sum(gidx[:, None] >= cum[None, :], axis=1, dtype=jnp.int32)
    last = jnp.maximum(jnp.max(jnp.where(nch > 0, jnp.arange(N_EXPERTS, dtype=jnp.int32), 0)), 0)
    used = gidx < cum[-1]
    cexp = jnp.where(used, jnp.minimum(cexp_raw, N_EXPERTS - 1), last)
    local = gidx - (cum[cexp] - nch[cexp])
    cbase = jnp.where(used, offs[cexp] + local * ch, 0)
    cnv = jnp.where(used, jnp.clip(counts[cexp] - local * ch, 0, ch), 0)
    order = jnp.concatenate([order, jnp.zeros((ch,), jnp.int32)])
    return order, cexp.astype(jnp.int32), cbase.astype(jnp.int32), cnv.astype(jnp.int32)


def _final_kernel(x2_ref, ya_ref, yb_ref, route_ref, g_ref, yp_ref, ys_ref, *, nb_p):
    i = pl.program_id(0)
    rt = route_ref[...]
    x3 = x2_ref[...] + rt[:, 2:3] * ya_ref[...] + rt[:, 3:4] * yb_ref[...]
    r = lax.rsqrt(jnp.mean(x3 * x3, axis=-1, keepdims=True) + EPS)
    y = (x3 * r) * g_ref[...]

    @pl.when(i < nb_p)
    def _():
        yp_ref[...] = y

    @pl.when(i >= nb_p)
    def _():
        ys_ref[...] = y


def _final(x2, y2, route, norm_g, np_, tm):
    n, d = x2.shape
    assert np_ % tm == 0 and n % tm == 0
    nb, nb_p = n // tm, np_ // tm
    return pl.pallas_call(
        functools.partial(_final_kernel, nb_p=nb_p),
        grid=(nb,),
        in_specs=[
            pl.BlockSpec((tm, d), lambda i: (i, 0)),
            pl.BlockSpec((tm, d), lambda i: (i, 0)),
            pl.BlockSpec((tm, d), lambda i: (i + nb, 0)),
            pl.BlockSpec((tm, LANES), lambda i: (i, 0)),
            pl.BlockSpec((1, d), lambda i: (0, 0)),
        ],
        out_specs=[
            pl.BlockSpec((tm, d), lambda i: (jnp.minimum(i, nb_p - 1), 0)),
            pl.BlockSpec((tm, d), lambda i: (jnp.maximum(i - nb_p, 0), 0)),
        ],
        out_shape=[jax.ShapeDtypeStruct((np_, d), F32), jax.ShapeDtypeStruct((n - np_, d), F32)],
        compiler_params=_cparams(("arbitrary",)),
        name="combine_final_norm",
    )(x2, y2, y2, route, norm_g)


def _pick_tile(n, pref):
    t = min(n, pref)
    while n % t:
        t //= 2
    return t


def kernel(x_prompt, x_sample, cache_k, cache_v, state_conv, page_table, norm_mix, w_in, w_dw, b_dw,
           conv_norm_g, conv_norm_b, attn_out_norm, w_out, norm_ffn, w_router_group, b_router_group,
           w_router_expert, b_router_expert, w_gate, w_up, w_down, norm_final):
    depth = norm_mix.shape[0]
    assert depth == 1
    b, t, d = x_prompt.shape
    db, s, _ = x_sample.shape
    n_pages = page_table.shape[1]
    page = cache_k.shape[2]
    past = n_pages * page
    nbf = past // MOBA_BLOCK
    assert past % MOBA_BLOCK == 0 and nbf >= MOBA_TOPK and nbf <= LANES
    ppb = MOBA_BLOCK // page
    d_conv = d - D_ATTN
    assert d_conv == D_ATTN
    np_, ns_ = b * t, db * s
    ntot = np_ + ns_

    slopes = jnp.exp2(-8.0 * jnp.arange(1, N_HEADS + 1, dtype=F32) / N_HEADS)
    w_in_bf = w_in[0].astype(BF16)
    wo_bf = w_out[0, :D_ATTN].astype(BF16)
    wc_bf = w_out[0, D_ATTN:].astype(BF16)
    wr = jnp.zeros((d, LANES), F32)
    wr = wr.at[:, :N_EXPERT_GROUPS].set(w_router_group[0])
    wr = wr.at[:, N_EXPERT_GROUPS:N_EXPERT_GROUPS + N_EXPERTS].set(w_router_expert[0])
    wr_hi = wr.astype(BF16)
    wr_lo = (wr - wr_hi.astype(F32)).astype(BF16)
    wr_cat = jnp.concatenate([wr_hi, wr_lo], axis=1)
    br = jnp.zeros((1, LANES), F32)
    br = br.at[0, :N_EXPERT_GROUPS].set(b_router_group[0])
    br = br.at[0, N_EXPERT_GROUPS:N_EXPERT_GROUPS + N_EXPERTS].set(b_router_expert[0])
    g_mix = norm_mix[0].reshape(1, d)
    g_ffn = norm_ffn[0].reshape(1, d)
    g_fin = norm_final.reshape(1, d)
    attn_norm = attn_out_norm[0].reshape(1, D_ATTN)
    conv_vec = lambda a: a[0].reshape(1, d_conv)

    xp = x_prompt.reshape(np_, d)
    xs = x_sample.reshape(ns_, d)
    qp, kp, vp, glu_p = _in_projection(xp, g_mix, w_in_bf, _pick_tile(np_, 512))
    qs, ks, vs, glu_s = _in_projection(xs, g_mix, w_in_bf, _pick_tile(ns_, 256))

    glu_p3 = glu_p.reshape(b, t, d_conv)
    glu_s3 = glu_s.reshape(db, s, d_conv)
    pad = HALO - (CONV_WIDTH - 1)
    st_p = jnp.zeros((b, HALO, d_conv), F32)
    st_s = jnp.pad(state_conv[0], ((0, 0), (pad, 0), (0, 0)))
    conv_args = (w_dw[0], conv_vec(b_dw), conv_vec(conv_norm_g), conv_vec(conv_norm_b))
    c_p = _conformer_conv(glu_p3, st_p, *conv_args, tt=_pick_tile(t, 256), nseq=1, out_dtype=BF16)
    c_s = _conformer_conv(glu_s3, st_s, *conv_args, tt=s, nseq=db, out_dtype=F32)
    conv_prompt = glu_p3[:, t - (CONV_WIDTH - 1):][None]
    conv_sample = jnp.concatenate([state_conv[0], glu_s3], axis=1)[:, -(CONV_WIDTH - 1):][None]

    ck = cache_k[0]
    cv = cache_v[0]
    kpi = max(k for k in (4, 2, 1) if (t // MOBA_BLOCK) % k == 0)
    o_p, kmean = _moba_prompt(qp.reshape(b, t, D_ATTN), kp.reshape(b, t, D_ATTN), vp.reshape(b, t, D_ATTN),
                              slopes, attn_norm, kpi, page_table, ck)

    kmean_t = jnp.pad(kmean.transpose(0, 2, 1, 3), ((0, 0), (0, 0), (0, LANES - nbf), (0, 0)))
    to_heads = lambda a: a.reshape(db, s, N_HEADS, HEAD_DIM).transpose(0, 2, 1, 3)
    q_t, kn_t, vn_t = to_heads(qs), to_heads(ks), to_heads(vs)
    idx = _sample_topk(q_t, kmean_t, nbf)[..., :MOBA_TOPK]
    o_t = _moba_sample(page_table.reshape(-1), idx.reshape(-1), slopes, q_t, kn_t, vn_t,
                       attn_norm.reshape(N_HEADS, HEAD_DIM), ck, cv, past)
    o_s = o_t.transpose(0, 2, 1, 3).reshape(ns_, D_ATTN)

    tm = _pick_tile(ns_, 256)
    x2, h2, route = _out_projection(o_p.reshape(np_, D_ATTN), c_p.reshape(np_, d_conv), xp,
                                    o_s, c_s.reshape(ns_, d_conv), xs,
                                    wo_bf, wc_bf, g_ffn, wr_cat, br, tm)

    ch = 256
    eid = route[:, 0:2].astype(jnp.int32).reshape(-1)
    order, cexp, cbase, cnv = _expert_chunks(eid, ch)
    y2 = _moe(order, cexp, cbase, cnv, h2, w_gate[0], w_up[0], w_down[0], ch)

    y_p, y_s = _final(x2, y2, route, g_fin, np_, tm)

    page_shape = (depth, b, t // page, page, N_HEADS, HEAD_DIM)
    return (y_p.reshape(b, t, d), y_s.reshape(db, s, d),
            kp.reshape(page_shape), vp.reshape(page_shape), conv_prompt,
            ks.reshape(depth, db, s, N_HEADS, HEAD_DIM), vs.reshape(depth, db, s, N_HEADS, HEAD_DIM),
            conv_sample)
```

```python
import functools

import jax
import jax.numpy as jnp
from jax import lax
from jax.experimental import pallas as pl
from jax.experimental.pallas import tpu as pltpu

F32 = jnp.float32
BF16 = jnp.bfloat16

N_HEADS = 8
HEAD_DIM = 128
D_ATTN = N_HEADS * HEAD_DIM
CONV_GROUPS = 8
CONV_WIDTH = 31
MOBA_BLOCK = 256
MOBA_TOPK = 3
N_EXPERT_GROUPS = 4
EXPERTS_PER_GROUP = 8
N_EXPERTS = N_EXPERT_GROUPS * EXPERTS_PER_GROUP
EPS = 1e-6
NEG_INF = -1e30

LANES = 128
SUBLANES = 8
HALO = 32
VMEM_LIMIT = 56 * 1024 * 1024


def _cparams(sem=None):
    return pltpu.CompilerParams(dimension_semantics=sem, vmem_limit_bytes=VMEM_LIMIT)


def _nt_dot(a, b, precision=None):
    return lax.dot_general(a, b, (((1,), (1,)), ((), ())), precision=precision,
                           preferred_element_type=F32)


def _inproj_kernel(x_ref, g_ref, w_ref, wgate_ref, q_ref, k_ref, v_ref, glu_ref, h_ref):
    j = pl.program_id(1)

    @pl.when(j == 0)
    def _():
        x = x_ref[...]
        r = lax.rsqrt(jnp.mean(x * x, axis=-1, keepdims=True) + EPS)
        h_ref[...] = ((x * r) * g_ref[...]).astype(BF16)

    z = jnp.dot(h_ref[...], w_ref[...], preferred_element_type=F32)

    @pl.when(j == 0)
    def _():
        q_ref[...] = z

    @pl.when(j == 1)
    def _():
        k_ref[...] = z

    @pl.when(j == 2)
    def _():
        v_ref[...] = z

    @pl.when(j == 3)
    def _():
        gate = jnp.dot(h_ref[...], wgate_ref[...], preferred_element_type=F32)
        glu_ref[...] = z * (1.0 / (1.0 + jnp.exp(-gate)))


def _in_projection(x, norm_g, w_in_bf, tm):
    n, d = x.shape
    dc = D_ATTN
    out = jax.ShapeDtypeStruct((n, dc), F32)
    row = pl.BlockSpec((tm, dc), lambda i, j: (i, 0))
    return pl.pallas_call(
        _inproj_kernel,
        grid=(n // tm, 4),
        in_specs=[
            pl.BlockSpec((tm, d), lambda i, j: (i, 0)),
            pl.BlockSpec((1, d), lambda i, j: (0, 0)),
            pl.BlockSpec((d, dc), lambda i, j: (0, j)),
            pl.BlockSpec((d, dc), lambda i, j: (0, 4)),
        ],
        out_specs=[row, row, row, row],
        out_shape=[out, out, out, out],
        scratch_shapes=[pltpu.VMEM((tm, d), BF16)],
        compiler_params=_cparams(("arbitrary", "arbitrary")),
        name="in_projection",
    )(x, norm_g, w_in_bf, w_in_bf)


def _conv_kernel(glu_ref, st_ref, w_ref, b_ref, gg_ref, gb_ref, c_ref, xs_ref, *, tt, rc, nseq, carry, preshift):
    t = pl.program_id(1)
    off = HALO - (CONV_WIDTH - 1)
    span = HALO + tt - SUBLANES

    def lane_block(sq, cs):
        for r0 in range(0, tt, rc):
            acc = jnp.broadcast_to(b_ref[:, cs], (rc, LANES))
            for j in range(CONV_WIDTH):
                o = off + j
                if preshift:
                    lo = r0 + (o // SUBLANES) * SUBLANES
                    x = xs_ref[o % SUBLANES, lo:lo + rc, cs]
                else:
                    x = xs_ref[0, r0 + o:r0 + o + rc, cs]
                acc = acc + x * w_ref[j:j + 1, cs]
            mu = jnp.mean(acc, axis=-1, keepdims=True)
            dlt = acc - mu
            var = jnp.mean(dlt * dlt, axis=-1, keepdims=True)
            yn = dlt * lax.rsqrt(var + EPS)
            yn = yn * gg_ref[:, cs] + gb_ref[:, cs]
            c_ref[sq, r0:r0 + rc, cs] = (yn * (1.0 / (1.0 + jnp.exp(-yn)))).astype(c_ref.dtype)

    def one_seq(sq):
        if carry:
            @pl.when(t == 0)
            def _():
                xs_ref[0, 0:HALO, :] = st_ref[sq]
        else:
            xs_ref[0, 0:HALO, :] = st_ref[sq]
        xs_ref[0, HALO:HALO + tt, :] = glu_ref[sq]
        if preshift:
            for r in range(1, SUBLANES):
                xs_ref[r, 0:span, :] = xs_ref[0, r:r + span, :]

            def group(cb, carry_):
                lane_block(sq, pl.ds(pl.multiple_of(cb * LANES, LANES), LANES))
                return carry_

            lax.fori_loop(0, CONV_GROUPS, group, 0)
        else:
            for cb in range(CONV_GROUPS):
                lane_block(sq, pl.ds(cb * LANES, LANES))
        if carry:
            xs_ref[0, 0:HALO, :] = xs_ref[0, tt:tt + HALO, :]

    if nseq == 1:
        one_seq(0)
    else:
        def seq(sq, carry_):
            one_seq(sq)
            return carry_

        lax.fori_loop(0, nseq, seq, 0)


def _conformer_conv(glu, state_pad, w_dw, b_dw, gn_g, gn_b, tt, nseq, out_dtype):
    b, t, dc = glu.shape
    assert dc == CONV_GROUPS * LANES and b % nseq == 0 and t % tt == 0
    nt = t // tt
    assert nt == 1 or nseq == 1
    preshift = tt >= 4 * SUBLANES
    rc = min(tt, 64)
    vec = pl.BlockSpec((1, dc), lambda i, j: (0, 0))
    return pl.pallas_call(
        functools.partial(_conv_kernel, tt=tt, rc=rc, nseq=nseq, carry=nt > 1, preshift=preshift),
        grid=(b // nseq, nt),
        in_specs=[
            pl.BlockSpec((nseq, tt, dc), lambda i, j: (i, j, 0)),
            pl.BlockSpec((nseq, HALO, dc), lambda i, j: (i, 0, 0)),
            pl.BlockSpec((CONV_WIDTH, dc), lambda i, j: (0, 0)),
            vec, vec, vec,
        ],
        out_specs=pl.BlockSpec((nseq, tt, dc), lambda i, j: (i, j, 0)),
        out_shape=jax.ShapeDtypeStruct((b, t, dc), out_dtype),
        scratch_shapes=[pltpu.VMEM((SUBLANES if preshift else 1, HALO + tt, dc), F32)],
        compiler_params=_cparams(("arbitrary", "arbitrary")),
        name="conformer_conv",
    )(glu, state_pad, w_dw, b_dw, gn_g, gn_b)


def _moba_prompt_kernel(pt_ref, slopes_ref, q_ref, k_ref, v_ref, nrm_ref, ck_ref, o_ref, kmean_ref,
                        qb_ref, kb_ref, vt_ref, g_ref, biasp_ref, biasd_ref, pbuf, kmbuf, psem, osem,
                        *, nblk, kpi, n_steps, n_chunks, pps, n_pages):
    h = pl.program_id(1)
    qi = pl.program_id(2)
    blk = MOBA_BLOCK
    t = q_ref.shape[1]
    nb8 = -(-nblk // SUBLANES) * SUBLANES
    slope = slopes_ref[h]

    step = (pl.program_id(0) * N_HEADS + h) * nblk + qi
    ppb = blk // pbuf.shape[2]

    def page_copy(chunk, i):
        slot = chunk % 2
        return pltpu.make_async_copy(ck_ref.at[pt_ref[chunk * pps + i]], pbuf.at[slot, i], psem.at[slot])

    def start_chunk(chunk):
        for i in range(pps):
            page_copy(chunk, i).start()

    def finish_chunk(chunk):
        slot = chunk % 2
        for i in range(pps):
            pltpu.make_async_copy(ck_ref.at[0], pbuf.at[slot, i], psem.at[slot]).wait()
        bi = (chunk * pps) // n_pages
        blk0 = ((chunk * pps) % n_pages) // ppb
        for i in range(pps // ppb):
            s = jnp.sum(pbuf[slot, ppb * i], axis=0)
            for p in range(1, ppb):
                s = s + jnp.sum(pbuf[slot, ppb * i + p], axis=0)
            kmbuf[bi, blk0 + i] = s * (1.0 / blk)

    @pl.when(step == 0)
    def _():
        start_chunk(0)

    @pl.when(step + 1 < n_chunks)
    def _():
        start_chunk(step + 1)

    @pl.when(qi == 0)
    def _():
        q = q_ref[0]
        kf = k_ref[0]
        qb_ref[...] = (q * (HEAD_DIM ** -0.5)).astype(BF16)
        kb_ref[...] = kf.astype(BF16)
        vt_ref[...] = v_ref[0].T.astype(BF16)
        km = jnp.sum(kf.reshape(nblk, blk, HEAD_DIM), axis=1) * (1.0 / blk)
        if nb8 > nblk:
            km = jnp.concatenate([km, jnp.zeros((nb8 - nblk, HEAD_DIM), F32)], axis=0)
        gate = _nt_dot(km, q, precision=lax.Precision.HIGHEST)
        j_io = lax.broadcasted_iota(jnp.int32, gate.shape, 0)
        t_io = lax.broadcasted_iota(jnp.int32, gate.shape, 1)
        valid = j_io * blk + (blk - 1) < t_io - (t_io % blk)
        g = jnp.where(valid, gate, NEG_INF)
        sel = jnp.zeros(gate.shape, jnp.bool_)
        for _ in range(MOBA_TOPK):
            m = jnp.max(g, axis=0, keepdims=True)
            idx = jnp.min(jnp.where(g == m, j_io, nb8), axis=0, keepdims=True)
            pick = j_io == idx
            sel = jnp.logical_or(sel, pick)
            g = jnp.where(pick, -jnp.inf, g)
        gb = (slope * blk) * j_io.astype(F32) + jnp.where(jnp.logical_and(sel, valid), 0.0, NEG_INF)
        for j in range(nblk):
            g_ref[j] = jnp.broadcast_to(gb[j:j + 1, :], (SUBLANES, t))
        c = lax.broadcasted_iota(jnp.int32, (kpi * blk, blk), 0)
        biasp_ref[...] = slope * (c % blk).astype(F32)
        c = lax.broadcasted_iota(jnp.int32, (blk, blk), 0)
        r = lax.broadcasted_iota(jnp.int32, (blk, blk), 1)
        biasd_ref[...] = slope * c.astype(F32) + jnp.where(c > r, NEG_INF, 0.0)

    own = pl.ds(pl.multiple_of(qi * blk, blk), blk)
    qs = qb_ref[own, :]

    g_own = (slope * blk) * qi.astype(F32)
    st = _nt_dot(kb_ref[own, :], qs) + biasd_ref[...]
    m0 = jnp.max(st, axis=0, keepdims=True) + g_own
    p = jnp.exp(st - (m0 - g_own))
    l0 = jnp.sum(p, axis=0, keepdims=True)
    acc0 = jnp.dot(vt_ref[:, own], p.astype(BF16), preferred_element_type=F32)

    def past_blocks(jp, carry):
        m, l, acc = carry
        rows = pl.ds(pl.multiple_of(jp * (kpi * blk), kpi * blk), kpi * blk)
        st = _nt_dot(kb_ref[rows, :], qs) + biasp_ref[...]
        segs = [st[i * blk:(i + 1) * blk] for i in range(kpi)]
        gs = [g_ref[jp * kpi + i, 0:1, own] for i in range(kpi)]
        m_new = m
        for seg, g in zip(segs, gs):
            m_new = jnp.maximum(m_new, jnp.max(seg, axis=0, keepdims=True) + g)
        alpha = jnp.exp(m - m_new)
        p = jnp.concatenate([jnp.exp(seg - (m_new - g)) for seg, g in zip(segs, gs)], axis=0)
        l = alpha * l + jnp.sum(p, axis=0, keepdims=True)
        acc = alpha * acc + jnp.dot(vt_ref[:, rows], p.astype(BF16), preferred_element_type=F32)
        return m_new, l, acc

    _, l, acc = lax.fori_loop(0, (qi + kpi - 1) // kpi, past_blocks, (m0, l0, acc0))
    ot = acc * (1.0 / l)
    ot = ot * lax.rsqrt(jnp.mean(ot * ot, axis=0, keepdims=True) + EPS)
    o_ref[0] = (ot.T * nrm_ref[...]).astype(o_ref.dtype)

    if n_chunks == n_steps:
        finish_chunk(step)
    else:
        @pl.when(step < n_chunks)
        def _():
            finish_chunk(step)

    @pl.when(step == n_steps - 1)
    def _():
        out = pltpu.make_async_copy(kmbuf, kmean_ref, osem.at[0])
        out.start()
        out.wait()


def _moba_prompt(q, k, v, slopes, attn_norm, kpi, page_table, cache_k):
    b, t, _ = q.shape
    blk = MOBA_BLOCK
    nblk = t // blk
    assert t % blk == 0 and nblk % kpi == 0
    db, n_pages = page_table.shape
    _, page, nh, hd = cache_k.shape
    ppb = blk // page
    n_steps = b * N_HEADS * nblk
    pps = -(-(db * n_pages) // n_steps)
    pps = -(-pps // ppb) * ppb
    assert n_pages % pps == 0
    n_chunks = db * n_pages // pps
    assert n_chunks <= n_steps
    seq = pl.BlockSpec((1, t, HEAD_DIM), lambda bi, h, qi, pt: (bi, 0, h))
    tile = pl.BlockSpec((1, blk, HEAD_DIM), lambda bi, h, qi, pt: (bi, qi, h))
    return pl.pallas_call(
        functools.partial(_moba_prompt_kernel, nblk=nblk, kpi=kpi, n_steps=n_steps, n_chunks=n_chunks, pps=pps,
                          n_pages=n_pages),
        grid_spec=pltpu.PrefetchScalarGridSpec(
            num_scalar_prefetch=1,
            grid=(b, N_HEADS, nblk),
            in_specs=[
                pl.BlockSpec(memory_space=pltpu.SMEM),
                seq, seq, seq,
                pl.BlockSpec((1, HEAD_DIM), lambda bi, h, qi, pt: (0, h)),
                pl.BlockSpec(memory_space=pl.ANY),
            ],
            out_specs=[tile, pl.BlockSpec(memory_space=pl.ANY)],
            scratch_shapes=[
                pltpu.VMEM((t, HEAD_DIM), BF16),
                pltpu.VMEM((t, HEAD_DIM), BF16),
                pltpu.VMEM((HEAD_DIM, t), BF16),
                pltpu.VMEM((nblk, SUBLANES, t), F32),
                pltpu.VMEM((kpi * blk, blk), F32),
                pltpu.VMEM((blk, blk), F32),
                pltpu.VMEM((2, pps, page, nh, hd), F32),
                pltpu.VMEM((db, n_pages // ppb, nh, hd), F32),
                pltpu.SemaphoreType.DMA((2,)),
                pltpu.SemaphoreType.DMA((1,)),
            ],
        ),
        out_shape=[jax.ShapeDtypeStruct((b, t, D_ATTN), BF16),
                   jax.ShapeDtypeStruct((db, n_pages // ppb, nh, hd), F32)],
        compiler_params=_cparams(("arbitrary", "arbitrary", "arbitrary")),
        name="moba_prompt",
    )(page_table.reshape(-1), slopes, q, k, v, attn_norm, cache_k)


def _sample_topk_kernel(q_ref, km_ref, idx_ref, *, nbf):
    for h in range(N_HEADS):
        gate = _nt_dot(q_ref[0, h], km_ref[0, h], precision=lax.Precision.HIGHEST)
        lane = lax.broadcasted_iota(jnp.int32, gate.shape, 1)
        g = jnp.where(lane < nbf, gate, -jnp.inf)
        out = jnp.zeros(gate.shape, jnp.int32)
        for r in range(MOBA_TOPK):
            m = jnp.max(g, axis=-1, keepdims=True)
            idx = jnp.min(jnp.where(g == m, lane, LANES), axis=-1, keepdims=True)
            out = jnp.where(lane == r, idx, out)
            g = jnp.where(lane == idx, -jnp.inf, g)
        idx_ref[0, h] = out


def _sample_topk(q_t, kmean_t, nbf):
    db, nh, s, hd = q_t.shape
    return pl.pallas_call(
        functools.partial(_sample_topk_kernel, nbf=nbf),
        grid=(db,),
        in_specs=[
            pl.BlockSpec((1, nh, s, hd), lambda i: (i, 0, 0, 0)),
            pl.BlockSpec((1, nh, LANES, hd), lambda i: (i, 0, 0, 0)),
        ],
        out_specs=pl.BlockSpec((1, nh, s, LANES), lambda i: (i, 0, 0, 0)),
        out_shape=jax.ShapeDtypeStruct((db, nh, s, LANES), jnp.int32),
        compiler_params=_cparams(("arbitrary",)),
        name="sample_topk",
    )(q_t, kmean_t)


def _moba_sample_kernel(pt_ref, blk_ref, slopes_ref, q_ref, kn_ref, vn_ref, nrm_ref, ck_ref, cv_ref, o_ref,
                        kbuf, vbuf, sem, *, n_units, n_sel, ppb, past):
    page = ck_ref.shape[1]
    n_slab = n_sel * ppb
    n_keys = n_slab * page
    n_pages = past // page
    s_len = q_ref.shape[2]
    rows = 16

    def copies(u, slot, i, h, pg):
        dst = pl.ds(pl.multiple_of(i * page, page), page)
        return (pltpu.make_async_copy(ck_ref.at[pg, :, h, :], kbuf.at[slot, dst], sem.at[0, slot]),
                pltpu.make_async_copy(cv_ref.at[pg, :, h, :], vbuf.at[slot, dst], sem.at[1, slot]))

    def start(u, slot):
        h = u % N_HEADS
        row0 = (u // N_HEADS) * n_pages

        def sel(n, carry):
            logical = blk_ref[u * n_sel + n] * ppb
            for p in range(ppb):
                ck, cv = copies(u, slot, n * ppb + p, h, pt_ref[row0 + logical + p])
                ck.start()
                cv.start(priority=1)
            return carry

        lax.fori_loop(0, n_sel, sel, 0)

    u = pl.program_id(0)

    @pl.when(u == 0)
    def _():
        start(0, 0)

    slot = u % 2
    bi = u // N_HEADS
    h = u % N_HEADS

    @pl.when(u + 1 < n_units)
    def _():
        start(u + 1, 1 - slot)

    def slab_done(i, carry):
        ck, cv = copies(u, slot, i, 0, 0)
        ck.wait()
        cv.wait()
        return carry

    lax.fori_loop(0, n_slab, slab_done, 0)

    slope = slopes_ref[h]
    q = q_ref[bi, h]
    qs = jnp.concatenate([q * (HEAD_DIM ** -0.5), jnp.zeros((rows - s_len, HEAD_DIM), F32)], axis=0)
    qs = qs.astype(BF16)
    kb = kbuf[slot].astype(BF16)
    vb = vbuf[slot].astype(BF16)
    s = _nt_dot(qs, kb)

    per_blk = page * ppb
    offs = lax.broadcasted_iota(jnp.int32, (1, per_blk), 1)
    kpos = jnp.concatenate(
        [(blk_ref[u * n_sel + i] * per_blk + offs).astype(F32) for i in range(n_sel)], axis=1)
    row = lax.broadcasted_iota(jnp.int32, (rows, 1), 0)
    qpos = (past + row).astype(F32)
    s = s - slope * (qpos - kpos)
    col = lax.broadcasted_iota(jnp.int32, (rows, n_keys), 1)
    per_q = MOBA_TOPK * per_blk
    mine = jnp.logical_and(col >= row * per_q, col < (row + 1) * per_q)
    s = jnp.where(mine, s, NEG_INF)

    kn = jnp.concatenate([kn_ref[bi, h], jnp.zeros((rows - s_len, HEAD_DIM), F32)], axis=0).astype(BF16)
    vn = jnp.concatenate([vn_ref[bi, h], jnp.zeros((rows - s_len, HEAD_DIM), F32)], axis=0).astype(BF16)
    ri = lax.broadcasted_iota(jnp.int32, (rows, rows), 0)
    ci = lax.broadcasted_iota(jnp.int32, (rows, rows), 1)
    so = _nt_dot(qs, kn) - slope * (ri - ci).astype(F32)
    so = jnp.where(jnp.logical_and(ri >= ci, ci < s_len), so, NEG_INF)

    m = jnp.maximum(jnp.max(s, axis=-1, keepdims=True), jnp.max(so, axis=-1, keepdims=True))
    p = jnp.exp(s - m)
    po = jnp.exp(so - m)
    l = jnp.sum(p, axis=-1, keepdims=True) + jnp.sum(po, axis=-1, keepdims=True)
    acc = (jnp.dot(p.astype(BF16), vb, preferred_element_type=F32)
           + jnp.dot(po.astype(BF16), vn, preferred_element_type=F32))
    o = (acc / l)[0:s_len]
    o = (o * lax.rsqrt(jnp.mean(o * o, axis=-1, keepdims=True) + EPS)) * nrm_ref[pl.ds(h, 1), :]
    o_ref[bi, h] = o


def _moba_sample(page_table_flat, blks, slopes, q_t, kn_t, vn_t, attn_norm_hd, cache_k, cache_v, past):
    db, nh, s, hd = q_t.shape
    _, page, _, _ = cache_k.shape
    ppb = MOBA_BLOCK // page
    n_sel = s * MOBA_TOPK
    whole = lambda shp: pl.BlockSpec(shp, lambda i, *_: (0,) * len(shp))
    return pl.pallas_call(
        functools.partial(_moba_sample_kernel, n_units=db * nh, n_sel=n_sel, ppb=ppb, past=past),
        grid_spec=pltpu.PrefetchScalarGridSpec(
            num_scalar_prefetch=3,
            grid=(db * nh,),
            in_specs=[
                whole(q_t.shape), whole(kn_t.shape), whole(vn_t.shape), whole(attn_norm_hd.shape),
                pl.BlockSpec(memory_space=pl.ANY),
                pl.BlockSpec(memory_space=pl.ANY),
            ],
            out_specs=whole(q_t.shape),
            scratch_shapes=[
                pltpu.VMEM((2, n_sel * ppb * page, hd), F32),
                pltpu.VMEM((2, n_sel * ppb * page, hd), F32),
                pltpu.SemaphoreType.DMA((2, 2)),
            ],
        ),
        out_shape=jax.ShapeDtypeStruct(q_t.shape, F32),
        compiler_params=_cparams(("arbitrary",)),
        name="moba_sample",
    )(page_table_flat, blks, slopes, q_t, kn_t, vn_t, attn_norm_hd, cache_k, cache_v)


def _outproj_kernel(op_ref, cp_ref, xp_ref, os_ref, cs_ref, xs_ref, wo_ref, wc_ref, g_ref, wr_ref, br_ref,
                    x2_ref, h2_ref, route_ref, *, nb_p):
    i = pl.program_id(0)
    tail = (wo_ref, wc_ref, g_ref, wr_ref, br_ref, x2_ref, h2_ref, route_ref)

    @pl.when(i < nb_p)
    def _():
        _outproj_tile(op_ref, cp_ref, xp_ref, *tail)

    @pl.when(i >= nb_p)
    def _():
        _outproj_tile(os_ref, cs_ref, xs_ref, *tail)


def _outproj_tile(o_ref, c_ref, x_ref, wo_ref, wc_ref, g_ref, wr_ref, br_ref, x2_ref, h2_ref, route_ref):
    mix = (jnp.dot(o_ref[...].astype(BF16), wo_ref[...], preferred_element_type=F32)
           + jnp.dot(c_ref[...].astype(BF16), wc_ref[...], preferred_element_type=F32))
    x2 = x_ref[...] + mix
    x2_ref[...] = x2
    r = lax.rsqrt(jnp.mean(x2 * x2, axis=-1, keepdims=True) + EPS)
    h2 = (x2 * r) * g_ref[...]
    for s in range(h2_ref.shape[1]):
        h2_ref[:, s, :] = h2[:, s * LANES:(s + 1) * LANES]

    h_hi = h2.astype(BF16)
    h_lo = (h2 - h_hi.astype(F32)).astype(BF16)
    r1 = jnp.dot(h_hi, wr_ref[...], preferred_element_type=F32)
    r2 = jnp.dot(h_lo, wr_ref[:, 0:LANES], preferred_element_type=F32)
    logits = r1[:, 0:LANES] + r1[:, LANES:2 * LANES] + r2 + br_ref[...]

    lane = lax.broadcasted_iota(jnp.int32, logits.shape, 1)
    is_g = lane < N_EXPERT_GROUPS
    gl = jnp.where(is_g, logits, -jnp.inf)
    gmax = jnp.max(gl, axis=-1, keepdims=True)
    gidx = jnp.min(jnp.where(gl == gmax, lane, LANES), axis=-1, keepdims=True)
    gsum = jnp.sum(jnp.where(is_g, jnp.exp(gl - gmax), 0.0), axis=-1, keepdims=True)
    g_gate = 1.0 / gsum

    lo = N_EXPERT_GROUPS + gidx * EXPERTS_PER_GROUP
    el = jnp.where(jnp.logical_and(lane >= lo, lane < lo + EXPERTS_PER_GROUP), logits, -jnp.inf)
    v0 = jnp.max(el, axis=-1, keepdims=True)
    i0 = jnp.min(jnp.where(el == v0, lane, LANES), axis=-1, keepdims=True)
    el = jnp.where(lane == i0, -jnp.inf, el)
    v1 = jnp.max(el, axis=-1, keepdims=True)
    i1 = jnp.min(jnp.where(el == v1, lane, LANES), axis=-1, keepdims=True)
    e = jnp.exp(v1 - v0)
    w0 = g_gate / (1.0 + e)
    w1 = g_gate * e / (1.0 + e)

    out = jnp.where(lane == 0, (i0 - N_EXPERT_GROUPS).astype(F32), 0.0)
    out = jnp.where(lane == 1, (i1 - N_EXPERT_GROUPS).astype(F32), out)
    out = jnp.where(lane == 2, w0, out)
    out = jnp.where(lane == 3, w1, out)
    route_ref[...] = out


def _out_projection(o_p, c_p, x_p, o_s, c_s, x_s, wo, wc, norm_g, wr, br, tm):
    np_, d = x_p.shape
    ns_ = x_s.shape[0]
    assert np_ % tm == 0 and ns_ % tm == 0
    nb_p, nb_s = np_ // tm, ns_ // tm
    n = np_ + ns_
    prow = lambda w: pl.BlockSpec((tm, w), lambda i: (jnp.minimum(i, nb_p - 1), 0))
    srow = lambda w: pl.BlockSpec((tm, w), lambda i: (jnp.maximum(i - nb_p, 0), 0))
    row = lambda w: pl.BlockSpec((tm, w), lambda i: (i, 0))
    full = lambda a: pl.BlockSpec(a.shape, lambda i: (0, 0))
    return pl.pallas_call(
        functools.partial(_outproj_kernel, nb_p=nb_p),
        grid=(nb_p + nb_s,),
        in_specs=[prow(o_p.shape[1]), prow(c_p.shape[1]), prow(d), srow(o_s.shape[1]), srow(c_s.shape[1]), srow(d),
                  full(wo), full(wc), full(norm_g), full(wr), full(br)],
        out_specs=[row(d), pl.BlockSpec((tm, d // LANES, LANES), lambda i: (i, 0, 0)), row(LANES)],
        out_shape=[jax.ShapeDtypeStruct((n, d), F32), jax.ShapeDtypeStruct((n, d // LANES, LANES), F32),
                   jax.ShapeDtypeStruct((n, LANES), F32)],
        compiler_params=_cparams(("arbitrary",)),
        name="out_projection_router",
    )(o_p, c_p, x_p, o_s, c_s, x_s, wo, wc, norm_g, wr, br)


DMA_UNROLL = 8


def _for_rows(n, fn):
    assert n % DMA_UNROLL == 0

    def group(gi, carry):
        for k in range(DMA_UNROLL):
            fn(gi * DMA_UNROLL + k, k)
        return carry

    lax.fori_loop(0, n // DMA_UNROLL, group, 0)


def _moe_kernel(order_ref, cexp_ref, cbase_ref, cnv_ref, h_ref, wg_ref, wu_ref, wd_ref, out_ref,
                xbuf, obuf, wgb, wub, wdb, gsem, ssem, *, n_tok, n_chunks, ch):
    g = pl.program_id(0)
    slot = g % 2

    def for_rows(fn, unrolled):
        if unrolled:
            for i in range(ch):
                fn(i, i % DMA_UNROLL)
        else:
            _for_rows(ch, fn)

    def issue_gather(chunk, nv_c, buf_slot, unrolled):
        base = cbase_ref[chunk]

        def one(i, k):
            tok = jnp.where(i < nv_c, lax.shift_right_logical(order_ref[base + i], 1), 0)
            pltpu.make_async_copy(h_ref.at[tok], xbuf.at[buf_slot, i], gsem.at[buf_slot]).start(priority=1)
        for_rows(one, unrolled)

    def issue_scatter(chunk, nv_c, buf_slot, unrolled):
        base = cbase_ref[chunk]

        def one(i, k):
            pair = order_ref[base + i]
            row = jnp.where(i < nv_c, (pair & 1) * n_tok + lax.shift_right_logical(pair, 1), 2 * n_tok + i)
            pltpu.make_async_copy(obuf.at[buf_slot, i], out_ref.at[row], ssem.at[buf_slot]).start(priority=k % 2)
        for_rows(one, unrolled)

    def gather_wait(buf_slot):
        pltpu.make_async_copy(h_ref.at[pl.ds(0, ch)], xbuf.at[buf_slot], gsem.at[buf_slot]).wait()

    def scatter_wait(buf_slot):
        pltpu.make_async_copy(obuf.at[buf_slot], out_ref.at[pl.ds(0, ch)], ssem.at[buf_slot]).wait()

    n_seg = xbuf.shape[2]

    def compute(buf_slot):
        x = jnp.concatenate([xbuf[buf_slot, :, s, :] for s in range(n_seg)], axis=1).astype(BF16)
        gt = jnp.dot(x, wgb[...], preferred_element_type=F32)
        up = jnp.dot(x, wub[...], preferred_element_type=F32)
        hid = (gt * (1.0 / (1.0 + jnp.exp(-gt)))) * up
        y = jnp.dot(hid.astype(BF16), wdb[...], preferred_element_type=F32)
        for s in range(n_seg):
            obuf[buf_slot, :, s, :] = y[:, s * LANES:(s + 1) * LANES]

    nv = cnv_ref[g]
    nv_prev = cnv_ref[jnp.maximum(g - 1, 0)]
    nv_next = cnv_ref[jnp.minimum(g + 1, n_chunks - 1)]
    real = nv > 0
    real_prev = jnp.logical_and(g >= 1, nv_prev > 0)
    real_pp = jnp.logical_and(g >= 2, cnv_ref[jnp.maximum(g - 2, 0)] > 0)
    real_next = jnp.logical_and(g + 1 < n_chunks, nv_next > 0)
    interior = jnp.logical_and(real_pp, real_next)

    @pl.when(jnp.logical_and(real, jnp.logical_or(g == 0, cexp_ref[g] != cexp_ref[jnp.maximum(g - 1, 0)])))
    def _():
        wgb[...] = wg_ref[0].astype(BF16)
        wub[...] = wu_ref[0].astype(BF16)
        wdb[...] = wd_ref[0].astype(BF16)

    @pl.when(interior)
    def _():
        gather_wait(slot)
        scatter_wait(slot)
        issue_gather(g + 1, nv_next, 1 - slot, True)
        issue_scatter(g - 1, nv_prev, 1 - slot, True)
        compute(slot)

    @pl.when(jnp.logical_not(interior))
    def _():
        @pl.when(g == 0)
        def _():
            obuf[1] = jnp.zeros(obuf.shape[1:], obuf.dtype)
            fill = pltpu.make_async_copy(obuf.at[1], out_ref.at[pl.ds(2 * n_tok, ch)], ssem.at[1])
            fill.start()
            fill.wait()

        @pl.when(jnp.logical_and(g == 0, real))
        def _():
            issue_gather(0, nv, 0, False)

        @pl.when(real)
        def _():
            gather_wait(slot)

        @pl.when(real_pp)
        def _():
            scatter_wait(slot)

        @pl.when(real_next)
        def _():
            issue_gather(g + 1, nv_next, 1 - slot, False)

        @pl.when(real_prev)
        def _():
            issue_scatter(g - 1, nv_prev, 1 - slot, False)

        @pl.when(real)
        def _():
            compute(slot)


def _moe(order, cexp, cbase, cnv, h2, w_gate, w_up, w_down, ch):
    n, n_seg, _ = h2.shape
    ne, d, de = w_gate.shape
    assert d == n_seg * LANES
    n_chunks = cexp.shape[0]
    wspec = lambda shp: pl.BlockSpec((1,) + shp, lambda g, order, cexp, cbase, cnv: (cexp[g], 0, 0))
    return pl.pallas_call(
        functools.partial(_moe_kernel, n_tok=n, n_chunks=n_chunks, ch=ch),
        grid_spec=pltpu.PrefetchScalarGridSpec(
            num_scalar_prefetch=4,
            grid=(n_chunks,),
            in_specs=[
                pl.BlockSpec(memory_space=pl.ANY),
                wspec((d, de)), wspec((d, de)), wspec((de, d)),
            ],
            out_specs=pl.BlockSpec(memory_space=pl.ANY),
            scratch_shapes=[
                pltpu.VMEM((2, ch, n_seg, LANES), F32),
                pltpu.VMEM((2, ch, n_seg, LANES), F32),
                pltpu.VMEM((d, de), BF16),
                pltpu.VMEM((d, de), BF16),
                pltpu.VMEM((de, d), BF16),
                pltpu.SemaphoreType.DMA((2,)),
                pltpu.SemaphoreType.DMA((2,)),
            ],
        ),
        out_shape=jax.ShapeDtypeStruct((2 * n + ch, n_seg, LANES), F32),
        compiler_params=_cparams(("arbitrary",)),
        name="expert_mlp",
    )(order, cexp, cbase, cnv, h2, w_gate, w_up, w_down)


def _expert_chunks(eid, ch):
    n_pairs = eid.shape[0]
    n_chunks = N_EXPERTS + n_pairs // ch + 2
    order = jnp.argsort(eid, stable=True).astype(jnp.int32)
    counts = jnp.sum(eid[:, None] == jnp.arange(N_EXPERTS, dtype=jnp.int32)[None, :], axis=0, dtype=jnp.int32)
    offs = jnp.cumsum(counts, dtype=jnp.int32) - counts
    nch = (counts + ch - 1) // ch
    cum = jnp.cumsum(nch, dtype=jnp.int32)
    gidx = jnp.arange(n_chunks, dtype=jnp.int32)
    cexp_raw = jnp.sum(gidx[:, None] >= cum[None, :], axis=1, dtype=jnp.int32)
    last = jnp.maximum(jnp.max(jnp.where(nch > 0, jnp.arange(N_EXPERTS, dtype=jnp.int32), 0)), 0)
    used = gidx < cum[-1]
    cexp = jnp.where(used, jnp.minimum(cexp_raw, N_EXPERTS - 1), last)
    local = gidx - (cum[cexp] - nch[cexp])
    cbase = jnp.where(used, offs[cexp] + local * ch, 0)
    cnv = jnp.where(used, jnp.clip(counts[cexp] - local * ch, 0, ch), 0)
    order = jnp.concatenate([order, jnp.zeros((ch,), jnp.int32)])
    return order, cexp.astype(jnp.int32), cbase.astype(jnp.int32), cnv.astype(jnp.int32)


def _final_kernel(x2_ref, ya_ref, yb_ref, route_ref, g_ref, yp_ref, ys_ref, *, nb_p):
    i = pl.program_id(0)
    rt = route_ref[...]
    n_seg = ya_ref.shape[1]
    ya = jnp.concatenate([ya_ref[:, s, :] for s in range(n_seg)], axis=1)
    yb = jnp.concatenate([yb_ref[:, s, :] for s in range(n_seg)], axis=1)
    x3 = x2_ref[...] + rt[:, 2:3] * ya + rt[:, 3:4] * yb
    r = lax.rsqrt(jnp.mean(x3 * x3, axis=-1, keepdims=True) + EPS)
    y = (x3 * r) * g_ref[...]

    @pl.when(i < nb_p)
    def _():
        yp_ref[...] = y

    @pl.when(i >= nb_p)
    def _():
        ys_ref[...] = y


def _final(x2, y2, route, norm_g, np_, tm):
    n, d = x2.shape
    assert np_ % tm == 0 and n % tm == 0
    nb, nb_p = n // tm, np_ // tm
    yspec = lambda off: pl.BlockSpec((tm,) + y2.shape[1:], lambda i: (i + off, 0, 0))
    return pl.pallas_call(
        functools.partial(_final_kernel, nb_p=nb_p),
        grid=(nb,),
        in_specs=[
            pl.BlockSpec((tm, d), lambda i: (i, 0)),
            yspec(0),
            yspec(nb),
            pl.BlockSpec((tm, LANES), lambda i: (i, 0)),
            pl.BlockSpec((1, d), lambda i: (0, 0)),
        ],
        out_specs=[
            pl.BlockSpec((tm, d), lambda i: (jnp.minimum(i, nb_p - 1), 0)),
            pl.BlockSpec((tm, d), lambda i: (jnp.maximum(i - nb_p, 0), 0)),
        ],
        out_shape=[jax.ShapeDtypeStruct((np_, d), F32), jax.ShapeDtypeStruct((n - np_, d), F32)],
        compiler_params=_cparams(("arbitrary",)),
        name="combine_final_norm",
    )(x2, y2, y2, route, norm_g)


def _pick_tile(n, pref):
    t = min(n, pref)
    while n % t:
        t //= 2
    return t


def kernel(x_prompt, x_sample, cache_k, cache_v, state_conv, page_table, norm_mix, w_in, w_dw, b_dw,
           conv_norm_g, conv_norm_b, attn_out_norm, w_out, norm_ffn, w_router_group, b_router_group,
           w_router_expert, b_router_expert, w_gate, w_up, w_down, norm_final):
    depth = norm_mix.shape[0]
    assert depth == 1
    b, t, d = x_prompt.shape
    db, s, _ = x_sample.shape
    n_pages = page_table.shape[1]
    page = cache_k.shape[2]
    past = n_pages * page
    nbf = past // MOBA_BLOCK
    assert past % MOBA_BLOCK == 0 and nbf >= MOBA_TOPK and nbf <= LANES
    ppb = MOBA_BLOCK // page
    d_conv = d - D_ATTN
    assert d_conv == D_ATTN
    np_, ns_ = b * t, db * s
    ntot = np_ + ns_

    slopes = jnp.exp2(-8.0 * jnp.arange(1, N_HEADS + 1, dtype=F32) / N_HEADS)
    w_in_bf = w_in[0].astype(BF16)
    wo_bf = w_out[0, :D_ATTN].astype(BF16)
    wc_bf = w_out[0, D_ATTN:].astype(BF16)
    wr = jnp.zeros((d, LANES), F32)
    wr = wr.at[:, :N_EXPERT_GROUPS].set(w_router_group[0])
    wr = wr.at[:, N_EXPERT_GROUPS:N_EXPERT_GROUPS + N_EXPERTS].set(w_router_expert[0])
    wr_hi = wr.astype(BF16)
    wr_lo = (wr - wr_hi.astype(F32)).astype(BF16)
    wr_cat = jnp.concatenate([wr_hi, wr_lo], axis=1)
    br = jnp.zeros((1, LANES), F32)
    br = br.at[0, :N_EXPERT_GROUPS].set(b_router_group[0])
    br = br.at[0, N_EXPERT_GROUPS:N_EXPERT_GROUPS + N_EXPERTS].set(b_router_expert[0])
    g_mix = norm_mix[0].reshape(1, d)
    g_ffn = norm_ffn[0].reshape(1, d)
    g_fin = norm_final.reshape(1, d)
    attn_norm = attn_out_norm[0].reshape(1, D_ATTN)
    conv_vec = lambda a: a[0].reshape(1, d_conv)

    xp = x_prompt.reshape(np_, d)
    xs = x_sample.reshape(ns_, d)
    qp, kp, vp, glu_p = _in_projection(xp, g_mix, w_in_bf, _pick_tile(np_, 512))
    qs, ks, vs, glu_s = _in_projection(xs, g_mix, w_in_bf, _pick_tile(ns_, 256))

    glu_p3 = glu_p.reshape(b, t, d_conv)
    glu_s3 = glu_s.reshape(db, s, d_conv)
    pad = HALO - (CONV_WIDTH - 1)
    st_p = jnp.zeros((b, HALO, d_conv), F32)
    st_s = jnp.pad(state_conv[0], ((0, 0), (pad, 0), (0, 0)))
    conv_args = (w_dw[0], conv_vec(b_dw), conv_vec(conv_norm_g), conv_vec(conv_norm_b))
    c_p = _conformer_conv(glu_p3, st_p, *conv_args, tt=_pick_tile(t, 256), nseq=1, out_dtype=BF16)
    c_s = _conformer_conv(glu_s3, st_s, *conv_args, tt=s, nseq=db, out_dtype=F32)
    conv_prompt = glu_p3[:, t - (CONV_WIDTH - 1):][None]
    conv_sample = jnp.concatenate([state_conv[0], glu_s3], axis=1)[:, -(CONV_WIDTH - 1):][None]

    ck = cache_k[0]
    cv = cache_v[0]
    kpi = max(k for k in (4, 2, 1) if (t // MOBA_BLOCK) % k == 0)
    o_p, kmean = _moba_prompt(qp.reshape(b, t, D_ATTN), kp.reshape(b, t, D_ATTN), vp.reshape(b, t, D_ATTN),
                              slopes, attn_norm, kpi, page_table, ck)

    kmean_t = jnp.pad(kmean.transpose(0, 2, 1, 3), ((0, 0), (0, 0), (0, LANES - nbf), (0, 0)))
    to_heads = lambda a: a.reshape(db, s, N_HEADS, HEAD_DIM).transpose(0, 2, 1, 3)
    q_t, kn_t, vn_t = to_heads(qs), to_heads(ks), to_heads(vs)
    idx = _sample_topk(q_t, kmean_t, nbf)[..., :MOBA_TOPK]
    o_t = _moba_sample(page_table.reshape(-1), idx.reshape(-1), slopes, q_t, kn_t, vn_t,
                       attn_norm.reshape(N_HEADS, HEAD_DIM), ck, cv, past)
    o_s = o_t.transpose(0, 2, 1, 3).reshape(ns_, D_ATTN)

    tm = _pick_tile(ns_, 256)
    x2, h2, route = _out_projection(o_p.reshape(np_, D_ATTN), c_p.reshape(np_, d_conv), xp,
                                    o_s, c_s.reshape(ns_, d_conv), xs,
                                    wo_bf, wc_bf, g_ffn, wr_cat, br, tm)

    ch = 256
    eid = route[:, 0:2].astype(jnp.int32).reshape(-1)
    order, cexp, cbase, cnv = _expert_chunks(eid, ch)
    y2 = _moe(order, cexp, cbase, cnv, h2, w_gate[0], w_up[0], w_down[0], ch)

    y_p, y_s = _final(x2, y2, route, g_fin, np_, tm)

    page_shape = (depth, b, t // page, page, N_HEADS, HEAD_DIM)
    return (y_p.reshape(b, t, d), y_s.reshape(db, s, d),
            kp.reshape(page_shape), vp.reshape(page_shape), conv_prompt,
            ks.reshape(depth, db, s, N_HEADS, HEAD_DIM), vs.reshape(depth, db, s, N_HEADS, HEAD_DIM),
            conv_sample)
```

```python
import functools

import jax
import jax.numpy as jnp
from jax import lax
from jax.experimental import pallas as pl
from jax.experimental.pallas import tpu as pltpu

F32 = jnp.float32
BF16 = jnp.bfloat16

N_HEADS = 8
HEAD_DIM = 128
D_ATTN = N_HEADS * HEAD_DIM
CONV_GROUPS = 8
CONV_WIDTH = 31
MOBA_BLOCK = 256
MOBA_TOPK = 3
N_EXPERT_GROUPS = 4
EXPERTS_PER_GROUP = 8
N_EXPERTS = N_EXPERT_GROUPS * EXPERTS_PER_GROUP
EPS = 1e-6
NEG_INF = -1e30

LANES = 128
SUBLANES = 8
HALO = 32
VMEM_LIMIT = 56 * 1024 * 1024
STREAM_PAGES = 16


def _cparams(sem=None):
    return pltpu.CompilerParams(dimension_semantics=sem, vmem_limit_bytes=VMEM_LIMIT)


def _nt_dot(a, b, precision=None):
    return lax.dot_general(a, b, (((1,), (1,)), ((), ())), precision=precision,
                           preferred_element_type=F32)


def _inproj_kernel(x_ref, g_ref, w_ref, wgate_ref, q_ref, k_ref, v_ref, glu_ref, h_ref):
    j = pl.program_id(1)

    @pl.when(j == 0)
    def _():
        x = x_ref[...]
        r = lax.rsqrt(jnp.mean(x * x, axis=-1, keepdims=True) + EPS)
        h_ref[...] = ((x * r) * g_ref[...]).astype(BF16)

    z = jnp.dot(h_ref[...], w_ref[...], preferred_element_type=F32)

    @pl.when(j == 0)
    def _():
        q_ref[...] = z

    @pl.when(j == 1)
    def _():
        k_ref[...] = z

    @pl.when(j == 2)
    def _():
        v_ref[...] = z

    @pl.when(j == 3)
    def _():
        gate = jnp.dot(h_ref[...], wgate_ref[...], preferred_element_type=F32)
        glu_ref[...] = z * (1.0 / (1.0 + jnp.exp(-gate)))


def _in_projection(x, norm_g, w_in_bf, tm):
    n, d = x.shape
    dc = D_ATTN
    out = jax.ShapeDtypeStruct((n, dc), F32)
    row = pl.BlockSpec((tm, dc), lambda i, j: (i, 0))
    return pl.pallas_call(
        _inproj_kernel,
        grid=(n // tm, 4),
        in_specs=[
            pl.BlockSpec((tm, d), lambda i, j: (i, 0)),
            pl.BlockSpec((1, d), lambda i, j: (0, 0)),
            pl.BlockSpec((d, dc), lambda i, j: (0, j)),
            pl.BlockSpec((d, dc), lambda i, j: (0, 4)),
        ],
        out_specs=[row, row, row, row],
        out_shape=[out, out, out, out],
        scratch_shapes=[pltpu.VMEM((tm, d), BF16)],
        compiler_params=_cparams(("arbitrary", "arbitrary")),
        name="in_projection",
    )(x, norm_g, w_in_bf, w_in_bf)


def _conv_kernel(glu_ref, st_ref, w_ref, b_ref, gg_ref, gb_ref, c_ref, xs_ref, *, tt, rc, nseq, carry, preshift):
    t = pl.program_id(1)
    off = HALO - (CONV_WIDTH - 1)
    span = HALO + tt - SUBLANES

    def lane_block(sq, cs):
        for r0 in range(0, tt, rc):
            acc = jnp.broadcast_to(b_ref[:, cs], (rc, LANES))
            for j in range(CONV_WIDTH):
                o = off + j
                if preshift:
                    lo = r0 + (o // SUBLANES) * SUBLANES
                    x = xs_ref[o % SUBLANES, lo:lo + rc, cs]
                else:
                    x = xs_ref[0, r0 + o:r0 + o + rc, cs]
                acc = acc + x * w_ref[j:j + 1, cs]
            mu = jnp.mean(acc, axis=-1, keepdims=True)
            dlt = acc - mu
            var = jnp.mean(dlt * dlt, axis=-1, keepdims=True)
            yn = dlt * lax.rsqrt(var + EPS)
            yn = yn * gg_ref[:, cs] + gb_ref[:, cs]
            c_ref[sq, r0:r0 + rc, cs] = (yn * (1.0 / (1.0 + jnp.exp(-yn)))).astype(c_ref.dtype)

    def one_seq(sq):
        if carry:
            @pl.when(t == 0)
            def _():
                xs_ref[0, 0:HALO, :] = st_ref[sq]
        else:
            xs_ref[0, 0:HALO, :] = st_ref[sq]
        xs_ref[0, HALO:HALO + tt, :] = glu_ref[sq]
        if preshift:
            for r in range(1, SUBLANES):
                xs_ref[r, 0:span, :] = xs_ref[0, r:r + span, :]

            def group(cb, carry_):
                lane_block(sq, pl.ds(pl.multiple_of(cb * LANES, LANES), LANES))
                return carry_

            lax.fori_loop(0, CONV_GROUPS, group, 0)
        else:
            for cb in range(CONV_GROUPS):
                lane_block(sq, pl.ds(cb * LANES, LANES))
        if carry:
            xs_ref[0, 0:HALO, :] = xs_ref[0, tt:tt + HALO, :]

    if nseq == 1:
        one_seq(0)
    else:
        def seq(sq, carry_):
            one_seq(sq)
            return carry_

        lax.fori_loop(0, nseq, seq, 0)


def _conformer_conv(glu, state_pad, w_dw, b_dw, gn_g, gn_b, tt, nseq, out_dtype):
    b, t, dc = glu.shape
    assert dc == CONV_GROUPS * LANES and b % nseq == 0 and t % tt == 0
    nt = t // tt
    assert nt == 1 or nseq == 1
    preshift = tt >= 4 * SUBLANES
    rc = min(tt, 64)
    vec = pl.BlockSpec((1, dc), lambda i, j: (0, 0))
    return pl.pallas_call(
        functools.partial(_conv_kernel, tt=tt, rc=rc, nseq=nseq, carry=nt > 1, preshift=preshift),
        grid=(b // nseq, nt),
        in_specs=[
            pl.BlockSpec((nseq, tt, dc), lambda i, j: (i, j, 0)),
            pl.BlockSpec((nseq, HALO, dc), lambda i, j: (i, 0, 0)),
            pl.BlockSpec((CONV_WIDTH, dc), lambda i, j: (0, 0)),
            vec, vec, vec,
        ],
        out_specs=pl.BlockSpec((nseq, tt, dc), lambda i, j: (i, j, 0)),
        out_shape=jax.ShapeDtypeStruct((b, t, dc), out_dtype),
        scratch_shapes=[pltpu.VMEM((SUBLANES if preshift else 1, HALO + tt, dc), F32)],
        compiler_params=_cparams(("arbitrary", "arbitrary")),
        name="conformer_conv",
    )(glu, state_pad, w_dw, b_dw, gn_g, gn_b)


def _moba_prompt_kernel(pt_ref, slopes_ref, q_ref, k_ref, v_ref, nrm_ref, ck_ref, o_ref, kmean_ref,
                        qb_ref, kb_ref, vt_ref, g_ref, biasp_ref, biasd_ref, pbuf, kmbuf, psem, osem,
                        *, nblk, qpt, kpi, n_steps, n_chunks, cps, pps, n_pages):
    h = pl.program_id(1)
    qi = pl.program_id(2)
    blk = MOBA_BLOCK
    qw = qpt * blk
    t = q_ref.shape[1]
    nb8 = -(-nblk // SUBLANES) * SUBLANES
    slope = slopes_ref[h]

    step = (pl.program_id(0) * N_HEADS + h) * (nblk // qpt) + qi
    c0 = step * cps
    exact = n_chunks == cps * n_steps
    ppb = blk // pbuf.shape[2]

    def start_chunk(chunk):
        slot = chunk % 2
        for i in range(pps):
            pltpu.make_async_copy(ck_ref.at[pt_ref[chunk * pps + i]], pbuf.at[slot, i], psem.at[slot]).start()

    def finish_chunk(chunk):
        slot = chunk % 2
        for i in range(pps):
            pltpu.make_async_copy(ck_ref.at[0], pbuf.at[slot, i], psem.at[slot]).wait()
        bi = (chunk * pps) // n_pages
        blk0 = ((chunk * pps) % n_pages) // ppb
        for i in range(pps // ppb):
            s = jnp.sum(pbuf[slot, ppb * i], axis=0)
            for p in range(1, ppb):
                s = s + jnp.sum(pbuf[slot, ppb * i + p], axis=0)
            kmbuf[bi, blk0 + i] = s * (1.0 / blk)

    def if_real(chunk, fn, always):
        if always:
            fn(chunk)
        else:
            pl.when(chunk < n_chunks)(functools.partial(fn, chunk))

    @pl.when(step == 0)
    def _():
        start_chunk(0)

    if_real(c0 + 1, start_chunk, exact and cps == 2)

    @pl.when(qi == 0)
    def _():
        q = q_ref[0]
        kf = k_ref[0]
        qb_ref[...] = (q * (HEAD_DIM ** -0.5)).astype(BF16)
        kb_ref[...] = kf.astype(BF16)
        vt_ref[...] = v_ref[0].T.astype(BF16)
        km = jnp.sum(kf.reshape(nblk, blk, HEAD_DIM), axis=1) * (1.0 / blk)
        if nb8 > nblk:
            km = jnp.concatenate([km, jnp.zeros((nb8 - nblk, HEAD_DIM), F32)], axis=0)
        gate = _nt_dot(km, q, precision=lax.Precision.HIGHEST)
        j_io = lax.broadcasted_iota(jnp.int32, gate.shape, 0)
        t_io = lax.broadcasted_iota(jnp.int32, gate.shape, 1)
        valid = j_io * blk + (blk - 1) < t_io - (t_io % blk)
        g = jnp.where(valid, gate, NEG_INF)
        sel = jnp.zeros(gate.shape, jnp.bool_)
        for _ in range(MOBA_TOPK):
            m = jnp.max(g, axis=0, keepdims=True)
            idx = jnp.min(jnp.where(g == m, j_io, nb8), axis=0, keepdims=True)
            pick = j_io == idx
            sel = jnp.logical_or(sel, pick)
            g = jnp.where(pick, -jnp.inf, g)
        gb = (slope * blk) * j_io.astype(F32) + jnp.where(jnp.logical_and(sel, valid), 0.0, NEG_INF)
        for j in range(nblk):
            g_ref[j] = jnp.broadcast_to(gb[j:j + 1, :], (SUBLANES, t))
        c = lax.broadcasted_iota(jnp.int32, (kpi * blk, qw), 0)
        biasp_ref[...] = slope * (c % blk).astype(F32)
        c = lax.broadcasted_iota(jnp.int32, (qw, qw), 0)
        r = lax.broadcasted_iota(jnp.int32, (qw, qw), 1)
        hidden = jnp.logical_or(c // blk > r // blk, jnp.logical_and(c // blk == r // blk, c % blk > r % blk))
        biasd_ref[...] = slope * (c % blk).astype(F32) + jnp.where(hidden, NEG_INF, 0.0)

    own = pl.ds(pl.multiple_of(qi * qw, qw), qw)
    qs = qb_ref[own, :]
    lane_blk = lax.broadcasted_iota(jnp.int32, (1, qw), 1) // blk

    def softmax_terms(st, gs, m_prev):
        segs = [st[i * blk:(i + 1) * blk] for i in range(len(gs))]
        m_new = m_prev
        for seg, g in zip(segs, gs):
            cand = jnp.max(seg, axis=0, keepdims=True) + g
            m_new = cand if m_new is None else jnp.maximum(m_new, cand)
        p = jnp.concatenate([jnp.exp(seg - (m_new - g)) for seg, g in zip(segs, gs)], axis=0)
        return m_new, p

    st = _nt_dot(kb_ref[own, :], qs) + biasd_ref[...]
    gs = []
    for i in range(qpt):
        g_blk = (slope * blk) * (qi * qpt + i).astype(F32)
        gs.append(jnp.where(lane_blk <= i, g_blk, g_ref[qi * qpt + i, 0:1, own]))
    m0, p = softmax_terms(st, gs, None)
    l0 = jnp.sum(p, axis=0, keepdims=True)
    acc0 = jnp.dot(vt_ref[:, own], p.astype(BF16), preferred_element_type=F32)

    n_past = qi * qpt

    def past_blocks(jp, carry):
        m, l, acc = carry
        rows = pl.ds(pl.multiple_of(jp * (kpi * blk), kpi * blk), kpi * blk)
        st = _nt_dot(kb_ref[rows, :], qs) + biasp_ref[...]
        gs = [jnp.where(jp * kpi + i < n_past, g_ref[jp * kpi + i, 0:1, own], NEG_INF) for i in range(kpi)]
        m_new, p = softmax_terms(st, gs, m)
        alpha = jnp.exp(m - m_new)
        l = alpha * l + jnp.sum(p, axis=0, keepdims=True)
        acc = alpha * acc + jnp.dot(vt_ref[:, rows], p.astype(BF16), preferred_element_type=F32)
        return m_new, l, acc

    _, l, acc = lax.fori_loop(0, (n_past + kpi - 1) // kpi, past_blocks, (m0, l0, acc0))
    ot = acc * (1.0 / l)
    ot = ot * lax.rsqrt(jnp.mean(ot * ot, axis=0, keepdims=True) + EPS)
    o_ref[0] = (ot.T * nrm_ref[...]).astype(o_ref.dtype)

    if_real(c0, finish_chunk, exact)
    if cps == 2:
        if_real(c0 + 2, start_chunk, False)
        if_real(c0 + 1, finish_chunk, exact)

    @pl.when(step == n_steps - 1)
    def _():
        out = pltpu.make_async_copy(kmbuf, kmean_ref, osem.at[0])
        out.start()
        out.wait()


def _moba_prompt(q, k, v, slopes, attn_norm, page_table, cache_k):
    b, t, _ = q.shape
    blk = MOBA_BLOCK
    nblk = t // blk
    assert t % blk == 0
    qpt = 2 if nblk % 2 == 0 else 1
    kpi = max(kk for kk in (4, 2, 1) if nblk % kk == 0)
    db, n_pages = page_table.shape
    _, page, nh, hd = cache_k.shape
    ppb = blk // page
    n_steps = b * N_HEADS * (nblk // qpt)
    pps = max(p for p in range(ppb, STREAM_PAGES + 1, ppb) if n_pages % p == 0)
    n_chunks = db * n_pages // pps
    cps = -(-n_chunks // n_steps)
    assert cps <= 2
    qw = qpt * blk
    seq = pl.BlockSpec((1, t, HEAD_DIM), lambda bi, h, qi, pt: (bi, 0, h))
    tile = pl.BlockSpec((1, qw, HEAD_DIM), lambda bi, h, qi, pt: (bi, qi, h))
    return pl.pallas_call(
        functools.partial(_moba_prompt_kernel, nblk=nblk, qpt=qpt, kpi=kpi, n_steps=n_steps, n_chunks=n_chunks,
                          cps=cps, pps=pps, n_pages=n_pages),
        grid_spec=pltpu.PrefetchScalarGridSpec(
            num_scalar_prefetch=1,
            grid=(b, N_HEADS, nblk // qpt),
            in_specs=[
                pl.BlockSpec(memory_space=pltpu.SMEM),
                seq, seq, seq,
                pl.BlockSpec((1, HEAD_DIM), lambda bi, h, qi, pt: (0, h)),
                pl.BlockSpec(memory_space=pl.ANY),
            ],
            out_specs=[tile, pl.BlockSpec(memory_space=pl.ANY)],
            scratch_shapes=[
                pltpu.VMEM((t, HEAD_DIM), BF16),
                pltpu.VMEM((t, HEAD_DIM), BF16),
                pltpu.VMEM((HEAD_DIM, t), BF16),
                pltpu.VMEM((nblk, SUBLANES, t), F32),
                pltpu.VMEM((kpi * blk, qw), F32),
                pltpu.VMEM((qw, qw), F32),
                pltpu.VMEM((2, pps, page, nh, hd), F32),
                pltpu.VMEM((db, n_pages // ppb, nh, hd), F32),
                pltpu.SemaphoreType.DMA((2,)),
                pltpu.SemaphoreType.DMA((1,)),
            ],
        ),
        out_shape=[jax.ShapeDtypeStruct((b, t, D_ATTN), BF16),
                   jax.ShapeDtypeStruct((db, n_pages // ppb, nh, hd), F32)],
        compiler_params=_cparams(("arbitrary", "arbitrary", "arbitrary")),
        name="moba_prompt",
    )(page_table.reshape(-1), slopes, q, k, v, attn_norm, cache_k)


def _sample_topk_kernel(q_ref, km_ref, idx_ref, *, nbf):
    for h in range(N_HEADS):
        gate = _nt_dot(q_ref[0, h], km_ref[0, h], precision=lax.Precision.HIGHEST)
        lane = lax.broadcasted_iota(jnp.int32, gate.shape, 1)
        g = jnp.where(lane < nbf, gate, -jnp.inf)
        out = jnp.zeros(gate.shape, jnp.int32)
        for r in range(MOBA_TOPK):
            m = jnp.max(g, axis=-1, keepdims=True)
            idx = jnp.min(jnp.where(g == m, lane, LANES), axis=-1, keepdims=True)
            out = jnp.where(lane == r, idx, out)
            g = jnp.where(lane == idx, -jnp.inf, g)
        idx_ref[0, h] = out


def _sample_topk(q_t, kmean_t, nbf):
    db, nh, s, hd = q_t.shape
    return pl.pallas_call(
        functools.partial(_sample_topk_kernel, nbf=nbf),
        grid=(db,),
        in_specs=[
            pl.BlockSpec((1, nh, s, hd), lambda i: (i, 0, 0, 0)),
            pl.BlockSpec((1, nh, LANES, hd), lambda i: (i, 0, 0, 0)),
        ],
        out_specs=pl.BlockSpec((1, nh, s, LANES), lambda i: (i, 0, 0, 0)),
        out_shape=jax.ShapeDtypeStruct((db, nh, s, LANES), jnp.int32),
        compiler_params=_cparams(("arbitrary",)),
        name="sample_topk",
    )(q_t, kmean_t)


def _moba_sample_kernel(pt_ref, blk_ref, slopes_ref, q_ref, kn_ref, vn_ref, nrm_ref, ck_ref, cv_ref, o_ref,
                        kbuf, vbuf, sem, *, n_units, n_sel, ppb, past):
    page = ck_ref.shape[1]
    n_slab = n_sel * ppb
    n_keys = n_slab * page
    n_pages = past // page
    s_len = q_ref.shape[2]
    rows = 16

    def copies(u, slot, i, h, pg):
        dst = pl.ds(pl.multiple_of(i * page, page), page)
        return (pltpu.make_async_copy(ck_ref.at[pg, :, h, :], kbuf.at[slot, dst], sem.at[0, slot]),
                pltpu.make_async_copy(cv_ref.at[pg, :, h, :], vbuf.at[slot, dst], sem.at[1, slot]))

    def start(u, slot):
        h = u % N_HEADS
        row0 = (u // N_HEADS) * n_pages

        def sel(n, carry):
            logical = blk_ref[u * n_sel + n] * ppb
            for p in range(ppb):
                ck, cv = copies(u, slot, n * ppb + p, h, pt_ref[row0 + logical + p])
                ck.start()
                cv.start()
            return carry

        lax.fori_loop(0, n_sel, sel, 0)

    u = pl.program_id(0)

    @pl.when(u == 0)
    def _():
        start(0, 0)

    slot = u % 2
    bi = u // N_HEADS
    h = u % N_HEADS

    @pl.when(u + 1 < n_units)
    def _():
        start(u + 1, 1 - slot)

    def slab_done(i, carry):
        ck, cv = copies(u, slot, i, 0, 0)
        ck.wait()
        cv.wait()
        return carry

    lax.fori_loop(0, n_slab, slab_done, 0)

    slope = slopes_ref[h]
    q = q_ref[bi, h]
    qs = jnp.concatenate([q * (HEAD_DIM ** -0.5), jnp.zeros((rows - s_len, HEAD_DIM), F32)], axis=0)
    qs = qs.astype(BF16)
    kb = kbuf[slot].astype(BF16)
    vb = vbuf[slot].astype(BF16)
    s = _nt_dot(qs, kb)

    per_blk = page * ppb
    offs = lax.broadcasted_iota(jnp.int32, (1, per_blk), 1)
    kpos = jnp.concatenate(
        [(blk_ref[u * n_sel + i] * per_blk + offs).astype(F32) for i in range(n_sel)], axis=1)
    row = lax.broadcasted_iota(jnp.int32, (rows, 1), 0)
    qpos = (past + row).astype(F32)
    s = s - slope * (qpos - kpos)
    col = lax.broadcasted_iota(jnp.int32, (rows, n_keys), 1)
    per_q = MOBA_TOPK * per_blk
    mine = jnp.logical_and(col >= row * per_q, col < (row + 1) * per_q)
    s = jnp.where(mine, s, NEG_INF)

    kn = jnp.concatenate([kn_ref[bi, h], jnp.zeros((rows - s_len, HEAD_DIM), F32)], axis=0).astype(BF16)
    vn = jnp.concatenate([vn_ref[bi, h], jnp.zeros((rows - s_len, HEAD_DIM), F32)], axis=0).astype(BF16)
    ri = lax.broadcasted_iota(jnp.int32, (rows, rows), 0)
    ci = lax.broadcasted_iota(jnp.int32, (rows, rows), 1)
    so = _nt_dot(qs, kn) - slope * (ri - ci).astype(F32)
    so = jnp.where(jnp.logical_and(ri >= ci, ci < s_len), so, NEG_INF)

    m = jnp.maximum(jnp.max(s, axis=-1, keepdims=True), jnp.max(so, axis=-1, keepdims=True))
    p = jnp.exp(s - m)
    po = jnp.exp(so - m)
    l = jnp.sum(p, axis=-1, keepdims=True) + jnp.sum(po, axis=-1, keepdims=True)
    acc = (jnp.dot(p.astype(BF16), vb, preferred_element_type=F32)
           + jnp.dot(po.astype(BF16), vn, preferred_element_type=F32))
    o = (acc / l)[0:s_len]
    o = (o * lax.rsqrt(jnp.mean(o * o, axis=-1, keepdims=True) + EPS)) * nrm_ref[pl.ds(h, 1), :]
    o_ref[bi, h] = o


def _moba_sample(page_table_flat, blks, slopes, q_t, kn_t, vn_t, attn_norm_hd, cache_k, cache_v, past):
    db, nh, s, hd = q_t.shape
    _, page, _, _ = cache_k.shape
    ppb = MOBA_BLOCK // page
    n_sel = s * MOBA_TOPK
    whole = lambda shp: pl.BlockSpec(shp, lambda i, *_: (0,) * len(shp))
    return pl.pallas_call(
        functools.partial(_moba_sample_kernel, n_units=db * nh, n_sel=n_sel, ppb=ppb, past=past),
        grid_spec=pltpu.PrefetchScalarGridSpec(
            num_scalar_prefetch=3,
            grid=(db * nh,),
            in_specs=[
                whole(q_t.shape), whole(kn_t.shape), whole(vn_t.shape), whole(attn_norm_hd.shape),
                pl.BlockSpec(memory_space=pl.ANY),
                pl.BlockSpec(memory_space=pl.ANY),
            ],
            out_specs=whole(q_t.shape),
            scratch_shapes=[
                pltpu.VMEM((2, n_sel * ppb * page, hd), F32),
                pltpu.VMEM((2, n_sel * ppb * page, hd), F32),
                pltpu.SemaphoreType.DMA((2, 2)),
            ],
        ),
        out_shape=jax.ShapeDtypeStruct(q_t.shape, F32),
        compiler_params=_cparams(("arbitrary",)),
        name="moba_sample",
    )(page_table_flat, blks, slopes, q_t, kn_t, vn_t, attn_norm_hd, cache_k, cache_v)


def _outproj_kernel(op_ref, cp_ref, xp_ref, os_ref, cs_ref, xs_ref, wo_ref, wc_ref, g_ref, wr_ref, br_ref,
                    x2_ref, h2_ref, route_ref, *, nb_p):
    i = pl.program_id(0)
    tail = (wo_ref, wc_ref, g_ref, wr_ref, br_ref, x2_ref, h2_ref, route_ref)

    @pl.when(i < nb_p)
    def _():
        _outproj_tile(op_ref, cp_ref, xp_ref, *tail)

    @pl.when(i >= nb_p)
    def _():
        _outproj_tile(os_ref, cs_ref, xs_ref, *tail)


def _outproj_tile(o_ref, c_ref, x_ref, wo_ref, wc_ref, g_ref, wr_ref, br_ref, x2_ref, h2_ref, route_ref):
    mix = (jnp.dot(o_ref[...].astype(BF16), wo_ref[...], preferred_element_type=F32)
           + jnp.dot(c_ref[...].astype(BF16), wc_ref[...], preferred_element_type=F32))
    x2 = x_ref[...] + mix
    x2_ref[...] = x2
    r = lax.rsqrt(jnp.mean(x2 * x2, axis=-1, keepdims=True) + EPS)
    h2 = (x2 * r) * g_ref[...]
    h2_ref[...] = h2

    h_hi = h2.astype(BF16)
    h_lo = (h2 - h_hi.astype(F32)).astype(BF16)
    r1 = jnp.dot(h_hi, wr_ref[...], preferred_element_type=F32)
    r2 = jnp.dot(h_lo, wr_ref[:, 0:LANES], preferred_element_type=F32)
    logits = r1[:, 0:LANES] + r1[:, LANES:2 * LANES] + r2 + br_ref[...]

    lane = lax.broadcasted_iota(jnp.int32, logits.shape, 1)
    is_g = lane < N_EXPERT_GROUPS
    gl = jnp.where(is_g, logits, -jnp.inf)
    gmax = jnp.max(gl, axis=-1, keepdims=True)
    gidx = jnp.min(jnp.where(gl == gmax, lane, LANES), axis=-1, keepdims=True)
    gsum = jnp.sum(jnp.where(is_g, jnp.exp(gl - gmax), 0.0), axis=-1, keepdims=True)
    g_gate = 1.0 / gsum

    lo = N_EXPERT_GROUPS + gidx * EXPERTS_PER_GROUP
    el = jnp.where(jnp.logical_and(lane >= lo, lane < lo + EXPERTS_PER_GROUP), logits, -jnp.inf)
    v0 = jnp.max(el, axis=-1, keepdims=True)
    i0 = jnp.min(jnp.where(el == v0, lane, LANES), axis=-1, keepdims=True)
    el = jnp.where(lane == i0, -jnp.inf, el)
    v1 = jnp.max(el, axis=-1, keepdims=True)
    i1 = jnp.min(jnp.where(el == v1, lane, LANES), axis=-1, keepdims=True)
    e = jnp.exp(v1 - v0)
    w0 = g_gate / (1.0 + e)
    w1 = g_gate * e / (1.0 + e)

    out = jnp.where(lane == 0, (i0 - N_EXPERT_GROUPS).astype(F32), 0.0)
    out = jnp.where(lane == 1, (i1 - N_EXPERT_GROUPS).astype(F32), out)
    out = jnp.where(lane == 2, w0, out)
    out = jnp.where(lane == 3, w1, out)
    route_ref[...] = out


def _out_projection(o_p, c_p, x_p, o_s, c_s, x_s, wo, wc, norm_g, wr, br, tm):
    np_, d = x_p.shape
    ns_ = x_s.shape[0]
    assert np_ % tm == 0 and ns_ % tm == 0
    nb_p, nb_s = np_ // tm, ns_ // tm
    n = np_ + ns_
    prow = lambda w: pl.BlockSpec((tm, w), lambda i: (jnp.minimum(i, nb_p - 1), 0))
    srow = lambda w: pl.BlockSpec((tm, w), lambda i: (jnp.maximum(i - nb_p, 0), 0))
    row = lambda w: pl.BlockSpec((tm, w), lambda i: (i, 0))
    full = lambda a: pl.BlockSpec(a.shape, lambda i: (0, 0))
    return pl.pallas_call(
        functools.partial(_outproj_kernel, nb_p=nb_p),
        grid=(nb_p + nb_s,),
        in_specs=[prow(o_p.shape[1]), prow(c_p.shape[1]), prow(d), srow(o_s.shape[1]), srow(c_s.shape[1]), srow(d),
                  full(wo), full(wc), full(norm_g), full(wr), full(br)],
        out_specs=[row(d), row(d), row(LANES)],
        out_shape=[jax.ShapeDtypeStruct((n, d), F32), jax.ShapeDtypeStruct((n, d), F32),
                   jax.ShapeDtypeStruct((n, LANES), F32)],
        compiler_params=_cparams(("arbitrary",)),
        name="out_projection_router",
    )(o_p, c_p, x_p, o_s, c_s, x_s, wo, wc, norm_g, wr, br)


DMA_UNROLL = 8


def _for_rows(n, fn):
    def group(gi, carry):
        for k in range(DMA_UNROLL):
            fn(gi * DMA_UNROLL + k)
        return carry

    def single(i, carry):
        fn(i)
        return carry

    n_grp = lax.div(n, DMA_UNROLL)
    lax.fori_loop(0, n_grp, group, 0)
    lax.fori_loop(n_grp * DMA_UNROLL, n, single, 0)


def _moe_kernel(order_ref, cexp_ref, cbase_ref, cnv_ref, h_ref, wg_ref, wu_ref, wd_ref, out_ref,
                xbuf, obuf, wgb, wub, wdb, gsem, ssem, *, n_tok, n_chunks):
    g = pl.program_id(0)
    slot = g % 2

    def gather(chunk, buf_slot):
        base = cbase_ref[chunk]

        def desc(i):
            tok = lax.shift_right_logical(order_ref[base + i], 1)
            return pltpu.make_async_copy(h_ref.at[pl.ds(tok, 1)], xbuf.at[buf_slot, pl.ds(i, 1)], gsem.at[buf_slot])
        return desc

    def scatter(chunk, buf_slot):
        base = cbase_ref[chunk]

        def desc(i):
            pair = order_ref[base + i]
            row = (pair & 1) * n_tok + lax.shift_right_logical(pair, 1)
            return pltpu.make_async_copy(obuf.at[buf_slot, pl.ds(i, 1)], out_ref.at[pl.ds(row, 1)], ssem.at[buf_slot])
        return desc

    def gather_done(buf_slot):
        pltpu.make_async_copy(h_ref.at[pl.ds(0, 1)], xbuf.at[buf_slot, pl.ds(0, 1)], gsem.at[buf_slot]).wait()

    def scatter_done(buf_slot):
        pltpu.make_async_copy(obuf.at[buf_slot, pl.ds(0, 1)], out_ref.at[pl.ds(0, 1)], ssem.at[buf_slot]).wait()

    @pl.when(g == 0)
    def _():
        xbuf[...] = jnp.zeros(xbuf.shape, xbuf.dtype)
        _for_rows(cnv_ref[0], lambda i: gather(0, 0)(i).start())

    @pl.when(jnp.logical_or(g == 0, cexp_ref[g] != cexp_ref[jnp.maximum(g - 1, 0)]))
    def _():
        wgb[...] = wg_ref[0].astype(BF16)
        wub[...] = wu_ref[0].astype(BF16)
        wdb[...] = wd_ref[0].astype(BF16)

    @pl.when(g + 1 < n_chunks)
    def _():
        _for_rows(cnv_ref[g + 1], lambda i: gather(g + 1, 1 - slot)(i).start())

    nv = cnv_ref[g]
    _for_rows(nv, lambda i: gather_done(slot))

    @pl.when(g >= 2)
    def _():
        _for_rows(cnv_ref[g - 2], lambda i: scatter_done(slot))

    @pl.when(nv > 0)
    def _():
        x = xbuf[slot].astype(BF16)
        gt = jnp.dot(x, wgb[...], preferred_element_type=F32)
        up = jnp.dot(x, wub[...], preferred_element_type=F32)
        hid = (gt * (1.0 / (1.0 + jnp.exp(-gt)))) * up
        obuf[slot] = jnp.dot(hid.astype(BF16), wdb[...], preferred_element_type=F32)

    _for_rows(nv, lambda i: scatter(g, slot)(i).start())

    @pl.when(g == n_chunks - 1)
    def _():
        if n_chunks >= 2:
            _for_rows(cnv_ref[g - 1], lambda i: scatter_done(1 - slot))
        _for_rows(nv, lambda i: scatter_done(slot))


def _moe(order, cexp, cbase, cnv, h2, w_gate, w_up, w_down, ch):
    n, d = h2.shape
    ne, _, de = w_gate.shape
    n_chunks = cexp.shape[0]
    wspec = lambda shp: pl.BlockSpec((1,) + shp, lambda g, order, cexp, cbase, cnv: (cexp[g], 0, 0))
    return pl.pallas_call(
        functools.partial(_moe_kernel, n_tok=n, n_chunks=n_chunks),
        grid_spec=pltpu.PrefetchScalarGridSpec(
            num_scalar_prefetch=4,
            grid=(n_chunks,),
            in_specs=[
                pl.BlockSpec(memory_space=pl.ANY),
                wspec((d, de)), wspec((d, de)), wspec((de, d)),
            ],
            out_specs=pl.BlockSpec(memory_space=pl.ANY),
            scratch_shapes=[
                pltpu.VMEM((2, ch, d), F32),
                pltpu.VMEM((2, ch, d), F32),
                pltpu.VMEM((d, de), BF16),
                pltpu.VMEM((d, de), BF16),
                pltpu.VMEM((de, d), BF16),
                pltpu.SemaphoreType.DMA((2,)),
                pltpu.SemaphoreType.DMA((2,)),
            ],
        ),
        out_shape=jax.ShapeDtypeStruct((2 * n, d), F32),
        compiler_params=_cparams(("arbitrary",)),
        name="expert_mlp",
    )(order, cexp, cbase, cnv, h2, w_gate, w_up, w_down)


def _expert_chunks(eid, ch):
    n_pairs = eid.shape[0]
    n_chunks = N_EXPERTS + n_pairs // ch
    order = jnp.argsort(eid, stable=True).astype(jnp.int32)
    counts = jnp.sum(eid[:, None] == jnp.arange(N_EXPERTS, dtype=jnp.int32)[None, :], axis=0, dtype=jnp.int32)
    offs = jnp.cumsum(counts, dtype=jnp.int32) - counts
    nch = (counts + ch - 1) // ch
    cum = jnp.cumsum(nch, dtype=jnp.int32)
    gidx = jnp.arange(n_chunks, dtype=jnp.int32)
    cexp_raw = jnp.sum(gidx[:, None] >= cum[None, :], axis=1, dtype=jnp.int32)
    last = jnp.maximum(jnp.max(jnp.where(nch > 0, jnp.arange(N_EXPERTS, dtype=jnp.int32), 0)), 0)
    used = gidx < cum[-1]
    cexp = jnp.where(used, jnp.minimum(cexp_raw, N_EXPERTS - 1), last)
    local = gidx - (cum[cexp] - nch[cexp])
    cbase = jnp.where(used, offs[cexp] + local * ch, 0)
    cnv = jnp.where(used, jnp.clip(counts[cexp] - local * ch, 0, ch), 0)
    return order, cexp.astype(jnp.int32), cbase.astype(jnp.int32), cnv.astype(jnp.int32)


def _final_kernel(x2_ref, ya_ref, yb_ref, route_ref, g_ref, yp_ref, ys_ref, *, nb_p):
    i = pl.program_id(0)
    rt = route_ref[...]
    x3 = x2_ref[...] + rt[:, 2:3] * ya_ref[...] + rt[:, 3:4] * yb_ref[...]
    r = lax.rsqrt(jnp.mean(x3 * x3, axis=-1, keepdims=True) + EPS)
    y = (x3 * r) * g_ref[...]

    @pl.when(i < nb_p)
    def _():
        yp_ref[...] = y

    @pl.when(i >= nb_p)
    def _():
        ys_ref[...] = y


def _final(x2, y2, route, norm_g, np_, tm):
    n, d = x2.shape
    assert np_ % tm == 0 and n % tm == 0
    nb, nb_p = n // tm, np_ // tm
    return pl.pallas_call(
        functools.partial(_final_kernel, nb_p=nb_p),
        grid=(nb,),
        in_specs=[
            pl.BlockSpec((tm, d), lambda i: (i, 0)),
            pl.BlockSpec((tm, d), lambda i: (i, 0)),
            pl.BlockSpec((tm, d), lambda i: (i + nb, 0)),
            pl.BlockSpec((tm, LANES), lambda i: (i, 0)),
            pl.BlockSpec((1, d), lambda i: (0, 0)),
        ],
        out_specs=[
            pl.BlockSpec((tm, d), lambda i: (jnp.minimum(i, nb_p - 1), 0)),
            pl.BlockSpec((tm, d), lambda i: (jnp.maximum(i - nb_p, 0), 0)),
        ],
        out_shape=[jax.ShapeDtypeStruct((np_, d), F32), jax.ShapeDtypeStruct((n - np_, d), F32)],
        compiler_params=_cparams(("arbitrary",)),
        name="combine_final_norm",
    )(x2, y2, y2, route, norm_g)


def _pick_tile(n, pref):
    t = min(n, pref)
    while n % t:
        t //= 2
    return t


def kernel(x_prompt, x_sample, cache_k, cache_v, state_conv, page_table, norm_mix, w_in, w_dw, b_dw,
           conv_norm_g, conv_norm_b, attn_out_norm, w_out, norm_ffn, w_router_group, b_router_group,
           w_router_expert, b_router_expert, w_gate, w_up, w_down, norm_final):
    depth = norm_mix.shape[0]
    assert depth == 1
    b, t, d = x_prompt.shape
    db, s, _ = x_sample.shape
    n_pages = page_table.shape[1]
    page = cache_k.shape[2]
    past = n_pages * page
    nbf = past // MOBA_BLOCK
    assert past % MOBA_BLOCK == 0 and nbf >= MOBA_TOPK and nbf <= LANES
    d_conv = d - D_ATTN
    assert d_conv == D_ATTN
    np_, ns_ = b * t, db * s

    slopes = jnp.exp2(-8.0 * jnp.arange(1, N_HEADS + 1, dtype=F32) / N_HEADS)
    w_in_bf = w_in[0].astype(BF16)
    wo_bf = w_out[0, :D_ATTN].astype(BF16)
    wc_bf = w_out[0, D_ATTN:].astype(BF16)
    wr = jnp.zeros((d, LANES), F32)
    wr = wr.at[:, :N_EXPERT_GROUPS].set(w_router_group[0])
    wr = wr.at[:, N_EXPERT_GROUPS:N_EXPERT_GROUPS + N_EXPERTS].set(w_router_expert[0])
    wr_hi = wr.astype(BF16)
    wr_lo = (wr - wr_hi.astype(F32)).astype(BF16)
    wr_cat = jnp.concatenate([wr_hi, wr_lo], axis=1)
    br = jnp.zeros((1, LANES), F32)
    br = br.at[0, :N_EXPERT_GROUPS].set(b_router_group[0])
    br = br.at[0, N_EXPERT_GROUPS:N_EXPERT_GROUPS + N_EXPERTS].set(b_router_expert[0])
    g_mix = norm_mix[0].reshape(1, d)
    g_ffn = norm_ffn[0].reshape(1, d)
    g_fin = norm_final.reshape(1, d)
    attn_norm = attn_out_norm[0].reshape(1, D_ATTN)
    conv_vec = lambda a: a[0].reshape(1, d_conv)

    xp = x_prompt.reshape(np_, d)
    xs = x_sample.reshape(ns_, d)
    qp, kp, vp, glu_p = _in_projection(xp, g_mix, w_in_bf, _pick_tile(np_, 512))
    qs, ks, vs, glu_s = _in_projection(xs, g_mix, w_in_bf, _pick_tile(ns_, 256))

    glu_p3 = glu_p.reshape(b, t, d_conv)
    glu_s3 = glu_s.reshape(db, s, d_conv)
    pad = HALO - (CONV_WIDTH - 1)
    st_p = jnp.zeros((b, HALO, d_conv), F32)
    st_s = jnp.pad(state_conv[0], ((0, 0), (pad, 0), (0, 0)))
    conv_args = (w_dw[0], conv_vec(b_dw), conv_vec(conv_norm_g), conv_vec(conv_norm_b))
    c_p = _conformer_conv(glu_p3, st_p, *conv_args, tt=_pick_tile(t, 256), nseq=1, out_dtype=BF16)
    c_s = _conformer_conv(glu_s3, st_s, *conv_args, tt=s, nseq=db, out_dtype=F32)
    conv_prompt = glu_p3[:, t - (CONV_WIDTH - 1):][None]
    conv_sample = jnp.concatenate([state_conv[0], glu_s3], axis=1)[:, -(CONV_WIDTH - 1):][None]

    ck = cache_k[0]
    cv = cache_v[0]
    o_p, kmean = _moba_prompt(qp.reshape(b, t, D_ATTN), kp.reshape(b, t, D_ATTN), vp.reshape(b, t, D_ATTN),
                              slopes, attn_norm, page_table, ck)

    kmean_t = jnp.pad(kmean.transpose(0, 2, 1, 3), ((0, 0), (0, 0), (0, LANES - nbf), (0, 0)))
    to_heads = lambda a: a.reshape(db, s, N_HEADS, HEAD_DIM).transpose(0, 2, 1, 3)
    q_t, kn_t, vn_t = to_heads(qs), to_heads(ks), to_heads(vs)
    idx = _sample_topk(q_t, kmean_t, nbf)[..., :MOBA_TOPK]
    o_t = _moba_sample(page_table.reshape(-1), idx.reshape(-1), slopes, q_t, kn_t, vn_t,
                       attn_norm.reshape(N_HEADS, HEAD_DIM), ck, cv, past)
    o_s = o_t.transpose(0, 2, 1, 3).reshape(ns_, D_ATTN)

    tm = _pick_tile(ns_, 256)
    x2, h2, route = _out_projection(o_p.reshape(np_, D_ATTN), c_p.reshape(np_, d_conv), xp,
                                    o_s, c_s.reshape(ns_, d_conv), xs,
                                    wo_bf, wc_bf, g_ffn, wr_cat, br, tm)

    ch = 256
    eid = route[:, 0:2].astype(jnp.int32).reshape(-1)
    order, cexp, cbase, cnv = _expert_chunks(eid, ch)
    y2 = _moe(order, cexp, cbase, cnv, h2, w_gate[0], w_up[0], w_down[0], ch)

    y_p, y_s = _final(x2, y2, route, g_fin, np_, tm)

    page_shape = (depth, b, t // page, page, N_HEADS, HEAD_DIM)
    return (y_p.reshape(b, t, d), y_s.reshape(db, s, d),
            kp.reshape(page_shape), vp.reshape(page_shape), conv_prompt,
            ks.reshape(depth, db, s, N_HEADS, HEAD_DIM), vs.reshape(depth, db, s, N_HEADS, HEAD_DIM),
            conv_sample)
```

```python
import functools

import jax
import jax.numpy as jnp
from jax import lax
from jax.experimental import pallas as pl
from jax.experimental.pallas import tpu as pltpu

F32 = jnp.float32
BF16 = jnp.bfloat16

N_HEADS = 8
HEAD_DIM = 128
D_ATTN = N_HEADS * HEAD_DIM
CONV_GROUPS = 8
CONV_WIDTH = 31
MOBA_BLOCK = 256
MOBA_TOPK = 3
N_EXPERT_GROUPS = 4
EXPERTS_PER_GROUP = 8
N_EXPERTS = N_EXPERT_GROUPS * EXPERTS_PER_GROUP
EPS = 1e-6
NEG_INF = -1e30

LANES = 128
SUBLANES = 8
HALO = 32
VMEM_LIMIT = 56 * 1024 * 1024
STREAM_PAGES = 16


def _cparams(sem=None):
    return pltpu.CompilerParams(dimension_semantics=sem, vmem_limit_bytes=VMEM_LIMIT)


def _nt_dot(a, b, precision=None):
    return lax.dot_general(a, b, (((1,), (1,)), ((), ())), precision=precision,
                           preferred_element_type=F32)


def _inproj_kernel(x_ref, g_ref, w_ref, wgate_ref, q_ref, k_ref, v_ref, glu_ref, h_ref):
    j = pl.program_id(1)

    @pl.when(j == 0)
    def _():
        x = x_ref[...]
        r = lax.rsqrt(jnp.mean(x * x, axis=-1, keepdims=True) + EPS)
        h_ref[...] = ((x * r) * g_ref[...]).astype(BF16)

    z = jnp.dot(h_ref[...], w_ref[...], preferred_element_type=F32)

    @pl.when(j == 0)
    def _():
        q_ref[...] = z

    @pl.when(j == 1)
    def _():
        k_ref[...] = z

    @pl.when(j == 2)
    def _():
        v_ref[...] = z

    @pl.when(j == 3)
    def _():
        gate = jnp.dot(h_ref[...], wgate_ref[...], preferred_element_type=F32)
        glu_ref[...] = z * (1.0 / (1.0 + jnp.exp(-gate)))


def _in_projection(x, norm_g, w_in_bf, tm):
    n, d = x.shape
    dc = D_ATTN
    out = jax.ShapeDtypeStruct((n, dc), F32)
    row = pl.BlockSpec((tm, dc), lambda i, j: (i, 0))
    return pl.pallas_call(
        _inproj_kernel,
        grid=(n // tm, 4),
        in_specs=[
            pl.BlockSpec((tm, d), lambda i, j: (i, 0)),
            pl.BlockSpec((1, d), lambda i, j: (0, 0)),
            pl.BlockSpec((d, dc), lambda i, j: (0, j)),
            pl.BlockSpec((d, dc), lambda i, j: (0, 4)),
        ],
        out_specs=[row, row, row, row],
        out_shape=[out, out, out, out],
        scratch_shapes=[pltpu.VMEM((tm, d), BF16)],
        compiler_params=_cparams(("arbitrary", "arbitrary")),
        name="in_projection",
    )(x, norm_g, w_in_bf, w_in_bf)


def _conv_lane_block(xs_ref, w_ref, b_ref, gg_ref, gb_ref, c_ref, sq, cs, *, tt, rc, preshift):
    off = HALO - (CONV_WIDTH - 1)
    for r0 in range(0, tt, rc):
        acc = jnp.broadcast_to(b_ref[:, cs], (rc, LANES))
        for j in range(CONV_WIDTH):
            o = off + j
            if preshift:
                lo = r0 + (o // SUBLANES) * SUBLANES
                x = xs_ref[o % SUBLANES, lo:lo + rc, cs]
            else:
                x = xs_ref[0, r0 + o:r0 + o + rc, cs]
            acc = acc + x * w_ref[j:j + 1, cs]
        mu = jnp.mean(acc, axis=-1, keepdims=True)
        dlt = acc - mu
        var = jnp.mean(dlt * dlt, axis=-1, keepdims=True)
        yn = dlt * lax.rsqrt(var + EPS)
        yn = yn * gg_ref[:, cs] + gb_ref[:, cs]
        c_ref[sq, r0:r0 + rc, cs] = (yn * (1.0 / (1.0 + jnp.exp(-yn)))).astype(c_ref.dtype)


def _conv_preshift(xs_ref, tt):
    span = HALO + tt - SUBLANES
    for r in range(1, SUBLANES):
        xs_ref[r, 0:span, :] = xs_ref[0, r:r + span, :]


def _conv_kernel(glu_ref, st_ref, w_ref, b_ref, gg_ref, gb_ref, c_ref, xs_ref, *, tt, rc, nseq, carry, preshift):
    t = pl.program_id(1)

    def lane_block(sq, cs):
        _conv_lane_block(xs_ref, w_ref, b_ref, gg_ref, gb_ref, c_ref, sq, cs, tt=tt, rc=rc, preshift=preshift)

    def one_seq(sq):
        if carry:
            @pl.when(t == 0)
            def _():
                xs_ref[0, 0:HALO, :] = st_ref[sq]
        else:
            xs_ref[0, 0:HALO, :] = st_ref[sq]
        xs_ref[0, HALO:HALO + tt, :] = glu_ref[sq]
        if preshift:
            _conv_preshift(xs_ref, tt)

            def group(cb, carry_):
                lane_block(sq, pl.ds(pl.multiple_of(cb * LANES, LANES), LANES))
                return carry_

            lax.fori_loop(0, CONV_GROUPS, group, 0)
        else:
            for cb in range(CONV_GROUPS):
                lane_block(sq, pl.ds(cb * LANES, LANES))
        if carry:
            xs_ref[0, 0:HALO, :] = xs_ref[0, tt:tt + HALO, :]

    if nseq == 1:
        one_seq(0)
    else:
        def seq(sq, carry_):
            one_seq(sq)
            return carry_

        lax.fori_loop(0, nseq, seq, 0)


def _conformer_conv(glu, state_pad, w_dw, b_dw, gn_g, gn_b, tt, nseq, out_dtype):
    b, t, dc = glu.shape
    assert dc == CONV_GROUPS * LANES and b % nseq == 0 and t % tt == 0
    nt = t // tt
    assert nt == 1 or nseq == 1
    preshift = tt >= 4 * SUBLANES
    rc = min(tt, 64)
    vec = pl.BlockSpec((1, dc), lambda i, j: (0, 0))
    return pl.pallas_call(
        functools.partial(_conv_kernel, tt=tt, rc=rc, nseq=nseq, carry=nt > 1, preshift=preshift),
        grid=(b // nseq, nt),
        in_specs=[
            pl.BlockSpec((nseq, tt, dc), lambda i, j: (i, j, 0)),
            pl.BlockSpec((nseq, HALO, dc), lambda i, j: (i, 0, 0)),
            pl.BlockSpec((CONV_WIDTH, dc), lambda i, j: (0, 0)),
            vec, vec, vec,
        ],
        out_specs=pl.BlockSpec((nseq, tt, dc), lambda i, j: (i, j, 0)),
        out_shape=jax.ShapeDtypeStruct((b, t, dc), out_dtype),
        scratch_shapes=[pltpu.VMEM((SUBLANES if preshift else 1, HALO + tt, dc), F32)],
        compiler_params=_cparams(("arbitrary", "arbitrary")),
        name="conformer_conv",
    )(glu, state_pad, w_dw, b_dw, gn_g, gn_b)


def _moba_prompt_kernel(pt_ref, slopes_ref, q_ref, k_ref, v_ref, nrm_ref, ck_ref, o_ref, kmean_ref,
                        qb_ref, kb_ref, vt_ref, g_ref, biasp_ref, biasd_ref, pbuf, kmbuf, psem, osem,
                        *, nblk, qpt, kpi, n_steps, n_chunks, cps, pps, n_pages):
    h = pl.program_id(1)
    qi = pl.program_id(2)
    blk = MOBA_BLOCK
    qw = qpt * blk
    t = q_ref.shape[1]
    nb8 = -(-nblk // SUBLANES) * SUBLANES
    slope = slopes_ref[h]

    step = (pl.program_id(0) * N_HEADS + h) * (nblk // qpt) + qi
    c0 = step * cps
    exact = n_chunks == cps * n_steps
    ppb = blk // pbuf.shape[2]

    def start_chunk(chunk):
        slot = chunk % 2
        for i in range(pps):
            pltpu.make_async_copy(ck_ref.at[pt_ref[chunk * pps + i]], pbuf.at[slot, i], psem.at[slot]).start()

    def finish_chunk(chunk):
        slot = chunk % 2
        for i in range(pps):
            pltpu.make_async_copy(ck_ref.at[0], pbuf.at[slot, i], psem.at[slot]).wait()
        bi = (chunk * pps) // n_pages
        blk0 = ((chunk * pps) % n_pages) // ppb
        for i in range(pps // ppb):
            s = jnp.sum(pbuf[slot, ppb * i], axis=0)
            for p in range(1, ppb):
                s = s + jnp.sum(pbuf[slot, ppb * i + p], axis=0)
            kmbuf[bi, blk0 + i] = s * (1.0 / blk)

    def if_real(chunk, fn, always):
        if always:
            fn(chunk)
        else:
            pl.when(chunk < n_chunks)(functools.partial(fn, chunk))

    @pl.when(step == 0)
    def _():
        start_chunk(0)

    if_real(c0 + 1, start_chunk, exact and cps == 2)

    @pl.when(qi == 0)
    def _():
        q = q_ref[0]
        kf = k_ref[0]
        qb_ref[...] = (q * (HEAD_DIM ** -0.5)).astype(BF16)
        kb_ref[...] = kf.astype(BF16)
        vt_ref[...] = v_ref[0].T.astype(BF16)
        km = jnp.sum(kf.reshape(nblk, blk, HEAD_DIM), axis=1) * (1.0 / blk)
        if nb8 > nblk:
            km = jnp.concatenate([km, jnp.zeros((nb8 - nblk, HEAD_DIM), F32)], axis=0)
        gate = _nt_dot(km, q, precision=lax.Precision.HIGHEST)
        j_io = lax.broadcasted_iota(jnp.int32, gate.shape, 0)
        t_io = lax.broadcasted_iota(jnp.int32, gate.shape, 1)
        valid = j_io * blk + (blk - 1) < t_io - (t_io % blk)
        g = jnp.where(valid, gate, NEG_INF)
        sel = jnp.zeros(gate.shape, jnp.bool_)
        for _ in range(MOBA_TOPK):
            m = jnp.max(g, axis=0, keepdims=True)
            idx = jnp.min(jnp.where(g == m, j_io, nb8), axis=0, keepdims=True)
            pick = j_io == idx
            sel = jnp.logical_or(sel, pick)
            g = jnp.where(pick, -jnp.inf, g)
        gb = (slope * blk) * j_io.astype(F32) + jnp.where(jnp.logical_and(sel, valid), 0.0, NEG_INF)
        for j in range(nblk):
            g_ref[j] = jnp.broadcast_to(gb[j:j + 1, :], (SUBLANES, t))
        c = lax.broadcasted_iota(jnp.int32, (kpi * blk, qw), 0)
        biasp_ref[...] = slope * (c % blk).astype(F32)
        c = lax.broadcasted_iota(jnp.int32, (qw, qw), 0)
        r = lax.broadcasted_iota(jnp.int32, (qw, qw), 1)
        hidden = jnp.logical_or(c // blk > r // blk, jnp.logical_and(c // blk == r // blk, c % blk > r % blk))
        biasd_ref[...] = slope * (c % blk).astype(F32) + jnp.where(hidden, NEG_INF, 0.0)

    own = pl.ds(pl.multiple_of(qi * qw, qw), qw)
    qs = qb_ref[own, :]
    lane_blk = lax.broadcasted_iota(jnp.int32, (1, qw), 1) // blk

    def softmax_terms(st, gs, m_prev):
        segs = [st[i * blk:(i + 1) * blk] for i in range(len(gs))]
        m_new = m_prev
        for seg, g in zip(segs, gs):
            cand = jnp.max(seg, axis=0, keepdims=True) + g
            m_new = cand if m_new is None else jnp.maximum(m_new, cand)
        p = jnp.concatenate([jnp.exp(seg - (m_new - g)) for seg, g in zip(segs, gs)], axis=0)
        return m_new, p

    st = _nt_dot(kb_ref[own, :], qs) + biasd_ref[...]
    gs = []
    for i in range(qpt):
        g_blk = (slope * blk) * (qi * qpt + i).astype(F32)
        gs.append(jnp.where(lane_blk <= i, g_blk, g_ref[qi * qpt + i, 0:1, own]))
    m0, p = softmax_terms(st, gs, None)
    l0 = jnp.sum(p, axis=0, keepdims=True)
    acc0 = jnp.dot(vt_ref[:, own], p.astype(BF16), preferred_element_type=F32)

    n_past = qi * qpt

    def past_blocks(jp, carry):
        m, l, acc = carry
        rows = pl.ds(pl.multiple_of(jp * (kpi * blk), kpi * blk), kpi * blk)
        st = _nt_dot(kb_ref[rows, :], qs) + biasp_ref[...]
        gs = [jnp.where(jp * kpi + i < n_past, g_ref[jp * kpi + i, 0:1, own], NEG_INF) for i in range(kpi)]
        m_new, p = softmax_terms(st, gs, m)
        alpha = jnp.exp(m - m_new)
        l = alpha * l + jnp.sum(p, axis=0, keepdims=True)
        acc = alpha * acc + jnp.dot(vt_ref[:, rows], p.astype(BF16), preferred_element_type=F32)
        return m_new, l, acc

    _, l, acc = lax.fori_loop(0, (n_past + kpi - 1) // kpi, past_blocks, (m0, l0, acc0))
    ot = acc * (1.0 / l)
    ot = ot * lax.rsqrt(jnp.mean(ot * ot, axis=0, keepdims=True) + EPS)
    o_ref[0] = (ot.T * nrm_ref[...]).astype(o_ref.dtype)

    if_real(c0, finish_chunk, exact)
    if cps == 2:
        if_real(c0 + 2, start_chunk, False)
        if_real(c0 + 1, finish_chunk, exact)

    @pl.when(step == n_steps - 1)
    def _():
        out = pltpu.make_async_copy(kmbuf, kmean_ref, osem.at[0])
        out.start()
        out.wait()


def _moba_prompt(q, k, v, slopes, attn_norm, page_table, cache_k):
    b, t, _ = q.shape
    blk = MOBA_BLOCK
    nblk = t // blk
    assert t % blk == 0
    qpt = 2 if nblk % 2 == 0 else 1
    kpi = max(kk for kk in (4, 2, 1) if nblk % kk == 0)
    db, n_pages = page_table.shape
    _, page, nh, hd = cache_k.shape
    ppb = blk // page
    n_steps = b * N_HEADS * (nblk // qpt)
    pps = max(p for p in range(ppb, STREAM_PAGES + 1, ppb) if n_pages % p == 0)
    n_chunks = db * n_pages // pps
    cps = -(-n_chunks // n_steps)
    assert cps <= 2
    qw = qpt * blk
    seq = pl.BlockSpec((1, t, HEAD_DIM), lambda bi, h, qi, pt: (bi, 0, h))
    tile = pl.BlockSpec((1, qw, HEAD_DIM), lambda bi, h, qi, pt: (bi, qi, h))
    return pl.pallas_call(
        functools.partial(_moba_prompt_kernel, nblk=nblk, qpt=qpt, kpi=kpi, n_steps=n_steps, n_chunks=n_chunks,
                          cps=cps, pps=pps, n_pages=n_pages),
        grid_spec=pltpu.PrefetchScalarGridSpec(
            num_scalar_prefetch=1,
            grid=(b, N_HEADS, nblk // qpt),
            in_specs=[
                pl.BlockSpec(memory_space=pltpu.SMEM),
                seq, seq, seq,
                pl.BlockSpec((1, HEAD_DIM), lambda bi, h, qi, pt: (0, h)),
                pl.BlockSpec(memory_space=pl.ANY),
            ],
            out_specs=[tile, pl.BlockSpec(memory_space=pl.ANY)],
            scratch_shapes=[
                pltpu.VMEM((t, HEAD_DIM), BF16),
                pltpu.VMEM((t, HEAD_DIM), BF16),
                pltpu.VMEM((HEAD_DIM, t), BF16),
                pltpu.VMEM((nblk, SUBLANES, t), F32),
                pltpu.VMEM((kpi * blk, qw), F32),
                pltpu.VMEM((qw, qw), F32),
                pltpu.VMEM((2, pps, page, nh, hd), F32),
                pltpu.VMEM((db, n_pages // ppb, nh, hd), F32),
                pltpu.SemaphoreType.DMA((2,)),
                pltpu.SemaphoreType.DMA((1,)),
            ],
        ),
        out_shape=[jax.ShapeDtypeStruct((b, t, D_ATTN), BF16),
                   jax.ShapeDtypeStruct((db, n_pages // ppb, nh, hd), F32)],
        compiler_params=_cparams(("arbitrary", "arbitrary", "arbitrary")),
        name="moba_prompt",
    )(page_table.reshape(-1), slopes, q, k, v, attn_norm, cache_k)


def _sample_topk_kernel(q_ref, km_ref, idx_ref, *, nbf):
    for h in range(N_HEADS):
        gate = _nt_dot(q_ref[0, h], km_ref[0, h], precision=lax.Precision.HIGHEST)
        lane = lax.broadcasted_iota(jnp.int32, gate.shape, 1)
        g = jnp.where(lane < nbf, gate, -jnp.inf)
        out = jnp.zeros(gate.shape, jnp.int32)
        for r in range(MOBA_TOPK):
            m = jnp.max(g, axis=-1, keepdims=True)
            idx = jnp.min(jnp.where(g == m, lane, LANES), axis=-1, keepdims=True)
            out = jnp.where(lane == r, idx, out)
            g = jnp.where(lane == idx, -jnp.inf, g)
        idx_ref[0, h] = out


def _sample_topk(q_t, kmean_t, nbf):
    db, nh, s, hd = q_t.shape
    return pl.pallas_call(
        functools.partial(_sample_topk_kernel, nbf=nbf),
        grid=(db,),
        in_specs=[
            pl.BlockSpec((1, nh, s, hd), lambda i: (i, 0, 0, 0)),
            pl.BlockSpec((1, nh, LANES, hd), lambda i: (i, 0, 0, 0)),
        ],
        out_specs=pl.BlockSpec((1, nh, s, LANES), lambda i: (i, 0, 0, 0)),
        out_shape=jax.ShapeDtypeStruct((db, nh, s, LANES), jnp.int32),
        compiler_params=_cparams(("arbitrary",)),
        name="sample_topk",
    )(q_t, kmean_t)


def _moba_sample_kernel(pt_ref, blk_ref, slopes_ref, q_ref, kn_ref, vn_ref, nrm_ref,
                        glu_ref, w_ref, b_ref, gg_ref, gb_ref, ck_ref, cv_ref, o_ref, c_ref,
                        kbuf, vbuf, xs_ref, sem, *, n_units, n_sel, ppb, past, tt, rc, nt, cpu):
    page = ck_ref.shape[1]
    n_slab = n_sel * ppb
    n_keys = n_slab * page
    n_pages = past // page
    s_len = q_ref.shape[2]
    rows = 16

    def copies(u, slot, i, h, pg):
        dst = pl.ds(pl.multiple_of(i * page, page), page)
        return (pltpu.make_async_copy(ck_ref.at[pg, :, h, :], kbuf.at[slot, dst], sem.at[0, slot]),
                pltpu.make_async_copy(cv_ref.at[pg, :, h, :], vbuf.at[slot, dst], sem.at[1, slot]))

    def start(u, slot):
        h = u % N_HEADS
        row0 = (u // N_HEADS) * n_pages

        def sel(n, carry):
            logical = blk_ref[u * n_sel + n] * ppb
            for p in range(ppb):
                ck, cv = copies(u, slot, n * ppb + p, h, pt_ref[row0 + logical + p])
                ck.start()
                cv.start()
            return carry

        lax.fori_loop(0, n_sel, sel, 0)

    u = pl.program_id(0)

    @pl.when(u == 0)
    def _():
        start(0, 0)

    slot = u % 2
    bi = u // N_HEADS
    h = u % N_HEADS

    @pl.when(u + 1 < n_units)
    def _():
        start(u + 1, 1 - slot)

    for k in range(cpu):
        v = u * cpu + k
        cb = v % CONV_GROUPS
        tile = v // CONV_GROUPS

        @pl.when(cb == 0)
        def _():
            @pl.when(tile % nt == 0)
            def _():
                xs_ref[0, 0:HALO, :] = jnp.zeros((HALO, xs_ref.shape[2]), F32)

            xs_ref[0, HALO:HALO + tt, :] = glu_ref[0]
            _conv_preshift(xs_ref, tt)

        _conv_lane_block(xs_ref, w_ref, b_ref, gg_ref, gb_ref, c_ref, 0,
                         pl.ds(pl.multiple_of(cb * LANES, LANES), LANES), tt=tt, rc=rc, preshift=True)

        @pl.when(cb == CONV_GROUPS - 1)
        def _():
            xs_ref[0, 0:HALO, :] = xs_ref[0, tt:tt + HALO, :]

    def slab_done(i, carry):
        ck, cv = copies(u, slot, i, 0, 0)
        ck.wait()
        cv.wait()
        return carry

    lax.fori_loop(0, n_slab, slab_done, 0)

    slope = slopes_ref[h]
    q = q_ref[bi, h]
    qs = jnp.concatenate([q * (HEAD_DIM ** -0.5), jnp.zeros((rows - s_len, HEAD_DIM), F32)], axis=0)
    qs = qs.astype(BF16)
    kb = kbuf[slot].astype(BF16)
    vb = vbuf[slot].astype(BF16)
    s = _nt_dot(qs, kb)

    per_blk = page * ppb
    offs = lax.broadcasted_iota(jnp.int32, (1, per_blk), 1)
    kpos = jnp.concatenate(
        [(blk_ref[u * n_sel + i] * per_blk + offs).astype(F32) for i in range(n_sel)], axis=1)
    row = lax.broadcasted_iota(jnp.int32, (rows, 1), 0)
    qpos = (past + row).astype(F32)
    s = s - slope * (qpos - kpos)
    col = lax.broadcasted_iota(jnp.int32, (rows, n_keys), 1)
    per_q = MOBA_TOPK * per_blk
    mine = jnp.logical_and(col >= row * per_q, col < (row + 1) * per_q)
    s = jnp.where(mine, s, NEG_INF)

    kn = jnp.concatenate([kn_ref[bi, h], jnp.zeros((rows - s_len, HEAD_DIM), F32)], axis=0).astype(BF16)
    vn = jnp.concatenate([vn_ref[bi, h], jnp.zeros((rows - s_len, HEAD_DIM), F32)], axis=0).astype(BF16)
    ri = lax.broadcasted_iota(jnp.int32, (rows, rows), 0)
    ci = lax.broadcasted_iota(jnp.int32, (rows, rows), 1)
    so = _nt_dot(qs, kn) - slope * (ri - ci).astype(F32)
    so = jnp.where(jnp.logical_and(ri >= ci, ci < s_len), so, NEG_INF)

    m = jnp.maximum(jnp.max(s, axis=-1, keepdims=True), jnp.max(so, axis=-1, keepdims=True))
    p = jnp.exp(s - m)
    po = jnp.exp(so - m)
    l = jnp.sum(p, axis=-1, keepdims=True) + jnp.sum(po, axis=-1, keepdims=True)
    acc = (jnp.dot(p.astype(BF16), vb, preferred_element_type=F32)
           + jnp.dot(po.astype(BF16), vn, preferred_element_type=F32))
    o = (acc / l)[0:s_len]
    o = (o * lax.rsqrt(jnp.mean(o * o, axis=-1, keepdims=True) + EPS)) * nrm_ref[pl.ds(h, 1), :]
    o_ref[bi, h] = o


def _moba_sample(page_table_flat, blks, slopes, q_t, kn_t, vn_t, attn_norm_hd, cache_k, cache_v, past,
                 glu, w_dw, b_dw, gn_g, gn_b, tt):
    db, nh, s, hd = q_t.shape
    _, page, _, _ = cache_k.shape
    ppb = MOBA_BLOCK // page
    n_sel = s * MOBA_TOPK
    n_units = db * nh
    b, t, dc = glu.shape
    assert dc == CONV_GROUPS * LANES and t % tt == 0 and tt >= 4 * SUBLANES
    nt = t // tt
    n_conv_units = b * nt * CONV_GROUPS
    cpu = -(-n_conv_units // n_units)
    assert CONV_GROUPS % cpu == 0 and cpu * n_units == n_conv_units
    whole = lambda shp: pl.BlockSpec(shp, lambda i, *_: (0,) * len(shp))
    tile = pl.BlockSpec((1, tt, dc), lambda i, *_: ((i * cpu // CONV_GROUPS) // nt, (i * cpu // CONV_GROUPS) % nt, 0))
    return pl.pallas_call(
        functools.partial(_moba_sample_kernel, n_units=n_units, n_sel=n_sel, ppb=ppb, past=past,
                          tt=tt, rc=min(tt, 64), nt=nt, cpu=cpu),
        grid_spec=pltpu.PrefetchScalarGridSpec(
            num_scalar_prefetch=3,
            grid=(n_units,),
            in_specs=[
                whole(q_t.shape), whole(kn_t.shape), whole(vn_t.shape), whole(attn_norm_hd.shape),
                tile, whole(w_dw.shape), whole(b_dw.shape), whole(gn_g.shape), whole(gn_b.shape),
                pl.BlockSpec(memory_space=pl.ANY),
                pl.BlockSpec(memory_space=pl.ANY),
            ],
            out_specs=[whole(q_t.shape), tile],
            scratch_shapes=[
                pltpu.VMEM((2, n_sel * ppb * page, hd), F32),
                pltpu.VMEM((2, n_sel * ppb * page, hd), F32),
                pltpu.VMEM((SUBLANES, HALO + tt, dc), F32),
                pltpu.SemaphoreType.DMA((2, 2)),
            ],
        ),
        out_shape=[jax.ShapeDtypeStruct(q_t.shape, F32), jax.ShapeDtypeStruct((b, t, dc), BF16)],
        compiler_params=_cparams(("arbitrary",)),
        name="moba_sample_conv",
    )(page_table_flat, blks, slopes, q_t, kn_t, vn_t, attn_norm_hd, glu, w_dw, b_dw, gn_g, gn_b, cache_k, cache_v)


def _outproj_kernel(op_ref, cp_ref, xp_ref, os_ref, cs_ref, xs_ref, wo_ref, wc_ref, g_ref, wr_ref, br_ref,
                    x2_ref, h2_ref, route_ref, *, nb_p):
    i = pl.program_id(0)
    tail = (wo_ref, wc_ref, g_ref, wr_ref, br_ref, x2_ref, h2_ref, route_ref)

    @pl.when(i < nb_p)
    def _():
        _outproj_tile(op_ref, cp_ref, xp_ref, *tail)

    @pl.when(i >= nb_p)
    def _():
        _outproj_tile(os_ref, cs_ref, xs_ref, *tail)


def _outproj_tile(o_ref, c_ref, x_ref, wo_ref, wc_ref, g_ref, wr_ref, br_ref, x2_ref, h2_ref, route_ref):
    mix = (jnp.dot(o_ref[...].astype(BF16), wo_ref[...], preferred_element_type=F32)
           + jnp.dot(c_ref[...].astype(BF16), wc_ref[...], preferred_element_type=F32))
    x2 = x_ref[...] + mix
    x2_ref[...] = x2
    r = lax.rsqrt(jnp.mean(x2 * x2, axis=-1, keepdims=True) + EPS)
    h2 = (x2 * r) * g_ref[...]
    h2_ref[...] = h2

    h_hi = h2.astype(BF16)
    h_lo = (h2 - h_hi.astype(F32)).astype(BF16)
    r1 = jnp.dot(h_hi, wr_ref[...], preferred_element_type=F32)
    r2 = jnp.dot(h_lo, wr_ref[:, 0:LANES], preferred_element_type=F32)
    logits = r1[:, 0:LANES] + r1[:, LANES:2 * LANES] + r2 + br_ref[...]

    lane = lax.broadcasted_iota(jnp.int32, logits.shape, 1)
    is_g = lane < N_EXPERT_GROUPS
    gl = jnp.where(is_g, logits, -jnp.inf)
    gmax = jnp.max(gl, axis=-1, keepdims=True)
    gidx = jnp.min(jnp.where(gl == gmax, lane, LANES), axis=-1, keepdims=True)
    gsum = jnp.sum(jnp.where(is_g, jnp.exp(gl - gmax), 0.0), axis=-1, keepdims=True)
    g_gate = 1.0 / gsum

    lo = N_EXPERT_GROUPS + gidx * EXPERTS_PER_GROUP
    el = jnp.where(jnp.logical_and(lane >= lo, lane < lo + EXPERTS_PER_GROUP), logits, -jnp.inf)
    v0 = jnp.max(el, axis=-1, keepdims=True)
    i0 = jnp.min(jnp.where(el == v0, lane, LANES), axis=-1, keepdims=True)
    el = jnp.where(lane == i0, -jnp.inf, el)
    v1 = jnp.max(el, axis=-1, keepdims=True)
    i1 = jnp.min(jnp.where(el == v1, lane, LANES), axis=-1, keepdims=True)
    e = jnp.exp(v1 - v0)
    w0 = g_gate / (1.0 + e)
    w1 = g_gate * e / (1.0 + e)

    out = jnp.where(lane == 0, (i0 - N_EXPERT_GROUPS).astype(F32), 0.0)
    out = jnp.where(lane == 1, (i1 - N_EXPERT_GROUPS).astype(F32), out)
    out = jnp.where(lane == 2, w0, out)
    out = jnp.where(lane == 3, w1, out)
    route_ref[...] = out


def _out_projection(o_p, c_p, x_p, o_s, c_s, x_s, wo, wc, norm_g, wr, br, tm):
    np_, d = x_p.shape
    ns_ = x_s.shape[0]
    assert np_ % tm == 0 and ns_ % tm == 0
    nb_p, nb_s = np_ // tm, ns_ // tm
    n = np_ + ns_
    prow = lambda w: pl.BlockSpec((tm, w), lambda i: (jnp.minimum(i, nb_p - 1), 0))
    srow = lambda w: pl.BlockSpec((tm, w), lambda i: (jnp.maximum(i - nb_p, 0), 0))
    row = lambda w: pl.BlockSpec((tm, w), lambda i: (i, 0))
    full = lambda a: pl.BlockSpec(a.shape, lambda i: (0, 0))
    return pl.pallas_call(
        functools.partial(_outproj_kernel, nb_p=nb_p),
        grid=(nb_p + nb_s,),
        in_specs=[prow(o_p.shape[1]), prow(c_p.shape[1]), prow(d), srow(o_s.shape[1]), srow(c_s.shape[1]), srow(d),
                  full(wo), full(wc), full(norm_g), full(wr), full(br)],
        out_specs=[row(d), row(d), row(LANES)],
        out_shape=[jax.ShapeDtypeStruct((n, d), F32), jax.ShapeDtypeStruct((n, d), F32),
                   jax.ShapeDtypeStruct((n, LANES), F32)],
        compiler_params=_cparams(("arbitrary",)),
        name="out_projection_router",
    )(o_p, c_p, x_p, o_s, c_s, x_s, wo, wc, norm_g, wr, br)


DMA_UNROLL = 8


def _for_rows(n, fn):
    def group(gi, carry):
        for k in range(DMA_UNROLL):
            fn(gi * DMA_UNROLL + k)
        return carry

    def single(i, carry):
        fn(i)
        return carry

    n_grp = lax.div(n, DMA_UNROLL)
    lax.fori_loop(0, n_grp, group, 0)
    lax.fori_loop(n_grp * DMA_UNROLL, n, single, 0)


def _moe_kernel(order_ref, cexp_ref, cbase_ref, cnv_ref, h_ref, wg_ref, wu_ref, wd_ref, out_ref,
                xbuf, obuf, wgb, wub, wdb, gsem, ssem, *, n_tok, n_chunks):
    g = pl.program_id(0)
    slot = g % 2

    def gather(chunk, buf_slot):
        base = cbase_ref[chunk]

        def desc(i):
            tok = lax.shift_right_logical(order_ref[base + i], 1)
            return pltpu.make_async_copy(h_ref.at[pl.ds(tok, 1)], xbuf.at[buf_slot, pl.ds(i, 1)], gsem.at[buf_slot])
        return desc

    def scatter(chunk, buf_slot):
        base = cbase_ref[chunk]

        def desc(i):
            pair = order_ref[base + i]
            row = (pair & 1) * n_tok + lax.shift_right_logical(pair, 1)
            return pltpu.make_async_copy(obuf.at[buf_slot, pl.ds(i, 1)], out_ref.at[pl.ds(row, 1)], ssem.at[buf_slot])
        return desc

    def gather_done(buf_slot):
        pltpu.make_async_copy(h_ref.at[pl.ds(0, 1)], xbuf.at[buf_slot, pl.ds(0, 1)], gsem.at[buf_slot]).wait()

    def scatter_done(buf_slot):
        pltpu.make_async_copy(obuf.at[buf_slot, pl.ds(0, 1)], out_ref.at[pl.ds(0, 1)], ssem.at[buf_slot]).wait()

    @pl.when(g == 0)
    def _():
        xbuf[...] = jnp.zeros(xbuf.shape, xbuf.dtype)
        _for_rows(cnv_ref[0], lambda i: gather(0, 0)(i).start())

    @pl.when(jnp.logical_or(g == 0, cexp_ref[g] != cexp_ref[jnp.maximum(g - 1, 0)]))
    def _():
        wgb[...] = wg_ref[0].astype(BF16)
        wub[...] = wu_ref[0].astype(BF16)
        wdb[...] = wd_ref[0].astype(BF16)

    @pl.when(g + 1 < n_chunks)
    def _():
        _for_rows(cnv_ref[g + 1], lambda i: gather(g + 1, 1 - slot)(i).start())

    nv = cnv_ref[g]
    _for_rows(nv, lambda i: gather_done(slot))

    @pl.when(g >= 2)
    def _():
        _for_rows(cnv_ref[g - 2], lambda i: scatter_done(slot))

    @pl.when(nv > 0)
    def _():
        x = xbuf[slot].astype(BF16)
        gt = jnp.dot(x, wgb[...], preferred_element_type=F32)
        up = jnp.dot(x, wub[...], preferred_element_type=F32)
        hid = (gt * (1.0 / (1.0 + jnp.exp(-gt)))) * up
        obuf[slot] = jnp.dot(hid.astype(BF16), wdb[...], preferred_element_type=F32)

    _for_rows(nv, lambda i: scatter(g, slot)(i).start())

    @pl.when(g == n_chunks - 1)
    def _():
        if n_chunks >= 2:
            _for_rows(cnv_ref[g - 1], lambda i: scatter_done(1 - slot))
        _for_rows(nv, lambda i: scatter_done(slot))


def _moe(order, cexp, cbase, cnv, h2, w_gate, w_up, w_down, ch):
    n, d = h2.shape
    ne, _, de = w_gate.shape
    n_chunks = cexp.shape[0]
    wspec = lambda shp: pl.BlockSpec((1,) + shp, lambda g, order, cexp, cbase, cnv: (cexp[g], 0, 0))
    return pl.pallas_call(
        functools.partial(_moe_kernel, n_tok=n, n_chunks=n_chunks),
        grid_spec=pltpu.PrefetchScalarGridSpec(
            num_scalar_prefetch=4,
            grid=(n_chunks,),
            in_specs=[
                pl.BlockSpec(memory_space=pl.ANY),
                wspec((d, de)), wspec((d, de)), wspec((de, d)),
            ],
            out_specs=pl.BlockSpec(memory_space=pl.ANY),
            scratch_shapes=[
                pltpu.VMEM((2, ch, d), F32),
                pltpu.VMEM((2, ch, d), F32),
                pltpu.VMEM((d, de), BF16),
                pltpu.VMEM((d, de), BF16),
                pltpu.VMEM((de, d), BF16),
                pltpu.SemaphoreType.DMA((2,)),
                pltpu.SemaphoreType.DMA((2,)),
            ],
        ),
        out_shape=jax.ShapeDtypeStruct((2 * n, d), F32),
        compiler_params=_cparams(("arbitrary",)),
        name="expert_mlp",
    )(order, cexp, cbase, cnv, h2, w_gate, w_up, w_down)


def _expert_chunks(eid, ch):
    n_pairs = eid.shape[0]
    n_chunks = N_EXPERTS + n_pairs // ch
    order = jnp.argsort(eid, stable=True).astype(jnp.int32)
    counts = jnp.sum(eid[:, None] == jnp.arange(N_EXPERTS, dtype=jnp.int32)[None, :], axis=0, dtype=jnp.int32)
    offs = jnp.cumsum(counts, dtype=jnp.int32) - counts
    nch = (counts + ch - 1) // ch
    cum = jnp.cumsum(nch, dtype=jnp.int32)
    gidx = jnp.arange(n_chunks, dtype=jnp.int32)
    cexp_raw = jnp.sum(gidx[:, None] >= cum[None, :], axis=1, dtype=jnp.int32)
    last = jnp.maximum(jnp.max(jnp.where(nch > 0, jnp.arange(N_EXPERTS, dtype=jnp.int32), 0)), 0)
    used = gidx < cum[-1]
    cexp = jnp.where(used, jnp.minimum(cexp_raw, N_EXPERTS - 1), last)
    local = gidx - (cum[cexp] - nch[cexp])
    cbase = jnp.where(used, offs[cexp] + local * ch, 0)
    cnv = jnp.where(used, jnp.clip(counts[cexp] - local * ch, 0, ch), 0)
    return order, cexp.astype(jnp.int32), cbase.astype(jnp.int32), cnv.astype(jnp.int32)


def _final_kernel(x2_ref, ya_ref, yb_ref, route_ref, g_ref, yp_ref, ys_ref, *, nb_p):
    i = pl.program_id(0)
    rt = route_ref[...]
    x3 = x2_ref[...] + rt[:, 2:3] * ya_ref[...] + rt[:, 3:4] * yb_ref[...]
    r = lax.rsqrt(jnp.mean(x3 * x3, axis=-1, keepdims=True) + EPS)
    y = (x3 * r) * g_ref[...]

    @pl.when(i < nb_p)
    def _():
        yp_ref[...] = y

    @pl.when(i >= nb_p)
    def _():
        ys_ref[...] = y


def _final(x2, y2, route, norm_g, np_, tm):
    n, d = x2.shape
    assert np_ % tm == 0 and n % tm == 0
    nb, nb_p = n // tm, np_ // tm
    return pl.pallas_call(
        functools.partial(_final_kernel, nb_p=nb_p),
        grid=(nb,),
        in_specs=[
            pl.BlockSpec((tm, d), lambda i: (i, 0)),
            pl.BlockSpec((tm, d), lambda i: (i, 0)),
            pl.BlockSpec((tm, d), lambda i: (i + nb, 0)),
            pl.BlockSpec((tm, LANES), lambda i: (i, 0)),
            pl.BlockSpec((1, d), lambda i: (0, 0)),
        ],
        out_specs=[
            pl.BlockSpec((tm, d), lambda i: (jnp.minimum(i, nb_p - 1), 0)),
            pl.BlockSpec((tm, d), lambda i: (jnp.maximum(i - nb_p, 0), 0)),
        ],
        out_shape=[jax.ShapeDtypeStruct((np_, d), F32), jax.ShapeDtypeStruct((n - np_, d), F32)],
        compiler_params=_cparams(("arbitrary",)),
        name="combine_final_norm",
    )(x2, y2, y2, route, norm_g)


def _pick_tile(n, pref):
    t = min(n, pref)
    while n % t:
        t //= 2
    return t


def kernel(x_prompt, x_sample, cache_k, cache_v, state_conv, page_table, norm_mix, w_in, w_dw, b_dw,
           conv_norm_g, conv_norm_b, attn_out_norm, w_out, norm_ffn, w_router_group, b_router_group,
           w_router_expert, b_router_expert, w_gate, w_up, w_down, norm_final):
    depth = norm_mix.shape[0]
    assert depth == 1
    b, t, d = x_prompt.shape
    db, s, _ = x_sample.shape
    n_pages = page_table.shape[1]
    page = cache_k.shape[2]
    past = n_pages * page
    nbf = past // MOBA_BLOCK
    assert past % MOBA_BLOCK == 0 and nbf >= MOBA_TOPK and nbf <= LANES
    d_conv = d - D_ATTN
    assert d_conv == D_ATTN
    np_, ns_ = b * t, db * s

    slopes = jnp.exp2(-8.0 * jnp.arange(1, N_HEADS + 1, dtype=F32) / N_HEADS)
    w_in_bf = w_in[0].astype(BF16)
    wo_bf = w_out[0, :D_ATTN].astype(BF16)
    wc_bf = w_out[0, D_ATTN:].astype(BF16)
    wr = jnp.zeros((d, LANES), F32)
    wr = wr.at[:, :N_EXPERT_GROUPS].set(w_router_group[0])
    wr = wr.at[:, N_EXPERT_GROUPS:N_EXPERT_GROUPS + N_EXPERTS].set(w_router_expert[0])
    wr_hi = wr.astype(BF16)
    wr_lo = (wr - wr_hi.astype(F32)).astype(BF16)
    wr_cat = jnp.concatenate([wr_hi, wr_lo], axis=1)
    br = jnp.zeros((1, LANES), F32)
    br = br.at[0, :N_EXPERT_GROUPS].set(b_router_group[0])
    br = br.at[0, N_EXPERT_GROUPS:N_EXPERT_GROUPS + N_EXPERTS].set(b_router_expert[0])
    g_mix = norm_mix[0].reshape(1, d)
    g_ffn = norm_ffn[0].reshape(1, d)
    g_fin = norm_final.reshape(1, d)
    attn_norm = attn_out_norm[0].reshape(1, D_ATTN)
    conv_vec = lambda a: a[0].reshape(1, d_conv)

    xp = x_prompt.reshape(np_, d)
    xs = x_sample.reshape(ns_, d)
    qp, kp, vp, glu_p = _in_projection(xp, g_mix, w_in_bf, _pick_tile(np_, 512))
    qs, ks, vs, glu_s = _in_projection(xs, g_mix, w_in_bf, _pick_tile(ns_, 256))

    glu_p3 = glu_p.reshape(b, t, d_conv)
    glu_s3 = glu_s.reshape(db, s, d_conv)
    pad = HALO - (CONV_WIDTH - 1)
    st_s = jnp.pad(state_conv[0], ((0, 0), (pad, 0), (0, 0)))
    conv_args = (w_dw[0], conv_vec(b_dw), conv_vec(conv_norm_g), conv_vec(conv_norm_b))
    c_s = _conformer_conv(glu_s3, st_s, *conv_args, tt=s, nseq=db, out_dtype=F32)
    conv_prompt = glu_p3[:, t - (CONV_WIDTH - 1):][None]
    conv_sample = jnp.concatenate([state_conv[0], glu_s3], axis=1)[:, -(CONV_WIDTH - 1):][None]

    ck = cache_k[0]
    cv = cache_v[0]
    o_p, kmean = _moba_prompt(qp.reshape(b, t, D_ATTN), kp.reshape(b, t, D_ATTN), vp.reshape(b, t, D_ATTN),
                              slopes, attn_norm, page_table, ck)

    kmean_t = jnp.pad(kmean.transpose(0, 2, 1, 3), ((0, 0), (0, 0), (0, LANES - nbf), (0, 0)))
    to_heads = lambda a: a.reshape(db, s, N_HEADS, HEAD_DIM).transpose(0, 2, 1, 3)
    q_t, kn_t, vn_t = to_heads(qs), to_heads(ks), to_heads(vs)
    idx = _sample_topk(q_t, kmean_t, nbf)[..., :MOBA_TOPK]
    o_t, c_p = _moba_sample(page_table.reshape(-1), idx.reshape(-1), slopes, q_t, kn_t, vn_t,
                            attn_norm.reshape(N_HEADS, HEAD_DIM), ck, cv, past,
                            glu_p3, *conv_args, tt=_pick_tile(t, 256))
    o_s = o_t.transpose(0, 2, 1, 3).reshape(ns_, D_ATTN)

    tm = _pick_tile(ns_, 256)
    x2, h2, route = _out_projection(o_p.reshape(np_, D_ATTN), c_p.reshape(np_, d_conv), xp,
                                    o_s, c_s.reshape(ns_, d_conv), xs,
                                    wo_bf, wc_bf, g_ffn, wr_cat, br, tm)

    ch = 256
    eid = route[:, 0:2].astype(jnp.int32).reshape(-1)
    order, cexp, cbase, cnv = _expert_chunks(eid, ch)
    y2 = _moe(order, cexp, cbase, cnv, h2, w_gate[0], w_up[0], w_down[0], ch)

    y_p, y_s = _final(x2, y2, route, g_fin, np_, tm)

    page_shape = (depth, b, t // page, page, N_HEADS, HEAD_DIM)
    return (y_p.reshape(b, t, d), y_s.reshape(db, s, d),
            kp.reshape(page_shape), vp.reshape(page_shape), conv_prompt,
            ks.reshape(depth, db, s, N_HEADS, HEAD_DIM), vs.reshape(depth, db, s, N_HEADS, HEAD_DIM),
            conv_sample)
```

```python
import functools

import jax
import jax.numpy as jnp
from jax import lax
from jax.experimental import pallas as pl
from jax.experimental.pallas import tpu as pltpu

F32 = jnp.float32
BF16 = jnp.bfloat16

N_HEADS = 8
HEAD_DIM = 128
D_ATTN = N_HEADS * HEAD_DIM
CONV_GROUPS = 8
CONV_WIDTH = 31
MOBA_BLOCK = 256
MOBA_TOPK = 3
N_EXPERT_GROUPS = 4
EXPERTS_PER_GROUP = 8
N_EXPERTS = N_EXPERT_GROUPS * EXPERTS_PER_GROUP
EPS = 1e-6
NEG_INF = -1e30

LANES = 128
SUBLANES = 8
HALO = 32
VMEM_LIMIT = 56 * 1024 * 1024
STREAM_PAGES = 16


def _cparams(sem=None):
    return pltpu.CompilerParams(dimension_semantics=sem, vmem_limit_bytes=VMEM_LIMIT)


def _nt_dot(a, b, precision=None):
    return lax.dot_general(a, b, (((1,), (1,)), ((), ())), precision=precision,
                           preferred_element_type=F32)


def _inproj_kernel(x_ref, g_ref, w_ref, wgate_ref, q_ref, k_ref, v_ref, glu_ref, h_ref):
    j = pl.program_id(1)

    @pl.when(j == 0)
    def _():
        x = x_ref[...]
        r = lax.rsqrt(jnp.mean(x * x, axis=-1, keepdims=True) + EPS)
        h_ref[...] = ((x * r) * g_ref[...]).astype(BF16)

    z = jnp.dot(h_ref[...], w_ref[...], preferred_element_type=F32)

    @pl.when(j == 0)
    def _():
        q_ref[...] = z

    @pl.when(j == 1)
    def _():
        k_ref[...] = z

    @pl.when(j == 2)
    def _():
        v_ref[...] = z

    @pl.when(j == 3)
    def _():
        gate = jnp.dot(h_ref[...], wgate_ref[...], preferred_element_type=F32)
        glu_ref[...] = z * (1.0 / (1.0 + jnp.exp(-gate)))


def _in_projection(x, norm_g, w_in_bf, tm):
    n, d = x.shape
    dc = D_ATTN
    out = jax.ShapeDtypeStruct((n, dc), F32)
    row = pl.BlockSpec((tm, dc), lambda i, j: (i, 0))
    return pl.pallas_call(
        _inproj_kernel,
        grid=(n // tm, 4),
        in_specs=[
            pl.BlockSpec((tm, d), lambda i, j: (i, 0)),
            pl.BlockSpec((1, d), lambda i, j: (0, 0)),
            pl.BlockSpec((d, dc), lambda i, j: (0, j)),
            pl.BlockSpec((d, dc), lambda i, j: (0, 4)),
        ],
        out_specs=[row, row, row, row],
        out_shape=[out, out, out, out],
        scratch_shapes=[pltpu.VMEM((tm, d), BF16)],
        compiler_params=_cparams(("arbitrary", "arbitrary")),
        name="in_projection",
    )(x, norm_g, w_in_bf, w_in_bf)


def _conv_lane_block(xs_ref, w_ref, b_ref, gg_ref, gb_ref, c_ref, sq, cs, *, tt, rc, preshift):
    off = HALO - (CONV_WIDTH - 1)
    for r0 in range(0, tt, rc):
        acc = jnp.broadcast_to(b_ref[:, cs], (rc, LANES))
        for j in range(CONV_WIDTH):
            o = off + j
            if preshift:
                lo = r0 + (o // SUBLANES) * SUBLANES
                x = xs_ref[o % SUBLANES, lo:lo + rc, cs]
            else:
                x = xs_ref[0, r0 + o:r0 + o + rc, cs]
            acc = acc + x * w_ref[j:j + 1, cs]
        mu = jnp.mean(acc, axis=-1, keepdims=True)
        dlt = acc - mu
        var = jnp.mean(dlt * dlt, axis=-1, keepdims=True)
        yn = dlt * lax.rsqrt(var + EPS)
        yn = yn * gg_ref[:, cs] + gb_ref[:, cs]
        c_ref[sq, r0:r0 + rc, cs] = (yn * (1.0 / (1.0 + jnp.exp(-yn)))).astype(c_ref.dtype)


def _conv_preshift(xs_ref, tt):
    span = HALO + tt - SUBLANES
    for r in range(1, SUBLANES):
        xs_ref[r, 0:span, :] = xs_ref[0, r:r + span, :]


def _conv_kernel(glu_ref, st_ref, w_ref, b_ref, gg_ref, gb_ref, c_ref, xs_ref, *, tt, rc, nseq, carry, preshift):
    t = pl.program_id(1)

    def lane_block(sq, cs):
        _conv_lane_block(xs_ref, w_ref, b_ref, gg_ref, gb_ref, c_ref, sq, cs, tt=tt, rc=rc, preshift=preshift)

    hist = HALO - (CONV_WIDTH - 1)

    def load_state(sq):
        xs_ref[0, 0:hist, :] = jnp.zeros((hist, xs_ref.shape[2]), F32)
        xs_ref[0, hist:HALO, :] = st_ref[sq]

    def one_seq(sq):
        if carry:
            pl.when(t == 0)(functools.partial(load_state, sq))
        else:
            load_state(sq)
        xs_ref[0, HALO:HALO + tt, :] = glu_ref[sq]
        if preshift:
            _conv_preshift(xs_ref, tt)

            def group(cb, carry_):
                lane_block(sq, pl.ds(pl.multiple_of(cb * LANES, LANES), LANES))
                return carry_

            lax.fori_loop(0, CONV_GROUPS, group, 0)
        else:
            for cb in range(CONV_GROUPS):
                lane_block(sq, pl.ds(cb * LANES, LANES))
        if carry:
            xs_ref[0, 0:HALO, :] = xs_ref[0, tt:tt + HALO, :]

    if nseq == 1:
        one_seq(0)
    else:
        def seq(sq, carry_):
            one_seq(sq)
            return carry_

        lax.fori_loop(0, nseq, seq, 0)


def _conformer_conv(glu, state, w_dw, b_dw, gn_g, gn_b, tt, nseq, out_dtype):
    b, t, dc = glu.shape
    assert dc == CONV_GROUPS * LANES and b % nseq == 0 and t % tt == 0 and state.shape[1] == CONV_WIDTH - 1
    nt = t // tt
    assert nt == 1 or nseq == 1
    preshift = tt >= 4 * SUBLANES
    rc = min(tt, 64)
    vec = pl.BlockSpec((1, dc), lambda i, j: (0, 0))
    return pl.pallas_call(
        functools.partial(_conv_kernel, tt=tt, rc=rc, nseq=nseq, carry=nt > 1, preshift=preshift),
        grid=(b // nseq, nt),
        in_specs=[
            pl.BlockSpec((nseq, tt, dc), lambda i, j: (i, j, 0)),
            pl.BlockSpec((nseq, CONV_WIDTH - 1, dc), lambda i, j: (i, 0, 0)),
            pl.BlockSpec((CONV_WIDTH, dc), lambda i, j: (0, 0)),
            vec, vec, vec,
        ],
        out_specs=pl.BlockSpec((nseq, tt, dc), lambda i, j: (i, j, 0)),
        out_shape=jax.ShapeDtypeStruct((b, t, dc), out_dtype),
        scratch_shapes=[pltpu.VMEM((SUBLANES if preshift else 1, HALO + tt, dc), F32)],
        compiler_params=_cparams(("arbitrary", "arbitrary")),
        name="conformer_conv",
    )(glu, state, w_dw, b_dw, gn_g, gn_b)


def _moba_prompt_kernel(pt_ref, slopes_ref, q_ref, k_ref, v_ref, nrm_ref, ck_ref, o_ref, kmean_ref,
                        qb_ref, kb_ref, vt_ref, g_ref, biasp_ref, biasd_ref, pbuf, kmbuf, psem, osem,
                        *, nblk, qpt, kpi, n_steps, n_chunks, cps, pps, n_pages):
    h = pl.program_id(1)
    qi = pl.program_id(2)
    blk = MOBA_BLOCK
    qw = qpt * blk
    t = q_ref.shape[1]
    nb8 = -(-nblk // SUBLANES) * SUBLANES
    slope = slopes_ref[h]

    step = (pl.program_id(0) * N_HEADS + h) * (nblk // qpt) + qi
    c0 = step * cps
    exact = n_chunks == cps * n_steps
    ppb = blk // pbuf.shape[2]

    def start_chunk(chunk):
        slot = chunk % 2
        for i in range(pps):
            pltpu.make_async_copy(ck_ref.at[pt_ref[chunk * pps + i]], pbuf.at[slot, i], psem.at[slot]).start()

    def finish_chunk(chunk):
        slot = chunk % 2
        for i in range(pps):
            pltpu.make_async_copy(ck_ref.at[0], pbuf.at[slot, i], psem.at[slot]).wait()
        bi = (chunk * pps) // n_pages
        blk0 = ((chunk * pps) % n_pages) // ppb
        for i in range(pps // ppb):
            s = jnp.sum(pbuf[slot, ppb * i], axis=0)
            for p in range(1, ppb):
                s = s + jnp.sum(pbuf[slot, ppb * i + p], axis=0)
            kmbuf[bi, blk0 + i] = s * (1.0 / blk)

    def if_real(chunk, fn, always):
        if always:
            fn(chunk)
        else:
            pl.when(chunk < n_chunks)(functools.partial(fn, chunk))

    @pl.when(step == 0)
    def _():
        start_chunk(0)

    if_real(c0 + 1, start_chunk, exact and cps == 2)

    @pl.when(qi == 0)
    def _():
        q = q_ref[0]
        kf = k_ref[0]
        qb_ref[...] = (q * (HEAD_DIM ** -0.5)).astype(BF16)
        kb_ref[...] = kf.astype(BF16)
        vt_ref[...] = v_ref[0].T.astype(BF16)
        km = jnp.sum(kf.reshape(nblk, blk, HEAD_DIM), axis=1) * (1.0 / blk)
        if nb8 > nblk:
            km = jnp.concatenate([km, jnp.zeros((nb8 - nblk, HEAD_DIM), F32)], axis=0)
        gate = _nt_dot(km, q, precision=lax.Precision.HIGHEST)
        j_io = lax.broadcasted_iota(jnp.int32, gate.shape, 0)
        t_io = lax.broadcasted_iota(jnp.int32, gate.shape, 1)
        valid = j_io * blk + (blk - 1) < t_io - (t_io % blk)
        g = jnp.where(valid, gate, NEG_INF)
        sel = jnp.zeros(gate.shape, jnp.bool_)
        for _ in range(MOBA_TOPK):
            m = jnp.max(g, axis=0, keepdims=True)
            idx = jnp.min(jnp.where(g == m, j_io, nb8), axis=0, keepdims=True)
            pick = j_io == idx
            sel = jnp.logical_or(sel, pick)
            g = jnp.where(pick, -jnp.inf, g)
        gb = (slope * blk) * j_io.astype(F32) + jnp.where(jnp.logical_and(sel, valid), 0.0, NEG_INF)
        for j in range(nblk):
            g_ref[j] = jnp.broadcast_to(gb[j:j + 1, :], (SUBLANES, t))
        c = lax.broadcasted_iota(jnp.int32, (kpi * blk, qw), 0)
        biasp_ref[...] = slope * (c % blk).astype(F32)
        c = lax.broadcasted_iota(jnp.int32, (qw, qw), 0)
        r = lax.broadcasted_iota(jnp.int32, (qw, qw), 1)
        hidden = jnp.logical_or(c // blk > r // blk, jnp.logical_and(c // blk == r // blk, c % blk > r % blk))
        biasd_ref[...] = slope * (c % blk).astype(F32) + jnp.where(hidden, NEG_INF, 0.0)

    own = pl.ds(pl.multiple_of(qi * qw, qw), qw)
    qs = qb_ref[own, :]
    lane_blk = lax.broadcasted_iota(jnp.int32, (1, qw), 1) // blk

    def softmax_terms(st, gs, m_prev):
        segs = [st[i * blk:(i + 1) * blk] for i in range(len(gs))]
        m_new = m_prev
        for seg, g in zip(segs, gs):
            cand = jnp.max(seg, axis=0, keepdims=True) + g
            m_new = cand if m_new is None else jnp.maximum(m_new, cand)
        p = jnp.concatenate([jnp.exp(seg - (m_new - g)) for seg, g in zip(segs, gs)], axis=0)
        return m_new, p

    st = _nt_dot(kb_ref[own, :], qs) + biasd_ref[...]
    gs = []
    for i in range(qpt):
        g_blk = (slope * blk) * (qi * qpt + i).astype(F32)
        gs.append(jnp.where(lane_blk <= i, g_blk, g_ref[qi * qpt + i, 0:1, own]))
    m0, p = softmax_terms(st, gs, None)
    l0 = jnp.sum(p, axis=0, keepdims=True)
    acc0 = jnp.dot(vt_ref[:, own], p.astype(BF16), preferred_element_type=F32)

    n_past = qi * qpt

    def past_blocks(jp, carry):
        m, l, acc = carry
        rows = pl.ds(pl.multiple_of(jp * (kpi * blk), kpi * blk), kpi * blk)
        st = _nt_dot(kb_ref[rows, :], qs) + biasp_ref[...]
        gs = [jnp.where(jp * kpi + i < n_past, g_ref[jp * kpi + i, 0:1, own], NEG_INF) for i in range(kpi)]
        m_new, p = softmax_terms(st, gs, m)
        alpha = jnp.exp(m - m_new)
        l = alpha * l + jnp.sum(p, axis=0, keepdims=True)
        acc = alpha * acc + jnp.dot(vt_ref[:, rows], p.astype(BF16), preferred_element_type=F32)
        return m_new, l, acc

    _, l, acc = lax.fori_loop(0, (n_past + kpi - 1) // kpi, past_blocks, (m0, l0, acc0))
    ot = acc * (1.0 / l)
    ot = ot * lax.rsqrt(jnp.mean(ot * ot, axis=0, keepdims=True) + EPS)
    o_ref[0] = (ot.T * nrm_ref[...]).astype(o_ref.dtype)

    if_real(c0, finish_chunk, exact)
    if cps == 2:
        if_real(c0 + 2, start_chunk, False)
        if_real(c0 + 1, finish_chunk, exact)

    @pl.when(step == n_steps - 1)
    def _():
        out = pltpu.make_async_copy(kmbuf, kmean_ref, osem.at[0])
        out.start()
        out.wait()


def _moba_prompt(q, k, v, slopes, attn_norm, page_table, cache_k):
    b, t, _ = q.shape
    blk = MOBA_BLOCK
    nblk = t // blk
    assert t % blk == 0
    qpt = 2 if nblk % 2 == 0 else 1
    kpi = max(kk for kk in (4, 2, 1) if nblk % kk == 0)
    db, n_pages = page_table.shape
    _, page, nh, hd = cache_k.shape
    ppb = blk // page
    n_steps = b * N_HEADS * (nblk // qpt)
    pps = max(p for p in range(ppb, STREAM_PAGES + 1, ppb) if n_pages % p == 0)
    n_chunks = db * n_pages // pps
    cps = -(-n_chunks // n_steps)
    assert cps <= 2
    qw = qpt * blk
    seq = pl.BlockSpec((1, t, HEAD_DIM), lambda bi, h, qi, pt: (bi, 0, h))
    tile = pl.BlockSpec((1, qw, HEAD_DIM), lambda bi, h, qi, pt: (bi, qi, h))
    return pl.pallas_call(
        functools.partial(_moba_prompt_kernel, nblk=nblk, qpt=qpt, kpi=kpi, n_steps=n_steps, n_chunks=n_chunks,
                          cps=cps, pps=pps, n_pages=n_pages),
        grid_spec=pltpu.PrefetchScalarGridSpec(
            num_scalar_prefetch=1,
            grid=(b, N_HEADS, nblk // qpt),
            in_specs=[
                pl.BlockSpec(memory_space=pltpu.SMEM),
                seq, seq, seq,
                pl.BlockSpec((1, HEAD_DIM), lambda bi, h, qi, pt: (0, h)),
                pl.BlockSpec(memory_space=pl.ANY),
            ],
            out_specs=[tile, pl.BlockSpec(memory_space=pl.ANY)],
            scratch_shapes=[
                pltpu.VMEM((t, HEAD_DIM), BF16),
                pltpu.VMEM((t, HEAD_DIM), BF16),
                pltpu.VMEM((HEAD_DIM, t), BF16),
                pltpu.VMEM((nblk, SUBLANES, t), F32),
                pltpu.VMEM((kpi * blk, qw), F32),
                pltpu.VMEM((qw, qw), F32),
                pltpu.VMEM((2, pps, page, nh, hd), F32),
                pltpu.VMEM((db, n_pages // ppb, nh, hd), F32),
                pltpu.SemaphoreType.DMA((2,)),
                pltpu.SemaphoreType.DMA((1,)),
            ],
        ),
        out_shape=[jax.ShapeDtypeStruct((b, t, D_ATTN), BF16),
                   jax.ShapeDtypeStruct((db, n_pages // ppb, nh, hd), F32)],
        compiler_params=_cparams(("arbitrary", "arbitrary", "arbitrary")),
        name="moba_prompt",
    )(page_table.reshape(-1), slopes, q, k, v, attn_norm, cache_k)


def _sample_topk_kernel(q_ref, km_ref, idx_ref):
    nbf = km_ref.shape[1]
    s_len = q_ref.shape[2]
    gates = []
    for h in range(N_HEADS):
        km = jnp.concatenate([km_ref[0, :, h, :], jnp.zeros((LANES - nbf, HEAD_DIM), F32)], axis=0)
        gates.append(_nt_dot(q_ref[0, h], km, precision=lax.Precision.HIGHEST))
    gate = jnp.concatenate(gates, axis=0)
    lane = lax.broadcasted_iota(jnp.int32, gate.shape, 1)
    g = jnp.where(lane < nbf, gate, -jnp.inf)
    out = jnp.zeros(gate.shape, jnp.int32)
    for r in range(MOBA_TOPK):
        m = jnp.max(g, axis=-1, keepdims=True)
        idx = jnp.min(jnp.where(g == m, lane, LANES), axis=-1, keepdims=True)
        out = jnp.where(lane == r, idx, out)
        g = jnp.where(lane == idx, -jnp.inf, g)
    idx_ref[0] = out.reshape(N_HEADS, s_len, LANES)


def _sample_topk(q_t, kmean):
    db, nh, s, hd = q_t.shape
    nbf = kmean.shape[1]
    return pl.pallas_call(
        _sample_topk_kernel,
        grid=(db,),
        in_specs=[
            pl.BlockSpec((1, nh, s, hd), lambda i: (i, 0, 0, 0)),
            pl.BlockSpec((1, nbf, nh, hd), lambda i: (i, 0, 0, 0)),
        ],
        out_specs=pl.BlockSpec((1, nh, s, LANES), lambda i: (i, 0, 0, 0)),
        out_shape=jax.ShapeDtypeStruct((db, nh, s, LANES), jnp.int32),
        compiler_params=_cparams(("arbitrary",)),
        name="sample_topk",
    )(q_t, kmean)


def _moba_sample_kernel(pt_ref, blk_ref, slopes_ref, q_ref, kn_ref, vn_ref, nrm_ref,
                        glu_ref, w_ref, b_ref, gg_ref, gb_ref, ck_ref, cv_ref, o_ref, c_ref,
                        kbuf, vbuf, xs_ref, sem, *, n_units, n_sel, ppb, past, tt, rc, nt, cpu):
    page = ck_ref.shape[1]
    n_slab = n_sel * ppb
    n_keys = n_slab * page
    n_pages = past // page
    s_len = q_ref.shape[2]
    rows = 16

    def copies(u, slot, i, h, pg):
        dst = pl.ds(pl.multiple_of(i * page, page), page)
        return (pltpu.make_async_copy(ck_ref.at[pg, :, h, :], kbuf.at[slot, dst], sem.at[0, slot]),
                pltpu.make_async_copy(cv_ref.at[pg, :, h, :], vbuf.at[slot, dst], sem.at[1, slot]))

    def start(u, slot):
        h = u % N_HEADS
        row0 = (u // N_HEADS) * n_pages

        def sel(n, carry):
            logical = blk_ref[u * n_sel + n] * ppb
            for p in range(ppb):
                ck, cv = copies(u, slot, n * ppb + p, h, pt_ref[row0 + logical + p])
                ck.start()
                cv.start()
            return carry

        lax.fori_loop(0, n_sel, sel, 0)

    u = pl.program_id(0)

    @pl.when(u == 0)
    def _():
        start(0, 0)

    slot = u % 2
    bi = u // N_HEADS
    h = u % N_HEADS

    @pl.when(u + 1 < n_units)
    def _():
        start(u + 1, 1 - slot)

    for k in range(cpu):
        v = u * cpu + k
        cb = v % CONV_GROUPS
        tile = v // CONV_GROUPS

        @pl.when(cb == 0)
        def _():
            @pl.when(tile % nt == 0)
            def _():
                xs_ref[0, 0:HALO, :] = jnp.zeros((HALO, xs_ref.shape[2]), F32)

            xs_ref[0, HALO:HALO + tt, :] = glu_ref[0]
            _conv_preshift(xs_ref, tt)

        _conv_lane_block(xs_ref, w_ref, b_ref, gg_ref, gb_ref, c_ref, 0,
                         pl.ds(pl.multiple_of(cb * LANES, LANES), LANES), tt=tt, rc=rc, preshift=True)

        @pl.when(cb == CONV_GROUPS - 1)
        def _():
            xs_ref[0, 0:HALO, :] = xs_ref[0, tt:tt + HALO, :]

    def slab_done(i, carry):
        ck, cv = copies(u, slot, i, 0, 0)
        ck.wait()
        cv.wait()
        return carry

    lax.fori_loop(0, n_slab, slab_done, 0)

    slope = slopes_ref[h]
    q = q_ref[bi, h]
    qs = jnp.concatenate([q * (HEAD_DIM ** -0.5), jnp.zeros((rows - s_len, HEAD_DIM), F32)], axis=0)
    qs = qs.astype(BF16)
    kb = kbuf[slot].astype(BF16)
    vb = vbuf[slot].astype(BF16)
    s = _nt_dot(qs, kb)

    per_blk = page * ppb
    offs = lax.broadcasted_iota(jnp.int32, (1, per_blk), 1)
    kpos = jnp.concatenate(
        [(blk_ref[u * n_sel + i] * per_blk + offs).astype(F32) for i in range(n_sel)], axis=1)
    row = lax.broadcasted_iota(jnp.int32, (rows, 1), 0)
    qpos = (past + row).astype(F32)
    s = s - slope * (qpos - kpos)
    col = lax.broadcasted_iota(jnp.int32, (rows, n_keys), 1)
    per_q = MOBA_TOPK * per_blk
    mine = jnp.logical_and(col >= row * per_q, col < (row + 1) * per_q)
    s = jnp.where(mine, s, NEG_INF)

    kn = jnp.concatenate([kn_ref[bi, h], jnp.zeros((rows - s_len, HEAD_DIM), F32)], axis=0).astype(BF16)
    vn = jnp.concatenate([vn_ref[bi, h], jnp.zeros((rows - s_len, HEAD_DIM), F32)], axis=0).astype(BF16)
    ri = lax.broadcasted_iota(jnp.int32, (rows, rows), 0)
    ci = lax.broadcasted_iota(jnp.int32, (rows, rows), 1)
    so = _nt_dot(qs, kn) - slope * (ri - ci).astype(F32)
    so = jnp.where(jnp.logical_and(ri >= ci, ci < s_len), so, NEG_INF)

    m = jnp.maximum(jnp.max(s, axis=-1, keepdims=True), jnp.max(so, axis=-1, keepdims=True))
    p = jnp.exp(s - m)
    po = jnp.exp(so - m)
    l = jnp.sum(p, axis=-1, keepdims=True) + jnp.sum(po, axis=-1, keepdims=True)
    acc = (jnp.dot(p.astype(BF16), vb, preferred_element_type=F32)
           + jnp.dot(po.astype(BF16), vn, preferred_element_type=F32))
    o = (acc / l)[0:s_len]
    o = (o * lax.rsqrt(jnp.mean(o * o, axis=-1, keepdims=True) + EPS)) * nrm_ref[pl.ds(h, 1), :]
    o_ref[bi, h] = o


def _moba_sample(page_table_flat, blks, slopes, q_t, kn_t, vn_t, attn_norm_hd, cache_k, cache_v, past,
                 glu, w_dw, b_dw, gn_g, gn_b, tt):
    db, nh, s, hd = q_t.shape
    _, page, _, _ = cache_k.shape
    ppb = MOBA_BLOCK // page
    n_sel = s * MOBA_TOPK
    n_units = db * nh
    b, t, dc = glu.shape
    assert dc == CONV_GROUPS * LANES and t % tt == 0 and tt >= 4 * SUBLANES
    nt = t // tt
    n_conv_units = b * nt * CONV_GROUPS
    cpu = -(-n_conv_units // n_units)
    assert CONV_GROUPS % cpu == 0 and cpu * n_units == n_conv_units
    whole = lambda shp: pl.BlockSpec(shp, lambda i, *_: (0,) * len(shp))
    tile = pl.BlockSpec((1, tt, dc), lambda i, *_: ((i * cpu // CONV_GROUPS) // nt, (i * cpu // CONV_GROUPS) % nt, 0))
    return pl.pallas_call(
        functools.partial(_moba_sample_kernel, n_units=n_units, n_sel=n_sel, ppb=ppb, past=past,
                          tt=tt, rc=min(tt, 64), nt=nt, cpu=cpu),
        grid_spec=pltpu.PrefetchScalarGridSpec(
            num_scalar_prefetch=3,
            grid=(n_units,),
            in_specs=[
                whole(q_t.shape), whole(kn_t.shape), whole(vn_t.shape), whole(attn_norm_hd.shape),
                tile, whole(w_dw.shape), whole(b_dw.shape), whole(gn_g.shape), whole(gn_b.shape),
                pl.BlockSpec(memory_space=pl.ANY),
                pl.BlockSpec(memory_space=pl.ANY),
            ],
            out_specs=[whole(q_t.shape), tile],
            scratch_shapes=[
                pltpu.VMEM((2, n_sel * ppb * page, hd), F32),
                pltpu.VMEM((2, n_sel * ppb * page, hd), F32),
                pltpu.VMEM((SUBLANES, HALO + tt, dc), F32),
                pltpu.SemaphoreType.DMA((2, 2)),
            ],
        ),
        out_shape=[jax.ShapeDtypeStruct(q_t.shape, F32), jax.ShapeDtypeStruct((b, t, dc), BF16)],
        compiler_params=_cparams(("arbitrary",)),
        name="moba_sample_conv",
    )(page_table_flat, blks, slopes, q_t, kn_t, vn_t, attn_norm_hd, glu, w_dw, b_dw, gn_g, gn_b, cache_k, cache_v)


def _outproj_kernel(op_ref, cp_ref, xp_ref, os_ref, cs_ref, xs_ref, wo_ref, wc_ref, g_ref, wr_ref, br_ref,
                    x2_ref, h2_ref, route_ref, *, nb_p):
    i = pl.program_id(0)
    tail = (wo_ref, wc_ref, g_ref, wr_ref, br_ref, x2_ref, h2_ref, route_ref)

    @pl.when(i < nb_p)
    def _():
        _outproj_tile(op_ref, cp_ref, xp_ref, *tail)

    @pl.when(i >= nb_p)
    def _():
        _outproj_tile(os_ref, cs_ref, xs_ref, *tail)


def _outproj_tile(o_ref, c_ref, x_ref, wo_ref, wc_ref, g_ref, wr_ref, br_ref, x2_ref, h2_ref, route_ref):
    mix = (jnp.dot(o_ref[...].astype(BF16), wo_ref[...], preferred_element_type=F32)
           + jnp.dot(c_ref[...].astype(BF16), wc_ref[...], preferred_element_type=F32))
    x2 = x_ref[...] + mix
    x2_ref[...] = x2
    r = lax.rsqrt(jnp.mean(x2 * x2, axis=-1, keepdims=True) + EPS)
    h2 = (x2 * r) * g_ref[...]
    h2_ref[...] = h2

    h_hi = h2.astype(BF16)
    h_lo = (h2 - h_hi.astype(F32)).astype(BF16)
    r1 = jnp.dot(h_hi, wr_ref[...], preferred_element_type=F32)
    r2 = jnp.dot(h_lo, wr_ref[:, 0:LANES], preferred_element_type=F32)
    logits = r1[:, 0:LANES] + r1[:, LANES:2 * LANES] + r2 + br_ref[...]

    lane = lax.broadcasted_iota(jnp.int32, logits.shape, 1)
    is_g = lane < N_EXPERT_GROUPS
    gl = jnp.where(is_g, logits, -jnp.inf)
    gmax = jnp.max(gl, axis=-1, keepdims=True)
    gidx = jnp.min(jnp.where(gl == gmax, lane, LANES), axis=-1, keepdims=True)
    gsum = jnp.sum(jnp.where(is_g, jnp.exp(gl - gmax), 0.0), axis=-1, keepdims=True)
    g_gate = 1.0 / gsum

    lo = N_EXPERT_GROUPS + gidx * EXPERTS_PER_GROUP
    el = jnp.where(jnp.logical_and(lane >= lo, lane < lo + EXPERTS_PER_GROUP), logits, -jnp.inf)
    v0 = jnp.max(el, axis=-1, keepdims=True)
    i0 = jnp.min(jnp.where(el == v0, lane, LANES), axis=-1, keepdims=True)
    el = jnp.where(lane == i0, -jnp.inf, el)
    v1 = jnp.max(el, axis=-1, keepdims=True)
    i1 = jnp.min(jnp.where(el == v1, lane, LANES), axis=-1, keepdims=True)
    e = jnp.exp(v1 - v0)
    w0 = g_gate / (1.0 + e)
    w1 = g_gate * e / (1.0 + e)

    out = jnp.where(lane == 0, (i0 - N_EXPERT_GROUPS).astype(F32), 0.0)
    out = jnp.where(lane == 1, (i1 - N_EXPERT_GROUPS).astype(F32), out)
    out = jnp.where(lane == 2, w0, out)
    out = jnp.where(lane == 3, w1, out)
    route_ref[...] = out


def _out_projection(o_p, c_p, x_p, o_s, c_s, x_s, wo, wc, norm_g, wr, br, tm):
    np_, d = x_p.shape
    ns_ = x_s.shape[0]
    assert np_ % tm == 0 and ns_ % tm == 0
    nb_p, nb_s = np_ // tm, ns_ // tm
    n = np_ + ns_
    prow = lambda w: pl.BlockSpec((tm, w), lambda i: (jnp.minimum(i, nb_p - 1), 0))
    srow = lambda w: pl.BlockSpec((tm, w), lambda i: (jnp.maximum(i - nb_p, 0), 0))
    row = lambda w: pl.BlockSpec((tm, w), lambda i: (i, 0))
    full = lambda a: pl.BlockSpec(a.shape, lambda i: (0, 0))
    return pl.pallas_call(
        functools.partial(_outproj_kernel, nb_p=nb_p),
        grid=(nb_p + nb_s,),
        in_specs=[prow(o_p.shape[1]), prow(c_p.shape[1]), prow(d), srow(o_s.shape[1]), srow(c_s.shape[1]), srow(d),
                  full(wo), full(wc), full(norm_g), full(wr), full(br)],
        out_specs=[row(d), row(d), row(LANES)],
        out_shape=[jax.ShapeDtypeStruct((n, d), F32), jax.ShapeDtypeStruct((n, d), F32),
                   jax.ShapeDtypeStruct((n, LANES), F32)],
        compiler_params=_cparams(("arbitrary",)),
        name="out_projection_router",
    )(o_p, c_p, x_p, o_s, c_s, x_s, wo, wc, norm_g, wr, br)


DMA_UNROLL = 8


def _for_rows(n, fn):
    def group(gi, carry):
        for k in range(DMA_UNROLL):
            fn(gi * DMA_UNROLL + k)
        return carry

    def single(i, carry):
        fn(i)
        return carry

    n_grp = lax.div(n, DMA_UNROLL)
    lax.fori_loop(0, n_grp, group, 0)
    lax.fori_loop(n_grp * DMA_UNROLL, n, single, 0)


def _moe_kernel(order_ref, cexp_ref, cbase_ref, cnv_ref, h_ref, wg_ref, wu_ref, wd_ref, out_ref,
                xbuf, obuf, wgb, wub, wdb, gsem, ssem, *, n_tok, n_chunks):
    g = pl.program_id(0)
    slot = g % 2

    def gather(chunk, buf_slot):
        base = cbase_ref[chunk]

        def desc(i):
            tok = lax.shift_right_logical(order_ref[base + i], 1)
            return pltpu.make_async_copy(h_ref.at[pl.ds(tok, 1)], xbuf.at[buf_slot, pl.ds(i, 1)], gsem.at[buf_slot])
        return desc

    def scatter(chunk, buf_slot):
        base = cbase_ref[chunk]

        def desc(i):
            pair = order_ref[base + i]
            row = (pair & 1) * n_tok + lax.shift_right_logical(pair, 1)
            return pltpu.make_async_copy(obuf.at[buf_slot, pl.ds(i, 1)], out_ref.at[pl.ds(row, 1)], ssem.at[buf_slot])
        return desc

    def gather_done(buf_slot):
        pltpu.make_async_copy(h_ref.at[pl.ds(0, 1)], xbuf.at[buf_slot, pl.ds(0, 1)], gsem.at[buf_slot]).wait()

    def scatter_done(buf_slot):
        pltpu.make_async_copy(obuf.at[buf_slot, pl.ds(0, 1)], out_ref.at[pl.ds(0, 1)], ssem.at[buf_slot]).wait()

    @pl.when(g == 0)
    def _():
        xbuf[...] = jnp.zeros(xbuf.shape, xbuf.dtype)
        _for_rows(cnv_ref[0], lambda i: gather(0, 0)(i).start())

    @pl.when(jnp.logical_or(g == 0, cexp_ref[g] != cexp_ref[jnp.maximum(g - 1, 0)]))
    def _():
        wgb[...] = wg_ref[0].astype(BF16)
        wub[...] = wu_ref[0].astype(BF16)
        wdb[...] = wd_ref[0].astype(BF16)

    @pl.when(g + 1 < n_chunks)
    def _():
        _for_rows(cnv_ref[g + 1], lambda i: gather(g + 1, 1 - slot)(i).start())

    nv = cnv_ref[g]
    _for_rows(nv, lambda i: gather_done(slot))

    @pl.when(g >= 2)
    def _():
        _for_rows(cnv_ref[g - 2], lambda i: scatter_done(slot))

    @pl.when(nv > 0)
    def _():
        x = xbuf[slot].astype(BF16)
        gt = jnp.dot(x, wgb[...], preferred_element_type=F32)
        up = jnp.dot(x, wub[...], preferred_element_type=F32)
        hid = (gt * (1.0 / (1.0 + jnp.exp(-gt)))) * up
        obuf[slot] = jnp.dot(hid.astype(BF16), wdb[...], preferred_element_type=F32)

    _for_rows(nv, lambda i: scatter(g, slot)(i).start())

    @pl.when(g == n_chunks - 1)
    def _():
        if n_chunks >= 2:
            _for_rows(cnv_ref[g - 1], lambda i: scatter_done(1 - slot))
        _for_rows(nv, lambda i: scatter_done(slot))


def _moe(order, cexp, cbase, cnv, h2, w_gate, w_up, w_down, ch):
    n, d = h2.shape
    ne, _, de = w_gate.shape
    n_chunks = cexp.shape[0]
    wspec = lambda shp: pl.BlockSpec((1,) + shp, lambda g, order, cexp, cbase, cnv: (cexp[g], 0, 0))
    return pl.pallas_call(
        functools.partial(_moe_kernel, n_tok=n, n_chunks=n_chunks),
        grid_spec=pltpu.PrefetchScalarGridSpec(
            num_scalar_prefetch=4,
            grid=(n_chunks,),
            in_specs=[
                pl.BlockSpec(memory_space=pl.ANY),
                wspec((d, de)), wspec((d, de)), wspec((de, d)),
            ],
            out_specs=pl.BlockSpec(memory_space=pl.ANY),
            scratch_shapes=[
                pltpu.VMEM((2, ch, d), F32),
                pltpu.VMEM((2, ch, d), F32),
                pltpu.VMEM((d, de), BF16),
                pltpu.VMEM((d, de), BF16),
                pltpu.VMEM((de, d), BF16),
                pltpu.SemaphoreType.DMA((2,)),
                pltpu.SemaphoreType.DMA((2,)),
            ],
        ),
        out_shape=jax.ShapeDtypeStruct((2 * n, d), F32),
        compiler_params=_cparams(("arbitrary",)),
        name="expert_mlp",
    )(order, cexp, cbase, cnv, h2, w_gate, w_up, w_down)


def _expert_chunks(eid, ch):
    n_pairs = eid.shape[0]
    n_chunks = N_EXPERTS + n_pairs // ch
    order = jnp.argsort(eid, stable=True).astype(jnp.int32)
    counts = jnp.sum(eid[:, None] == jnp.arange(N_EXPERTS, dtype=jnp.int32)[None, :], axis=0, dtype=jnp.int32)
    offs = jnp.cumsum(counts, dtype=jnp.int32) - counts
    nch = (counts + ch - 1) // ch
    cum = jnp.cumsum(nch, dtype=jnp.int32)
    gidx = jnp.arange(n_chunks, dtype=jnp.int32)
    cexp_raw = jnp.sum(gidx[:, None] >= cum[None, :], axis=1, dtype=jnp.int32)
    last = jnp.maximum(jnp.max(jnp.where(nch > 0, jnp.arange(N_EXPERTS, dtype=jnp.int32), 0)), 0)
    used = gidx < cum[-1]
    cexp = jnp.where(used, jnp.minimum(cexp_raw, N_EXPERTS - 1), last)
    local = gidx - (cum[cexp] - nch[cexp])
    cbase = jnp.where(used, offs[cexp] + local * ch, 0)
    cnv = jnp.where(used, jnp.clip(counts[cexp] - local * ch, 0, ch), 0)
    return order, cexp.astype(jnp.int32), cbase.astype(jnp.int32), cnv.astype(jnp.int32)


def _final_kernel(x2_ref, ya_ref, yb_ref, route_ref, g_ref, yp_ref, ys_ref, *, nb_p):
    i = pl.program_id(0)
    rt = route_ref[...]
    x3 = x2_ref[...] + rt[:, 2:3] * ya_ref[...] + rt[:, 3:4] * yb_ref[...]
    r = lax.rsqrt(jnp.mean(x3 * x3, axis=-1, keepdims=True) + EPS)
    y = (x3 * r) * g_ref[...]

    @pl.when(i < nb_p)
    def _():
        yp_ref[...] = y

    @pl.when(i >= nb_p)
    def _():
        ys_ref[...] = y


def _final(x2, y2, route, norm_g, np_, tm):
    n, d = x2.shape
    assert np_ % tm == 0 and n % tm == 0
    nb, nb_p = n // tm, np_ // tm
    return pl.pallas_call(
        functools.partial(_final_kernel, nb_p=nb_p),
        grid=(nb,),
        in_specs=[
            pl.BlockSpec((tm, d), lambda i: (i, 0)),
            pl.BlockSpec((tm, d), lambda i: (i, 0)),
            pl.BlockSpec((tm, d), lambda i: (i + nb, 0)),
            pl.BlockSpec((tm, LANES), lambda i: (i, 0)),
            pl.BlockSpec((1, d), lambda i: (0, 0)),
        ],
        out_specs=[
            pl.BlockSpec((tm, d), lambda i: (jnp.minimum(i, nb_p - 1), 0)),
            pl.BlockSpec((tm, d), lambda i: (jnp.maximum(i - nb_p, 0), 0)),
        ],
        out_shape=[jax.ShapeDtypeStruct((np_, d), F32), jax.ShapeDtypeStruct((n - np_, d), F32)],
        compiler_params=_cparams(("arbitrary",)),
        name="combine_final_norm",
    )(x2, y2, y2, route, norm_g)


def _pick_tile(n, pref):
    t = min(n, pref)
    while n % t:
        t //= 2
    return t


def kernel(x_prompt, x_sample, cache_k, cache_v, state_conv, page_table, norm_mix, w_in, w_dw, b_dw,
           conv_norm_g, conv_norm_b, attn_out_norm, w_out, norm_ffn, w_router_group, b_router_group,
           w_router_expert, b_router_expert, w_gate, w_up, w_down, norm_final):
    depth = norm_mix.shape[0]
    assert depth == 1
    b, t, d = x_prompt.shape
    db, s, _ = x_sample.shape
    n_pages = page_table.shape[1]
    page = cache_k.shape[2]
    past = n_pages * page
    nbf = past // MOBA_BLOCK
    assert past % MOBA_BLOCK == 0 and nbf >= MOBA_TOPK and nbf <= LANES
    d_conv = d - D_ATTN
    assert d_conv == D_ATTN
    np_, ns_ = b * t, db * s

    slopes = jnp.exp2(-8.0 * jnp.arange(1, N_HEADS + 1, dtype=F32) / N_HEADS)
    w_in_bf = w_in[0].astype(BF16)
    wo_bf = w_out[0, :D_ATTN].astype(BF16)
    wc_bf = w_out[0, D_ATTN:].astype(BF16)
    wr = jnp.zeros((d, LANES), F32)
    wr = wr.at[:, :N_EXPERT_GROUPS].set(w_router_group[0])
    wr = wr.at[:, N_EXPERT_GROUPS:N_EXPERT_GROUPS + N_EXPERTS].set(w_router_expert[0])
    wr_hi = wr.astype(BF16)
    wr_lo = (wr - wr_hi.astype(F32)).astype(BF16)
    wr_cat = jnp.concatenate([wr_hi, wr_lo], axis=1)
    br = jnp.zeros((1, LANES), F32)
    br = br.at[0, :N_EXPERT_GROUPS].set(b_router_group[0])
    br = br.at[0, N_EXPERT_GROUPS:N_EXPERT_GROUPS + N_EXPERTS].set(b_router_expert[0])
    g_mix = norm_mix[0].reshape(1, d)
    g_ffn = norm_ffn[0].reshape(1, d)
    g_fin = norm_final.reshape(1, d)
    attn_norm = attn_out_norm[0].reshape(1, D_ATTN)
    conv_vec = lambda a: a[0].reshape(1, d_conv)

    xp = x_prompt.reshape(np_, d)
    xs = x_sample.reshape(ns_, d)
    qp, kp, vp, glu_p = _in_projection(xp, g_mix, w_in_bf, _pick_tile(np_, 512))
    qs, ks, vs, glu_s = _in_projection(xs, g_mix, w_in_bf, _pick_tile(ns_, 256))

    glu_p3 = glu_p.reshape(b, t, d_conv)
    glu_s3 = glu_s.reshape(db, s, d_conv)
    conv_args = (w_dw[0], conv_vec(b_dw), conv_vec(conv_norm_g), conv_vec(conv_norm_b))
    c_s = _conformer_conv(glu_s3, state_conv[0], *conv_args, tt=s, nseq=db, out_dtype=F32)
    conv_prompt = glu_p3[:, t - (CONV_WIDTH - 1):][None]
    conv_sample = jnp.concatenate([state_conv[0], glu_s3], axis=1)[:, -(CONV_WIDTH - 1):][None]

    ck = cache_k[0]
    cv = cache_v[0]
    o_p, kmean = _moba_prompt(qp.reshape(b, t, D_ATTN), kp.reshape(b, t, D_ATTN), vp.reshape(b, t, D_ATTN),
                              slopes, attn_norm, page_table, ck)

    to_heads = lambda a: a.reshape(db, s, N_HEADS, HEAD_DIM).transpose(0, 2, 1, 3)
    q_t, kn_t, vn_t = to_heads(qs), to_heads(ks), to_heads(vs)
    idx = _sample_topk(q_t, kmean)[..., :MOBA_TOPK]
    o_t, c_p = _moba_sample(page_table.reshape(-1), idx.reshape(-1), slopes, q_t, kn_t, vn_t,
                            attn_norm.reshape(N_HEADS, HEAD_DIM), ck, cv, past,
                            glu_p3, *conv_args, tt=_pick_tile(t, 256))
    o_s = o_t.transpose(0, 2, 1, 3).reshape(ns_, D_ATTN)

    tm = _pick_tile(ns_, 256)
    x2, h2, route = _out_projection(o_p.reshape(np_, D_ATTN), c_p.reshape(np_, d_conv), xp,
                                    o_s, c_s.reshape(ns_, d_conv), xs,
                                    wo_bf, wc_bf, g_ffn, wr_cat, br, tm)

    ch = 256
    eid = route[:, 0:2].astype(jnp.int32).reshape(-1)
    order, cexp, cbase, cnv = _expert_chunks(eid, ch)
    y2 = _moe(order, cexp, cbase, cnv, h2, w_gate[0], w_up[0], w_down[0], ch)

    y_p, y_s = _final(x2, y2, route, g_fin, np_, tm)

    page_shape = (depth, b, t // page, page, N_HEADS, HEAD_DIM)
    return (y_p.reshape(b, t, d), y_s.reshape(db, s, d),
            kp.reshape(page_shape), vp.reshape(page_shape), conv_prompt,
            ks.reshape(depth, db, s, N_HEADS, HEAD_DIM), vs.reshape(depth, db, s, N_HEADS, HEAD_DIM),
            conv_sample)
```

```python
import functools

import jax
import jax.numpy as jnp
from jax import lax
from jax.experimental import pallas as pl
from jax.experimental.pallas import tpu as pltpu

F32 = jnp.float32
BF16 = jnp.bfloat16

N_HEADS = 8
HEAD_DIM = 128
D_ATTN = N_HEADS * HEAD_DIM
CONV_GROUPS = 8
CONV_WIDTH = 31
MOBA_BLOCK = 256
MOBA_TOPK = 3
N_EXPERT_GROUPS = 4
EXPERTS_PER_GROUP = 8
N_EXPERTS = N_EXPERT_GROUPS * EXPERTS_PER_GROUP
EPS = 1e-6
NEG_INF = -1e30

LANES = 128
SUBLANES = 8
HALO = 32
VMEM_LIMIT = 56 * 1024 * 1024
STREAM_PAGES = 16


def _cparams(sem=None):
    return pltpu.CompilerParams(dimension_semantics=sem, vmem_limit_bytes=VMEM_LIMIT)


def _nt_dot(a, b, precision=None):
    return lax.dot_general(a, b, (((1,), (1,)), ((), ())), precision=precision,
                           preferred_element_type=F32)


def _inproj_kernel(x_ref, g_ref, w_ref, wgate_ref, q_ref, k_ref, v_ref, glu_ref, h_ref):
    j = pl.program_id(1)

    @pl.when(j == 0)
    def _():
        x = x_ref[...]
        r = lax.rsqrt(jnp.mean(x * x, axis=-1, keepdims=True) + EPS)
        h_ref[...] = ((x * r) * g_ref[...]).astype(BF16)

    z = jnp.dot(h_ref[...], w_ref[...], preferred_element_type=F32)

    @pl.when(j == 0)
    def _():
        q_ref[...] = z

    @pl.when(j == 1)
    def _():
        k_ref[...] = z

    @pl.when(j == 2)
    def _():
        v_ref[...] = z

    @pl.when(j == 3)
    def _():
        gate = jnp.dot(h_ref[...], wgate_ref[...], preferred_element_type=F32)
        glu_ref[...] = z * (1.0 / (1.0 + jnp.exp(-gate)))


def _in_projection(x, norm_g, w_in_bf, tm):
    n, d = x.shape
    dc = D_ATTN
    out = jax.ShapeDtypeStruct((n, dc), F32)
    row = pl.BlockSpec((tm, dc), lambda i, j: (i, 0))
    return pl.pallas_call(
        _inproj_kernel,
        grid=(n // tm, 4),
        in_specs=[
            pl.BlockSpec((tm, d), lambda i, j: (i, 0)),
            pl.BlockSpec((1, d), lambda i, j: (0, 0)),
            pl.BlockSpec((d, dc), lambda i, j: (0, j)),
            pl.BlockSpec((d, dc), lambda i, j: (0, 4)),
        ],
        out_specs=[row, row, row, row],
        out_shape=[out, out, out, out],
        scratch_shapes=[pltpu.VMEM((tm, d), BF16)],
        compiler_params=_cparams(("arbitrary", "arbitrary")),
        name="in_projection",
    )(x, norm_g, w_in_bf, w_in_bf)


def _conv_lane_block(xs_ref, w_ref, b_ref, gg_ref, gb_ref, c_ref, sq, cs, *, tt, rc, preshift):
    off = HALO - (CONV_WIDTH - 1)
    for r0 in range(0, tt, rc):
        acc = jnp.broadcast_to(b_ref[:, cs], (rc, LANES))
        for j in range(CONV_WIDTH):
            o = off + j
            if preshift:
                lo = r0 + (o // SUBLANES) * SUBLANES
                x = xs_ref[o % SUBLANES, lo:lo + rc, cs]
            else:
                x = xs_ref[0, r0 + o:r0 + o + rc, cs]
            acc = acc + x * w_ref[j:j + 1, cs]
        mu = jnp.mean(acc, axis=-1, keepdims=True)
        dlt = acc - mu
        var = jnp.mean(dlt * dlt, axis=-1, keepdims=True)
        yn = dlt * lax.rsqrt(var + EPS)
        yn = yn * gg_ref[:, cs] + gb_ref[:, cs]
        c_ref[sq, r0:r0 + rc, cs] = (yn * (1.0 / (1.0 + jnp.exp(-yn)))).astype(c_ref.dtype)


def _conv_preshift(xs_ref, tt):
    span = HALO + tt - SUBLANES
    for r in range(1, SUBLANES):
        xs_ref[r, 0:span, :] = xs_ref[0, r:r + span, :]


def _conv_kernel(glu_ref, st_ref, w_ref, b_ref, gg_ref, gb_ref, c_ref, xs_ref, *, tt, rc, nseq, carry, preshift):
    t = pl.program_id(1)

    def lane_block(sq, cs):
        _conv_lane_block(xs_ref, w_ref, b_ref, gg_ref, gb_ref, c_ref, sq, cs, tt=tt, rc=rc, preshift=preshift)

    hist = HALO - (CONV_WIDTH - 1)

    def load_state(sq):
        xs_ref[0, 0:hist, :] = jnp.zeros((hist, xs_ref.shape[2]), F32)
        xs_ref[0, hist:HALO, :] = st_ref[sq]

    def one_seq(sq):
        if carry:
            pl.when(t == 0)(functools.partial(load_state, sq))
        else:
            load_state(sq)
        xs_ref[0, HALO:HALO + tt, :] = glu_ref[sq]
        if preshift:
            _conv_preshift(xs_ref, tt)

            def group(cb, carry_):
                lane_block(sq, pl.ds(pl.multiple_of(cb * LANES, LANES), LANES))
                return carry_

            lax.fori_loop(0, CONV_GROUPS, group, 0)
        else:
            for cb in range(CONV_GROUPS):
                lane_block(sq, pl.ds(cb * LANES, LANES))
        if carry:
            xs_ref[0, 0:HALO, :] = xs_ref[0, tt:tt + HALO, :]

    if nseq == 1:
        one_seq(0)
    else:
        def seq(sq, carry_):
            one_seq(sq)
            return carry_

        lax.fori_loop(0, nseq, seq, 0)


def _conformer_conv(glu, state, w_dw, b_dw, gn_g, gn_b, tt, nseq, out_dtype):
    b, t, dc = glu.shape
    assert dc == CONV_GROUPS * LANES and b % nseq == 0 and t % tt == 0 and state.shape[1] == CONV_WIDTH - 1
    nt = t // tt
    assert nt == 1 or nseq == 1
    preshift = tt >= 4 * SUBLANES
    rc = min(tt, 64)
    vec = pl.BlockSpec((1, dc), lambda i, j: (0, 0))
    return pl.pallas_call(
        functools.partial(_conv_kernel, tt=tt, rc=rc, nseq=nseq, carry=nt > 1, preshift=preshift),
        grid=(b // nseq, nt),
        in_specs=[
            pl.BlockSpec((nseq, tt, dc), lambda i, j: (i, j, 0)),
            pl.BlockSpec((nseq, CONV_WIDTH - 1, dc), lambda i, j: (i, 0, 0)),
            pl.BlockSpec((CONV_WIDTH, dc), lambda i, j: (0, 0)),
            vec, vec, vec,
        ],
        out_specs=pl.BlockSpec((nseq, tt, dc), lambda i, j: (i, j, 0)),
        out_shape=jax.ShapeDtypeStruct((b, t, dc), out_dtype),
        scratch_shapes=[pltpu.VMEM((SUBLANES if preshift else 1, HALO + tt, dc), F32)],
        compiler_params=_cparams(("arbitrary", "arbitrary")),
        name="conformer_conv",
    )(glu, state, w_dw, b_dw, gn_g, gn_b)


def _moba_prompt_kernel(pt_ref, slopes_ref, q_ref, k_ref, v_ref, nrm_ref, ck_ref, o_ref, kmean_ref,
                        qb_ref, kb_ref, vt_ref, g_ref, biasp_ref, biasd_ref, pbuf, kmbuf, psem, osem,
                        *, nblk, qpt, kpi, n_steps, n_chunks, cps, pps, n_pages):
    h = pl.program_id(1)
    qi = pl.program_id(2)
    blk = MOBA_BLOCK
    qw = qpt * blk
    t = q_ref.shape[1]
    nb8 = -(-nblk // SUBLANES) * SUBLANES
    slope = slopes_ref[h]

    step = (pl.program_id(0) * N_HEADS + h) * (nblk // qpt) + qi
    c0 = step * cps
    exact = n_chunks == cps * n_steps
    ppb = blk // pbuf.shape[2]

    def start_chunk(chunk):
        slot = chunk % 2
        for i in range(pps):
            pltpu.make_async_copy(ck_ref.at[pt_ref[chunk * pps + i]], pbuf.at[slot, i], psem.at[slot]).start()

    def finish_chunk(chunk):
        slot = chunk % 2
        for i in range(pps):
            pltpu.make_async_copy(ck_ref.at[0], pbuf.at[slot, i], psem.at[slot]).wait()
        bi = (chunk * pps) // n_pages
        blk0 = ((chunk * pps) % n_pages) // ppb
        for i in range(pps // ppb):
            s = jnp.sum(pbuf[slot, ppb * i], axis=0)
            for p in range(1, ppb):
                s = s + jnp.sum(pbuf[slot, ppb * i + p], axis=0)
            kmbuf[bi, blk0 + i] = s * (1.0 / blk)

    def if_real(chunk, fn, always):
        if always:
            fn(chunk)
        else:
            pl.when(chunk < n_chunks)(functools.partial(fn, chunk))

    @pl.when(step == 0)
    def _():
        start_chunk(0)

    if_real(c0 + 1, start_chunk, exact and cps == 2)

    @pl.when(qi == 0)
    def _():
        q = q_ref[0]
        kf = k_ref[0]
        qb_ref[...] = (q * (HEAD_DIM ** -0.5)).astype(BF16)
        kb_ref[...] = kf.astype(BF16)
        vt_ref[...] = v_ref[0].T.astype(BF16)
        km = jnp.sum(kf.reshape(nblk, blk, HEAD_DIM), axis=1) * (1.0 / blk)
        if nb8 > nblk:
            km = jnp.concatenate([km, jnp.zeros((nb8 - nblk, HEAD_DIM), F32)], axis=0)
        gate = _nt_dot(km, q, precision=lax.Precision.HIGHEST)
        j_io = lax.broadcasted_iota(jnp.int32, gate.shape, 0)
        t_io = lax.broadcasted_iota(jnp.int32, gate.shape, 1)
        valid = j_io * blk + (blk - 1) < t_io - (t_io % blk)
        g = jnp.where(valid, gate, NEG_INF)
        sel = jnp.zeros(gate.shape, jnp.bool_)
        for _ in range(MOBA_TOPK):
            m = jnp.max(g, axis=0, keepdims=True)
            idx = jnp.min(jnp.where(g == m, j_io, nb8), axis=0, keepdims=True)
            pick = j_io == idx
            sel = jnp.logical_or(sel, pick)
            g = jnp.where(pick, -jnp.inf, g)
        gb = (slope * blk) * j_io.astype(F32) + jnp.where(jnp.logical_and(sel, valid), 0.0, NEG_INF)
        for j in range(nblk):
            g_ref[j] = jnp.broadcast_to(gb[j:j + 1, :], (SUBLANES, t))
        c = lax.broadcasted_iota(jnp.int32, (kpi * blk, qw), 0)
        biasp_ref[...] = slope * (c % blk).astype(F32)
        c = lax.broadcasted_iota(jnp.int32, (qw, qw), 0)
        r = lax.broadcasted_iota(jnp.int32, (qw, qw), 1)
        hidden = jnp.logical_or(c // blk > r // blk, jnp.logical_and(c // blk == r // blk, c % blk > r % blk))
        biasd_ref[...] = slope * (c % blk).astype(F32) + jnp.where(hidden, NEG_INF, 0.0)

    own = pl.ds(pl.multiple_of(qi * qw, qw), qw)
    qs = qb_ref[own, :]
    lane_blk = lax.broadcasted_iota(jnp.int32, (1, qw), 1) // blk

    def softmax_terms(st, gs, m_prev):
        segs = [st[i * blk:(i + 1) * blk] for i in range(len(gs))]
        m_new = m_prev
        for seg, g in zip(segs, gs):
            cand = jnp.max(seg, axis=0, keepdims=True) + g
            m_new = cand if m_new is None else jnp.maximum(m_new, cand)
        p = jnp.concatenate([jnp.exp(seg - (m_new - g)) for seg, g in zip(segs, gs)], axis=0)
        return m_new, p

    st = _nt_dot(kb_ref[own, :], qs) + biasd_ref[...]
    gs = []
    for i in range(qpt):
        g_blk = (slope * blk) * (qi * qpt + i).astype(F32)
        gs.append(jnp.where(lane_blk <= i, g_blk, g_ref[qi * qpt + i, 0:1, own]))
    m0, p = softmax_terms(st, gs, None)
    l0 = jnp.sum(p, axis=0, keepdims=True)
    acc0 = jnp.dot(vt_ref[:, own], p.astype(BF16), preferred_element_type=F32)

    n_past = qi * qpt

    def past_blocks(jp, carry):
        m, l, acc = carry
        rows = pl.ds(pl.multiple_of(jp * (kpi * blk), kpi * blk), kpi * blk)
        st = _nt_dot(kb_ref[rows, :], qs) + biasp_ref[...]
        gs = [jnp.where(jp * kpi + i < n_past, g_ref[jp * kpi + i, 0:1, own], NEG_INF) for i in range(kpi)]
        m_new, p = softmax_terms(st, gs, m)
        alpha = jnp.exp(m - m_new)
        l = alpha * l + jnp.sum(p, axis=0, keepdims=True)
        acc = alpha * acc + jnp.dot(vt_ref[:, rows], p.astype(BF16), preferred_element_type=F32)
        return m_new, l, acc

    _, l, acc = lax.fori_loop(0, (n_past + kpi - 1) // kpi, past_blocks, (m0, l0, acc0))
    ot = acc * (1.0 / l)
    ot = ot * lax.rsqrt(jnp.mean(ot * ot, axis=0, keepdims=True) + EPS)
    o_ref[0] = (ot.T * nrm_ref[...]).astype(o_ref.dtype)

    if_real(c0, finish_chunk, exact)
    if cps == 2:
        if_real(c0 + 2, start_chunk, False)
        if_real(c0 + 1, finish_chunk, exact)

    @pl.when(step == n_steps - 1)
    def _():
        out = pltpu.make_async_copy(kmbuf, kmean_ref, osem.at[0])
        out.start()
        out.wait()


def _moba_prompt(q, k, v, slopes, attn_norm, page_table, cache_k):
    b, t, _ = q.shape
    blk = MOBA_BLOCK
    nblk = t // blk
    assert t % blk == 0
    qpt = 2 if nblk % 2 == 0 else 1
    kpi = max(kk for kk in (4, 2, 1) if nblk % kk == 0)
    db, n_pages = page_table.shape
    _, page, nh, hd = cache_k.shape
    ppb = blk // page
    n_steps = b * N_HEADS * (nblk // qpt)
    pps = max(p for p in range(ppb, STREAM_PAGES + 1, ppb) if n_pages % p == 0)
    n_chunks = db * n_pages // pps
    cps = -(-n_chunks // n_steps)
    assert cps <= 2
    qw = qpt * blk
    seq = pl.BlockSpec((1, t, HEAD_DIM), lambda bi, h, qi, pt: (bi, 0, h))
    tile = pl.BlockSpec((1, qw, HEAD_DIM), lambda bi, h, qi, pt: (bi, qi, h))
    return pl.pallas_call(
        functools.partial(_moba_prompt_kernel, nblk=nblk, qpt=qpt, kpi=kpi, n_steps=n_steps, n_chunks=n_chunks,
                          cps=cps, pps=pps, n_pages=n_pages),
        grid_spec=pltpu.PrefetchScalarGridSpec(
            num_scalar_prefetch=1,
            grid=(b, N_HEADS, nblk // qpt),
            in_specs=[
                pl.BlockSpec(memory_space=pltpu.SMEM),
                seq, seq, seq,
                pl.BlockSpec((1, HEAD_DIM), lambda bi, h, qi, pt: (0, h)),
                pl.BlockSpec(memory_space=pl.ANY),
            ],
            out_specs=[tile, pl.BlockSpec(memory_space=pl.ANY)],
            scratch_shapes=[
                pltpu.VMEM((t, HEAD_DIM), BF16),
                pltpu.VMEM((t, HEAD_DIM), BF16),
                pltpu.VMEM((HEAD_DIM, t), BF16),
                pltpu.VMEM((nblk, SUBLANES, t), F32),
                pltpu.VMEM((kpi * blk, qw), F32),
                pltpu.VMEM((qw, qw), F32),
                pltpu.VMEM((2, pps, page, nh, hd), F32),
                pltpu.VMEM((db, n_pages // ppb, nh, hd), F32),
                pltpu.SemaphoreType.DMA((2,)),
                pltpu.SemaphoreType.DMA((1,)),
            ],
        ),
        out_shape=[jax.ShapeDtypeStruct((b, t, D_ATTN), BF16),
                   jax.ShapeDtypeStruct((db, n_pages // ppb, nh, hd), F32)],
        compiler_params=_cparams(("arbitrary", "arbitrary", "arbitrary")),
        name="moba_prompt",
    )(page_table.reshape(-1), slopes, q, k, v, attn_norm, cache_k)


def _sample_topk_kernel(q_ref, km_ref, idx_ref):
    nbf = km_ref.shape[1]
    s_len = q_ref.shape[2]
    gates = []
    for h in range(N_HEADS):
        km = jnp.concatenate([km_ref[0, :, h, :], jnp.zeros((LANES - nbf, HEAD_DIM), F32)], axis=0)
        gates.append(_nt_dot(q_ref[0, h], km, precision=lax.Precision.HIGHEST))
    gate = jnp.concatenate(gates, axis=0)
    lane = lax.broadcasted_iota(jnp.int32, gate.shape, 1)
    g = jnp.where(lane < nbf, gate, -jnp.inf)
    out = jnp.zeros(gate.shape, jnp.int32)
    for r in range(MOBA_TOPK):
        m = jnp.max(g, axis=-1, keepdims=True)
        idx = jnp.min(jnp.where(g == m, lane, LANES), axis=-1, keepdims=True)
        out = jnp.where(lane == r, idx, out)
        g = jnp.where(lane == idx, -jnp.inf, g)
    idx_ref[0] = out.reshape(N_HEADS, s_len, LANES)


def _sample_topk(q_t, kmean):
    db, nh, s, hd = q_t.shape
    nbf = kmean.shape[1]
    return pl.pallas_call(
        _sample_topk_kernel,
        grid=(db,),
        in_specs=[
            pl.BlockSpec((1, nh, s, hd), lambda i: (i, 0, 0, 0)),
            pl.BlockSpec((1, nbf, nh, hd), lambda i: (i, 0, 0, 0)),
        ],
        out_specs=pl.BlockSpec((1, nh, s, LANES), lambda i: (i, 0, 0, 0)),
        out_shape=jax.ShapeDtypeStruct((db, nh, s, LANES), jnp.int32),
        compiler_params=_cparams(("arbitrary",)),
        name="sample_topk",
    )(q_t, kmean)


def _moba_sample_kernel(pt_ref, blk_ref, slopes_ref, q_ref, kn_ref, vn_ref, nrm_ref,
                        glu_ref, w_ref, b_ref, gg_ref, gb_ref, ck_ref, cv_ref, o_ref, c_ref,
                        kbuf, vbuf, xs_ref, sem, *, n_units, n_sel, ppb, past, tt, rc, nt, cpu):
    page = ck_ref.shape[1]
    n_slab = n_sel * ppb
    n_keys = n_slab * page
    n_pages = past // page
    s_len = q_ref.shape[2]
    rows = 16

    def copies(u, slot, i, h, pg):
        dst = pl.ds(pl.multiple_of(i * page, page), page)
        return (pltpu.make_async_copy(ck_ref.at[pg, :, h, :], kbuf.at[slot, dst], sem.at[0, slot]),
                pltpu.make_async_copy(cv_ref.at[pg, :, h, :], vbuf.at[slot, dst], sem.at[1, slot]))

    def start(u, slot):
        h = u % N_HEADS
        row0 = (u // N_HEADS) * n_pages

        def sel(n, carry):
            logical = blk_ref[u * n_sel + n] * ppb
            for p in range(ppb):
                ck, cv = copies(u, slot, n * ppb + p, h, pt_ref[row0 + logical + p])
                ck.start()
                cv.start()
            return carry

        lax.fori_loop(0, n_sel, sel, 0)

    u = pl.program_id(0)

    @pl.when(u == 0)
    def _():
        start(0, 0)

    slot = u % 2
    bi = u // N_HEADS
    h = u % N_HEADS

    @pl.when(u + 1 < n_units)
    def _():
        start(u + 1, 1 - slot)

    for k in range(cpu):
        v = u * cpu + k
        cb = v % CONV_GROUPS
        tile = v // CONV_GROUPS

        @pl.when(cb == 0)
        def _():
            @pl.when(tile % nt == 0)
            def _():
                xs_ref[0, 0:HALO, :] = jnp.zeros((HALO, xs_ref.shape[2]), F32)

            xs_ref[0, HALO:HALO + tt, :] = glu_ref[0]
            _conv_preshift(xs_ref, tt)

        _conv_lane_block(xs_ref, w_ref, b_ref, gg_ref, gb_ref, c_ref, 0,
                         pl.ds(pl.multiple_of(cb * LANES, LANES), LANES), tt=tt, rc=rc, preshift=True)

        @pl.when(cb == CONV_GROUPS - 1)
        def _():
            xs_ref[0, 0:HALO, :] = xs_ref[0, tt:tt + HALO, :]

    def slab_done(i, carry):
        ck, cv = copies(u, slot, i, 0, 0)
        ck.wait()
        cv.wait()
        return carry

    lax.fori_loop(0, n_slab, slab_done, 0)

    slope = slopes_ref[h]
    q = q_ref[bi, h]
    qs = jnp.concatenate([q * (HEAD_DIM ** -0.5), jnp.zeros((rows - s_len, HEAD_DIM), F32)], axis=0)
    qs = qs.astype(BF16)
    kb = kbuf[slot].astype(BF16)
    vb = vbuf[slot].astype(BF16)
    s = _nt_dot(qs, kb)

    per_blk = page * ppb
    offs = lax.broadcasted_iota(jnp.int32, (1, per_blk), 1)
    kpos = jnp.concatenate(
        [(blk_ref[u * n_sel + i] * per_blk + offs).astype(F32) for i in range(n_sel)], axis=1)
    row = lax.broadcasted_iota(jnp.int32, (rows, 1), 0)
    qpos = (past + row).astype(F32)
    s = s - slope * (qpos - kpos)
    col = lax.broadcasted_iota(jnp.int32, (rows, n_keys), 1)
    per_q = MOBA_TOPK * per_blk
    mine = jnp.logical_and(col >= row * per_q, col < (row + 1) * per_q)
    s = jnp.where(mine, s, NEG_INF)

    kn = jnp.concatenate([kn_ref[bi, h], jnp.zeros((rows - s_len, HEAD_DIM), F32)], axis=0).astype(BF16)
    vn = jnp.concatenate([vn_ref[bi, h], jnp.zeros((rows - s_len, HEAD_DIM), F32)], axis=0).astype(BF16)
    ri = lax.broadcasted_iota(jnp.int32, (rows, rows), 0)
    ci = lax.broadcasted_iota(jnp.int32, (rows, rows), 1)
    so = _nt_dot(qs, kn) - slope * (ri - ci).astype(F32)
    so = jnp.where(jnp.logical_and(ri >= ci, ci < s_len), so, NEG_INF)

    m = jnp.maximum(jnp.max(s, axis=-1, keepdims=True), jnp.max(so, axis=-1, keepdims=True))
    p = jnp.exp(s - m)
    po = jnp.exp(so - m)
    l = jnp.sum(p, axis=-1, keepdims=True) + jnp.sum(po, axis=-1, keepdims=True)
    acc = (jnp.dot(p.astype(BF16), vb, preferred_element_type=F32)
           + jnp.dot(po.astype(BF16), vn, preferred_element_type=F32))
    o = (acc / l)[0:s_len]
    o = (o * lax.rsqrt(jnp.mean(o * o, axis=-1, keepdims=True) + EPS)) * nrm_ref[pl.ds(h, 1), :]
    o_ref[bi, h] = o


def _moba_sample(page_table_flat, blks, slopes, q_t, kn_t, vn_t, attn_norm_hd, cache_k, cache_v, past,
                 glu, w_dw, b_dw, gn_g, gn_b, tt):
    db, nh, s, hd = q_t.shape
    _, page, _, _ = cache_k.shape
    ppb = MOBA_BLOCK // page
    n_sel = s * MOBA_TOPK
    n_units = db * nh
    b, t, dc = glu.shape
    assert dc == CONV_GROUPS * LANES and t % tt == 0 and tt >= 4 * SUBLANES
    nt = t // tt
    n_conv_units = b * nt * CONV_GROUPS
    cpu = -(-n_conv_units // n_units)
    assert CONV_GROUPS % cpu == 0 and cpu * n_units == n_conv_units
    whole = lambda shp: pl.BlockSpec(shp, lambda i, *_: (0,) * len(shp))
    tile = pl.BlockSpec((1, tt, dc), lambda i, *_: ((i * cpu // CONV_GROUPS) // nt, (i * cpu // CONV_GROUPS) % nt, 0))
    return pl.pallas_call(
        functools.partial(_moba_sample_kernel, n_units=n_units, n_sel=n_sel, ppb=ppb, past=past,
                          tt=tt, rc=min(tt, 64), nt=nt, cpu=cpu),
        grid_spec=pltpu.PrefetchScalarGridSpec(
            num_scalar_prefetch=3,
            grid=(n_units,),
            in_specs=[
                whole(q_t.shape), whole(kn_t.shape), whole(vn_t.shape), whole(attn_norm_hd.shape),
                tile, whole(w_dw.shape), whole(b_dw.shape), whole(gn_g.shape), whole(gn_b.shape),
                pl.BlockSpec(memory_space=pl.ANY),
                pl.BlockSpec(memory_space=pl.ANY),
            ],
            out_specs=[whole(q_t.shape), tile],
            scratch_shapes=[
                pltpu.VMEM((2, n_sel * ppb * page, hd), F32),
                pltpu.VMEM((2, n_sel * ppb * page, hd), F32),
                pltpu.VMEM((SUBLANES, HALO + tt, dc), F32),
                pltpu.SemaphoreType.DMA((2, 2)),
            ],
        ),
        out_shape=[jax.ShapeDtypeStruct(q_t.shape, F32), jax.ShapeDtypeStruct((b, t, dc), BF16)],
        compiler_params=_cparams(("arbitrary",)),
        name="moba_sample_conv",
    )(page_table_flat, blks, slopes, q_t, kn_t, vn_t, attn_norm_hd, glu, w_dw, b_dw, gn_g, gn_b, cache_k, cache_v)


def _outproj_kernel(op_ref, cp_ref, xp_ref, os_ref, cs_ref, xs_ref, wo_ref, wc_ref, g_ref, wr_ref, br_ref,
                    x2_ref, h2_ref, route_ref, *, nb_p):
    i = pl.program_id(0)
    tail = (wo_ref, wc_ref, g_ref, wr_ref, br_ref, x2_ref, h2_ref, route_ref)

    @pl.when(i < nb_p)
    def _():
        _outproj_tile(op_ref, cp_ref, xp_ref, *tail)

    @pl.when(i >= nb_p)
    def _():
        _outproj_tile(os_ref, cs_ref, xs_ref, *tail)


def _outproj_tile(o_ref, c_ref, x_ref, wo_ref, wc_ref, g_ref, wr_ref, br_ref, x2_ref, h2_ref, route_ref):
    mix = (jnp.dot(o_ref[...].astype(BF16), wo_ref[...], preferred_element_type=F32)
           + jnp.dot(c_ref[...].astype(BF16), wc_ref[...], preferred_element_type=F32))
    x2 = x_ref[...] + mix
    x2_ref[...] = x2
    r = lax.rsqrt(jnp.mean(x2 * x2, axis=-1, keepdims=True) + EPS)
    h2 = (x2 * r) * g_ref[...]
    h2_ref[...] = h2

    h_hi = h2.astype(BF16)
    h_lo = (h2 - h_hi.astype(F32)).astype(BF16)
    r1 = jnp.dot(h_hi, wr_ref[...], preferred_element_type=F32)
    r2 = jnp.dot(h_lo, wr_ref[:, 0:LANES], preferred_element_type=F32)
    logits = r1[:, 0:LANES] + r1[:, LANES:2 * LANES] + r2 + br_ref[...]

    lane = lax.broadcasted_iota(jnp.int32, logits.shape, 1)
    is_g = lane < N_EXPERT_GROUPS
    gl = jnp.where(is_g, logits, -jnp.inf)
    gmax = jnp.max(gl, axis=-1, keepdims=True)
    gidx = jnp.min(jnp.where(gl == gmax, lane, LANES), axis=-1, keepdims=True)
    gsum = jnp.sum(jnp.where(is_g, jnp.exp(gl - gmax), 0.0), axis=-1, keepdims=True)
    g_gate = 1.0 / gsum

    lo = N_EXPERT_GROUPS + gidx * EXPERTS_PER_GROUP
    el = jnp.where(jnp.logical_and(lane >= lo, lane < lo + EXPERTS_PER_GROUP), logits, -jnp.inf)
    v0 = jnp.max(el, axis=-1, keepdims=True)
    i0 = jnp.min(jnp.where(el == v0, lane, LANES), axis=-1, keepdims=True)
    el = jnp.where(lane == i0, -jnp.inf, el)
    v1 = jnp.max(el, axis=-1, keepdims=True)
    i1 = jnp.min(jnp.where(el == v1, lane, LANES), axis=-1, keepdims=True)
    e = jnp.exp(v1 - v0)
    w0 = g_gate / (1.0 + e)
    w1 = g_gate * e / (1.0 + e)

    out = jnp.where(lane == 0, (i0 - N_EXPERT_GROUPS).astype(F32), 0.0)
    out = jnp.where(lane == 1, (i1 - N_EXPERT_GROUPS).astype(F32), out)
    out = jnp.where(lane == 2, w0, out)
    out = jnp.where(lane == 3, w1, out)
    route_ref[...] = out


def _out_projection(o_p, c_p, x_p, o_s, c_s, x_s, wo, wc, norm_g, wr, br, tm):
    np_, d = x_p.shape
    ns_ = x_s.shape[0]
    assert np_ % tm == 0 and ns_ % tm == 0
    nb_p, nb_s = np_ // tm, ns_ // tm
    n = np_ + ns_
    prow = lambda w: pl.BlockSpec((tm, w), lambda i: (jnp.minimum(i, nb_p - 1), 0))
    srow = lambda w: pl.BlockSpec((tm, w), lambda i: (jnp.maximum(i - nb_p, 0), 0))
    row = lambda w: pl.BlockSpec((tm, w), lambda i: (i, 0))
    full = lambda a: pl.BlockSpec(a.shape, lambda i: (0, 0))
    return pl.pallas_call(
        functools.partial(_outproj_kernel, nb_p=nb_p),
        grid=(nb_p + nb_s,),
        in_specs=[prow(o_p.shape[1]), prow(c_p.shape[1]), prow(d), srow(o_s.shape[1]), srow(c_s.shape[1]), srow(d),
                  full(wo), full(wc), full(norm_g), full(wr), full(br)],
        out_specs=[row(d), row(d), row(LANES)],
        out_shape=[jax.ShapeDtypeStruct((n, d), F32), jax.ShapeDtypeStruct((n, d), F32),
                   jax.ShapeDtypeStruct((n, LANES), F32)],
        compiler_params=_cparams(("arbitrary",)),
        name="out_projection_router",
    )(o_p, c_p, x_p, o_s, c_s, x_s, wo, wc, norm_g, wr, br)


DMA_UNROLL = SUBLANES


def _for_rows(n, fn):
    def group(gi, carry):
        for k in range(DMA_UNROLL):
            fn(gi * DMA_UNROLL + k, gi, k)
        return carry

    def single(i, carry):
        fn(i, lax.div(i, DMA_UNROLL), lax.rem(i, DMA_UNROLL))
        return carry

    n_grp = lax.div(n, DMA_UNROLL)
    lax.fori_loop(0, n_grp, group, 0)
    lax.fori_loop(n_grp * DMA_UNROLL, n, single, 0)


def _moe_kernel(order_ref, cexp_ref, cbase_ref, cnv_ref, h_ref, wg_ref, wu_ref, wd_ref, out_ref,
                xbuf, obuf, wgb, wub, wdb, gsem, ssem, *, n_tok, n_chunks):
    g = pl.program_id(0)
    slot = g % 2
    _, n_grp, grp, d = xbuf.shape

    def gather(chunk, buf_slot):
        base = cbase_ref[chunk]

        def desc(i, gi, k):
            tok = lax.shift_right_logical(order_ref[base + i], 1)
            return pltpu.make_async_copy(h_ref.at[pl.ds(tok, 1)], xbuf.at[buf_slot, gi, pl.ds(k, 1)],
                                         gsem.at[buf_slot])
        return desc

    def scatter(chunk, buf_slot):
        base = cbase_ref[chunk]

        def desc(i, gi, k):
            pair = order_ref[base + i]
            row = (pair & 1) * n_tok + lax.shift_right_logical(pair, 1)
            return pltpu.make_async_copy(obuf.at[buf_slot, gi, pl.ds(k, 1)], out_ref.at[pl.ds(row, 1)],
                                         ssem.at[buf_slot])
        return desc

    def gather_done(buf_slot):
        pltpu.make_async_copy(h_ref.at[pl.ds(0, 1)], xbuf.at[buf_slot, 0, pl.ds(0, 1)], gsem.at[buf_slot]).wait()

    def scatter_done(buf_slot):
        pltpu.make_async_copy(obuf.at[buf_slot, 0, pl.ds(0, 1)], out_ref.at[pl.ds(0, 1)], ssem.at[buf_slot]).wait()

    @pl.when(g == 0)
    def _():
        xbuf[...] = jnp.zeros(xbuf.shape, xbuf.dtype)
        _for_rows(cnv_ref[0], lambda i, gi, k: gather(0, 0)(i, gi, k).start())

    @pl.when(jnp.logical_or(g == 0, cexp_ref[g] != cexp_ref[jnp.maximum(g - 1, 0)]))
    def _():
        wgb[...] = wg_ref[0].astype(BF16)
        wub[...] = wu_ref[0].astype(BF16)
        wdb[...] = wd_ref[0].astype(BF16)

    @pl.when(g + 1 < n_chunks)
    def _():
        _for_rows(cnv_ref[g + 1], lambda i, gi, k: gather(g + 1, 1 - slot)(i, gi, k).start())

    nv = cnv_ref[g]
    _for_rows(nv, lambda i, gi, k: gather_done(slot))

    @pl.when(g >= 2)
    def _():
        _for_rows(cnv_ref[g - 2], lambda i, gi, k: scatter_done(slot))

    @pl.when(nv > 0)
    def _():
        x = xbuf[slot].reshape(n_grp * grp, d).astype(BF16)
        gt = jnp.dot(x, wgb[...], preferred_element_type=F32)
        up = jnp.dot(x, wub[...], preferred_element_type=F32)
        hid = (gt * (1.0 / (1.0 + jnp.exp(-gt)))) * up
        y = jnp.dot(hid.astype(BF16), wdb[...], preferred_element_type=F32)
        obuf[slot] = y.reshape(n_grp, grp, d)

    _for_rows(nv, lambda i, gi, k: scatter(g, slot)(i, gi, k).start())

    @pl.when(g == n_chunks - 1)
    def _():
        if n_chunks >= 2:
            _for_rows(cnv_ref[g - 1], lambda i, gi, k: scatter_done(1 - slot))
        _for_rows(nv, lambda i, gi, k: scatter_done(slot))


def _moe(order, cexp, cbase, cnv, h2, w_gate, w_up, w_down, ch):
    n, d = h2.shape
    ne, _, de = w_gate.shape
    n_chunks = cexp.shape[0]
    wspec = lambda shp: pl.BlockSpec((1,) + shp, lambda g, order, cexp, cbase, cnv: (cexp[g], 0, 0))
    return pl.pallas_call(
        functools.partial(_moe_kernel, n_tok=n, n_chunks=n_chunks),
        grid_spec=pltpu.PrefetchScalarGridSpec(
            num_scalar_prefetch=4,
            grid=(n_chunks,),
            in_specs=[
                pl.BlockSpec(memory_space=pl.ANY),
                wspec((d, de)), wspec((d, de)), wspec((de, d)),
            ],
            out_specs=pl.BlockSpec(memory_space=pl.ANY),
            scratch_shapes=[
                pltpu.VMEM((2, ch // DMA_UNROLL, DMA_UNROLL, d), F32),
                pltpu.VMEM((2, ch // DMA_UNROLL, DMA_UNROLL, d), F32),
                pltpu.VMEM((d, de), BF16),
                pltpu.VMEM((d, de), BF16),
                pltpu.VMEM((de, d), BF16),
                pltpu.SemaphoreType.DMA((2,)),
                pltpu.SemaphoreType.DMA((2,)),
            ],
        ),
        out_shape=jax.ShapeDtypeStruct((2 * n, d), F32),
        compiler_params=_cparams(("arbitrary",)),
        name="expert_mlp",
    )(order, cexp, cbase, cnv, h2, w_gate, w_up, w_down)


def _expert_chunks(eid, ch):
    n_pairs = eid.shape[0]
    n_chunks = N_EXPERTS + n_pairs // ch
    order = jnp.argsort(eid, stable=True).astype(jnp.int32)
    counts = jnp.sum(eid[:, None] == jnp.arange(N_EXPERTS, dtype=jnp.int32)[None, :], axis=0, dtype=jnp.int32)
    offs = jnp.cumsum(counts, dtype=jnp.int32) - counts
    nch = (counts + ch - 1) // ch
    cum = jnp.cumsum(nch, dtype=jnp.int32)
    gidx = jnp.arange(n_chunks, dtype=jnp.int32)
    cexp_raw = jnp.sum(gidx[:, None] >= cum[None, :], axis=1, dtype=jnp.int32)
    last = jnp.maximum(jnp.max(jnp.where(nch > 0, jnp.arange(N_EXPERTS, dtype=jnp.int32), 0)), 0)
    used = gidx < cum[-1]
    cexp = jnp.where(used, jnp.minimum(cexp_raw, N_EXPERTS - 1), last)
    local = gidx - (cum[cexp] - nch[cexp])
    cbase = jnp.where(used, offs[cexp] + local * ch, 0)
    cnv = jnp.where(used, jnp.clip(counts[cexp] - local * ch, 0, ch), 0)
    return order, cexp.astype(jnp.int32), cbase.astype(jnp.int32), cnv.astype(jnp.int32)


def _final_kernel(x2_ref, ya_ref, yb_ref, route_ref, g_ref, yp_ref, ys_ref, *, nb_p):
    i = pl.program_id(0)
    rt = route_ref[...]
    x3 = x2_ref[...] + rt[:, 2:3] * ya_ref[...] + rt[:, 3:4] * yb_ref[...]
    r = lax.rsqrt(jnp.mean(x3 * x3, axis=-1, keepdims=True) + EPS)
    y = (x3 * r) * g_ref[...]

    @pl.when(i < nb_p)
    def _():
        yp_ref[...] = y

    @pl.when(i >= nb_p)
    def _():
        ys_ref[...] = y


def _final(x2, y2, route, norm_g, np_, tm):
    n, d = x2.shape
    assert np_ % tm == 0 and n % tm == 0
    nb, nb_p = n // tm, np_ // tm
    return pl.pallas_call(
        functools.partial(_final_kernel, nb_p=nb_p),
        grid=(nb,),
        in_specs=[
            pl.BlockSpec((tm, d), lambda i: (i, 0)),
            pl.BlockSpec((tm, d), lambda i: (i, 0)),
            pl.BlockSpec((tm, d), lambda i: (i + nb, 0)),
            pl.BlockSpec((tm, LANES), lambda i: (i, 0)),
            pl.BlockSpec((1, d), lambda i: (0, 0)),
        ],
        out_specs=[
            pl.BlockSpec((tm, d), lambda i: (jnp.minimum(i, nb_p - 1), 0)),
            pl.BlockSpec((tm, d), lambda i: (jnp.maximum(i - nb_p, 0), 0)),
        ],
        out_shape=[jax.ShapeDtypeStruct((np_, d), F32), jax.ShapeDtypeStruct((n - np_, d), F32)],
        compiler_params=_cparams(("arbitrary",)),
        name="combine_final_norm",
    )(x2, y2, y2, route, norm_g)


def _pick_tile(n, pref):
    t = min(n, pref)
    while n % t:
        t //= 2
    return t


def kernel(x_prompt, x_sample, cache_k, cache_v, state_conv, page_table, norm_mix, w_in, w_dw, b_dw,
           conv_norm_g, conv_norm_b, attn_out_norm, w_out, norm_ffn, w_router_group, b_router_group,
           w_router_expert, b_router_expert, w_gate, w_up, w_down, norm_final):
    depth = norm_mix.shape[0]
    assert depth == 1
    b, t, d = x_prompt.shape
    db, s, _ = x_sample.shape
    n_pages = page_table.shape[1]
    page = cache_k.shape[2]
    past = n_pages * page
    nbf = past // MOBA_BLOCK
    assert past % MOBA_BLOCK == 0 and nbf >= MOBA_TOPK and nbf <= LANES
    d_conv = d - D_ATTN
    assert d_conv == D_ATTN
    np_, ns_ = b * t, db * s

    slopes = jnp.exp2(-8.0 * jnp.arange(1, N_HEADS + 1, dtype=F32) / N_HEADS)
    w_in_bf = w_in[0].astype(BF16)
    wo_bf = w_out[0, :D_ATTN].astype(BF16)
    wc_bf = w_out[0, D_ATTN:].astype(BF16)
    wr = jnp.zeros((d, LANES), F32)
    wr = wr.at[:, :N_EXPERT_GROUPS].set(w_router_group[0])
    wr = wr.at[:, N_EXPERT_GROUPS:N_EXPERT_GROUPS + N_EXPERTS].set(w_router_expert[0])
    wr_hi = wr.astype(BF16)
    wr_lo = (wr - wr_hi.astype(F32)).astype(BF16)
    wr_cat = jnp.concatenate([wr_hi, wr_lo], axis=1)
    br = jnp.zeros((1, LANES), F32)
    br = br.at[0, :N_EXPERT_GROUPS].set(b_router_group[0])
    br = br.at[0, N_EXPERT_GROUPS:N_EXPERT_GROUPS + N_EXPERTS].set(b_router_expert[0])
    g_mix = norm_mix[0].reshape(1, d)
    g_ffn = norm_ffn[0].reshape(1, d)
    g_fin = norm_final.reshape(1, d)
    attn_norm = attn_out_norm[0].reshape(1, D_ATTN)
    conv_vec = lambda a: a[0].reshape(1, d_conv)

    xp = x_prompt.reshape(np_, d)
    xs = x_sample.reshape(ns_, d)
    qp, kp, vp, glu_p = _in_projection(xp, g_mix, w_in_bf, _pick_tile(np_, 512))
    qs, ks, vs, glu_s = _in_projection(xs, g_mix, w_in_bf, _pick_tile(ns_, 256))

    glu_p3 = glu_p.reshape(b, t, d_conv)
    glu_s3 = glu_s.reshape(db, s, d_conv)
    conv_args = (w_dw[0], conv_vec(b_dw), conv_vec(conv_norm_g), conv_vec(conv_norm_b))
    c_s = _conformer_conv(glu_s3, state_conv[0], *conv_args, tt=s, nseq=db, out_dtype=F32)
    conv_prompt = glu_p3[:, t - (CONV_WIDTH - 1):][None]
    conv_sample = jnp.concatenate([state_conv[0], glu_s3], axis=1)[:, -(CONV_WIDTH - 1):][None]

    ck = cache_k[0]
    cv = cache_v[0]
    o_p, kmean = _moba_prompt(qp.reshape(b, t, D_ATTN), kp.reshape(b, t, D_ATTN), vp.reshape(b, t, D_ATTN),
                              slopes, attn_norm, page_table, ck)

    to_heads = lambda a: a.reshape(db, s, N_HEADS, HEAD_DIM).transpose(0, 2, 1, 3)
    q_t, kn_t, vn_t = to_heads(qs), to_heads(ks), to_heads(vs)
    idx = _sample_topk(q_t, kmean)[..., :MOBA_TOPK]
    o_t, c_p = _moba_sample(page_table.reshape(-1), idx.reshape(-1), slopes, q_t, kn_t, vn_t,
                            attn_norm.reshape(N_HEADS, HEAD_DIM), ck, cv, past,
                            glu_p3, *conv_args, tt=_pick_tile(t, 256))
    o_s = o_t.transpose(0, 2, 1, 3).reshape(ns_, D_ATTN)

    tm = _pick_tile(ns_, 256)
    x2, h2, route = _out_projection(o_p.reshape(np_, D_ATTN), c_p.reshape(np_, d_conv), xp,
                                    o_s, c_s.reshape(ns_, d_conv), xs,
                                    wo_bf, wc_bf, g_ffn, wr_cat, br, tm)

    ch = 256
    eid = route[:, 0:2].astype(jnp.int32).reshape(-1)
    order, cexp, cbase, cnv = _expert_chunks(eid, ch)
    y2 = _moe(order, cexp, cbase, cnv, h2, w_gate[0], w_up[0], w_down[0], ch)

    y_p, y_s = _final(x2, y2, route, g_fin, np_, tm)

    page_shape = (depth, b, t // page, page, N_HEADS, HEAD_DIM)
    return (y_p.reshape(b, t, d), y_s.reshape(db, s, d),
            kp.reshape(page_shape), vp.reshape(page_shape), conv_prompt,
            ks.reshape(depth, db, s, N_HEADS, HEAD_DIM), vs.reshape(depth, db, s, N_HEADS, HEAD_DIM),
            conv_sample)
```

```python
import functools

import jax
import jax.numpy as jnp
from jax import lax
from jax.experimental import pallas as pl
from jax.experimental.pallas import tpu as pltpu

F32 = jnp.float32
BF16 = jnp.bfloat16

N_HEADS = 8
HEAD_DIM = 128
D_ATTN = N_HEADS * HEAD_DIM
CONV_GROUPS = 8
CONV_WIDTH = 31
MOBA_BLOCK = 256
MOBA_TOPK = 3
N_EXPERT_GROUPS = 4
EXPERTS_PER_GROUP = 8
N_EXPERTS = N_EXPERT_GROUPS * EXPERTS_PER_GROUP
EPS = 1e-6
NEG_INF = -1e30

LANES = 128
SUBLANES = 8
HALO = 32
VMEM_LIMIT = 56 * 1024 * 1024
STREAM_PAGES = 16


def _cparams(sem=None):
    return pltpu.CompilerParams(dimension_semantics=sem, vmem_limit_bytes=VMEM_LIMIT)


def _nt_dot(a, b, precision=None):
    return lax.dot_general(a, b, (((1,), (1,)), ((), ())), precision=precision,
                           preferred_element_type=F32)


def _inproj_kernel(x_ref, g_ref, w_ref, wgate_ref, q_ref, k_ref, v_ref, glu_ref, h_ref):
    j = pl.program_id(1)

    @pl.when(j == 0)
    def _():
        x = x_ref[...]
        r = lax.rsqrt(jnp.mean(x * x, axis=-1, keepdims=True) + EPS)
        h_ref[...] = ((x * r) * g_ref[...]).astype(BF16)

    z = jnp.dot(h_ref[...], w_ref[...], preferred_element_type=F32)

    @pl.when(j == 0)
    def _():
        q_ref[...] = z

    @pl.when(j == 1)
    def _():
        k_ref[...] = z

    @pl.when(j == 2)
    def _():
        v_ref[...] = z

    @pl.when(j == 3)
    def _():
        gate = jnp.dot(h_ref[...], wgate_ref[...], preferred_element_type=F32)
        glu_ref[...] = z * (1.0 / (1.0 + jnp.exp(-gate)))


def _in_projection(x, norm_g, w_in_bf, tm):
    n, d = x.shape
    dc = D_ATTN
    out = jax.ShapeDtypeStruct((n, dc), F32)
    row = pl.BlockSpec((tm, dc), lambda i, j: (i, 0))
    return pl.pallas_call(
        _inproj_kernel,
        grid=(n // tm, 4),
        in_specs=[
            pl.BlockSpec((tm, d), lambda i, j: (i, 0)),
            pl.BlockSpec((1, d), lambda i, j: (0, 0)),
            pl.BlockSpec((d, dc), lambda i, j: (0, j)),
            pl.BlockSpec((d, dc), lambda i, j: (0, 4)),
        ],
        out_specs=[row, row, row, row],
        out_shape=[out, out, out, out],
        scratch_shapes=[pltpu.VMEM((tm, d), BF16)],
        compiler_params=_cparams(("arbitrary", "arbitrary")),
        name="in_projection",
    )(x, norm_g, w_in_bf, w_in_bf)


def _conv_lane_block(xs_ref, w_ref, b_ref, gg_ref, gb_ref, c_ref, sq, cs, *, tt, rc, preshift):
    off = HALO - (CONV_WIDTH - 1)
    for r0 in range(0, tt, rc):
        acc = jnp.broadcast_to(b_ref[:, cs], (rc, LANES))
        for j in range(CONV_WIDTH):
            o = off + j
            if preshift:
                lo = r0 + (o // SUBLANES) * SUBLANES
                x = xs_ref[o % SUBLANES, lo:lo + rc, cs]
            else:
                x = xs_ref[0, r0 + o:r0 + o + rc, cs]
            acc = acc + x * w_ref[j:j + 1, cs]
        mu = jnp.mean(acc, axis=-1, keepdims=True)
        dlt = acc - mu
        var = jnp.mean(dlt * dlt, axis=-1, keepdims=True)
        yn = dlt * lax.rsqrt(var + EPS)
        yn = yn * gg_ref[:, cs] + gb_ref[:, cs]
        c_ref[sq, r0:r0 + rc, cs] = (yn * (1.0 / (1.0 + jnp.exp(-yn)))).astype(c_ref.dtype)


def _conv_preshift(xs_ref, tt):
    span = HALO + tt - SUBLANES
    for r in range(1, SUBLANES):
        xs_ref[r, 0:span, :] = xs_ref[0, r:r + span, :]


def _conv_kernel(glu_ref, st_ref, w_ref, b_ref, gg_ref, gb_ref, c_ref, xs_ref, *, tt, rc, nseq, carry, preshift):
    t = pl.program_id(1)

    def lane_block(sq, cs):
        _conv_lane_block(xs_ref, w_ref, b_ref, gg_ref, gb_ref, c_ref, sq, cs, tt=tt, rc=rc, preshift=preshift)

    hist = HALO - (CONV_WIDTH - 1)

    def load_state(sq):
        xs_ref[0, 0:hist, :] = jnp.zeros((hist, xs_ref.shape[2]), F32)
        xs_ref[0, hist:HALO, :] = st_ref[sq]

    def one_seq(sq):
        if carry:
            pl.when(t == 0)(functools.partial(load_state, sq))
        else:
            load_state(sq)
        xs_ref[0, HALO:HALO + tt, :] = glu_ref[sq]
        if preshift:
            _conv_preshift(xs_ref, tt)

            def group(cb, carry_):
                lane_block(sq, pl.ds(pl.multiple_of(cb * LANES, LANES), LANES))
                return carry_

            lax.fori_loop(0, CONV_GROUPS, group, 0)
        else:
            for cb in range(CONV_GROUPS):
                lane_block(sq, pl.ds(cb * LANES, LANES))
        if carry:
            xs_ref[0, 0:HALO, :] = xs_ref[0, tt:tt + HALO, :]

    if nseq == 1:
        one_seq(0)
    else:
        def seq(sq, carry_):
            one_seq(sq)
            return carry_

        lax.fori_loop(0, nseq, seq, 0)


def _conformer_conv(glu, state, w_dw, b_dw, gn_g, gn_b, tt, nseq, out_dtype):
    b, t, dc = glu.shape
    assert dc == CONV_GROUPS * LANES and b % nseq == 0 and t % tt == 0 and state.shape[1] == CONV_WIDTH - 1
    nt = t // tt
    assert nt == 1 or nseq == 1
    preshift = tt >= 4 * SUBLANES
    rc = min(tt, 64)
    vec = pl.BlockSpec((1, dc), lambda i, j: (0, 0))
    return pl.pallas_call(
        functools.partial(_conv_kernel, tt=tt, rc=rc, nseq=nseq, carry=nt > 1, preshift=preshift),
        grid=(b // nseq, nt),
        in_specs=[
            pl.BlockSpec((nseq, tt, dc), lambda i, j: (i, j, 0)),
            pl.BlockSpec((nseq, CONV_WIDTH - 1, dc), lambda i, j: (i, 0, 0)),
            pl.BlockSpec((CONV_WIDTH, dc), lambda i, j: (0, 0)),
            vec, vec, vec,
        ],
        out_specs=pl.BlockSpec((nseq, tt, dc), lambda i, j: (i, j, 0)),
        out_shape=jax.ShapeDtypeStruct((b, t, dc), out_dtype),
        scratch_shapes=[pltpu.VMEM((SUBLANES if preshift else 1, HALO + tt, dc), F32)],
        compiler_params=_cparams(("arbitrary", "arbitrary")),
        name="conformer_conv",
    )(glu, state, w_dw, b_dw, gn_g, gn_b)


def _moba_prompt_kernel(pt_ref, slopes_ref, q_ref, k_ref, v_ref, nrm_ref, ck_ref, o_ref, kmean_ref,
                        qb_ref, kb_ref, vt_ref, g_ref, biasp_ref, biasd_ref, pbuf, kmbuf, psem, osem,
                        *, nblk, qpt, kpi, n_steps, n_chunks, cps, pps, n_pages):
    h = pl.program_id(1)
    qi = pl.program_id(2)
    blk = MOBA_BLOCK
    qw = qpt * blk
    t = q_ref.shape[1]
    nb8 = -(-nblk // SUBLANES) * SUBLANES
    slope = slopes_ref[h]

    step = (pl.program_id(0) * N_HEADS + h) * (nblk // qpt) + qi
    c0 = step * cps
    exact = n_chunks == cps * n_steps
    ppb = blk // pbuf.shape[2]

    def start_chunk(chunk):
        slot = chunk % 2
        for i in range(pps):
            pltpu.make_async_copy(ck_ref.at[pt_ref[chunk * pps + i]], pbuf.at[slot, i], psem.at[slot]).start()

    def finish_chunk(chunk):
        slot = chunk % 2
        for i in range(pps):
            pltpu.make_async_copy(ck_ref.at[0], pbuf.at[slot, i], psem.at[slot]).wait()
        bi = (chunk * pps) // n_pages
        blk0 = ((chunk * pps) % n_pages) // ppb
        for i in range(pps // ppb):
            s = jnp.sum(pbuf[slot, ppb * i], axis=0)
            for p in range(1, ppb):
                s = s + jnp.sum(pbuf[slot, ppb * i + p], axis=0)
            kmbuf[bi, blk0 + i] = s * (1.0 / blk)

    def if_real(chunk, fn, always):
        if always:
            fn(chunk)
        else:
            pl.when(chunk < n_chunks)(functools.partial(fn, chunk))

    @pl.when(step == 0)
    def _():
        start_chunk(0)

    if_real(c0 + 1, start_chunk, exact and cps == 2)

    @pl.when(qi == 0)
    def _():
        q = q_ref[0]
        kf = k_ref[0]
        qb_ref[...] = (q * (HEAD_DIM ** -0.5)).astype(BF16)
        kb_ref[...] = kf.astype(BF16)
        vt_ref[...] = v_ref[0].T.astype(BF16)
        km = jnp.sum(kf.reshape(nblk, blk, HEAD_DIM), axis=1) * (1.0 / blk)
        if nb8 > nblk:
            km = jnp.concatenate([km, jnp.zeros((nb8 - nblk, HEAD_DIM), F32)], axis=0)
        gate = _nt_dot(km, q, precision=lax.Precision.HIGHEST)
        j_io = lax.broadcasted_iota(jnp.int32, gate.shape, 0)
        t_io = lax.broadcasted_iota(jnp.int32, gate.shape, 1)
        valid = j_io * blk + (blk - 1) < t_io - (t_io % blk)
        g = jnp.where(valid, gate, NEG_INF)
        sel = jnp.zeros(gate.shape, jnp.bool_)
        for _ in range(MOBA_TOPK):
            m = jnp.max(g, axis=0, keepdims=True)
            idx = jnp.min(jnp.where(g == m, j_io, nb8), axis=0, keepdims=True)
            pick = j_io == idx
            sel = jnp.logical_or(sel, pick)
            g = jnp.where(pick, -jnp.inf, g)
        gb = (slope * blk) * j_io.astype(F32) + jnp.where(jnp.logical_and(sel, valid), 0.0, NEG_INF)
        for j in range(nblk):
            g_ref[j] = jnp.broadcast_to(gb[j:j + 1, :], (SUBLANES, t))
        c = lax.broadcasted_iota(jnp.int32, (kpi * blk, qw), 0)
        biasp_ref[...] = slope * (c % blk).astype(F32)
        c = lax.broadcasted_iota(jnp.int32, (qw, qw), 0)
        r = lax.broadcasted_iota(jnp.int32, (qw, qw), 1)
        hidden = jnp.logical_or(c // blk > r // blk, jnp.logical_and(c // blk == r // blk, c % blk > r % blk))
        biasd_ref[...] = slope * (c % blk).astype(F32) + jnp.where(hidden, NEG_INF, 0.0)

    own = pl.ds(pl.multiple_of(qi * qw, qw), qw)
    qs = qb_ref[own, :]
    lane_blk = lax.broadcasted_iota(jnp.int32, (1, qw), 1) // blk

    def softmax_terms(st, gs, m_prev):
        segs = [st[i * blk:(i + 1) * blk] for i in range(len(gs))]
        m_new = m_prev
        for seg, g in zip(segs, gs):
            cand = jnp.max(seg, axis=0, keepdims=True) + g
            m_new = cand if m_new is None else jnp.maximum(m_new, cand)
        p = jnp.concatenate([jnp.exp(seg - (m_new - g)) for seg, g in zip(segs, gs)], axis=0)
        return m_new, p

    st = _nt_dot(kb_ref[own, :], qs) + biasd_ref[...]
    gs = []
    for i in range(qpt):
        g_blk = (slope * blk) * (qi * qpt + i).astype(F32)
        gs.append(jnp.where(lane_blk <= i, g_blk, g_ref[qi * qpt + i, 0:1, own]))
    m0, p = softmax_terms(st, gs, None)
    l0 = jnp.sum(p, axis=0, keepdims=True)
    acc0 = jnp.dot(vt_ref[:, own], p.astype(BF16), preferred_element_type=F32)

    n_past = qi * qpt

    def past_blocks(jp, carry):
        m, l, acc = carry
        rows = pl.ds(pl.multiple_of(jp * (kpi * blk), kpi * blk), kpi * blk)
        st = _nt_dot(kb_ref[rows, :], qs) + biasp_ref[...]
        gs = [jnp.where(jp * kpi + i < n_past, g_ref[jp * kpi + i, 0:1, own], NEG_INF) for i in range(kpi)]
        m_new, p = softmax_terms(st, gs, m)
        alpha = jnp.exp(m - m_new)
        l = alpha * l + jnp.sum(p, axis=0, keepdims=True)
        acc = alpha * acc + jnp.dot(vt_ref[:, rows], p.astype(BF16), preferred_element_type=F32)
        return m_new, l, acc

    _, l, acc = lax.fori_loop(0, (n_past + kpi - 1) // kpi, past_blocks, (m0, l0, acc0))
    ot = acc * (1.0 / l)
    ot = ot * lax.rsqrt(jnp.mean(ot * ot, axis=0, keepdims=True) + EPS)
    o_ref[0] = (ot.T * nrm_ref[...]).astype(o_ref.dtype)

    if_real(c0, finish_chunk, exact)
    if cps == 2:
        if_real(c0 + 2, start_chunk, False)
        if_real(c0 + 1, finish_chunk, exact)

    @pl.when(step == n_steps - 1)
    def _():
        out = pltpu.make_async_copy(kmbuf, kmean_ref, osem.at[0])
        out.start()
        out.wait()


def _moba_prompt(q, k, v, slopes, attn_norm, page_table, cache_k):
    b, t, _ = q.shape
    blk = MOBA_BLOCK
    nblk = t // blk
    assert t % blk == 0
    qpt = 2 if nblk % 2 == 0 else 1
    kpi = max(kk for kk in (4, 2, 1) if nblk % kk == 0)
    db, n_pages = page_table.shape
    _, page, nh, hd = cache_k.shape
    ppb = blk // page
    n_steps = b * N_HEADS * (nblk // qpt)
    pps = max(p for p in range(ppb, STREAM_PAGES + 1, ppb) if n_pages % p == 0)
    n_chunks = db * n_pages // pps
    cps = -(-n_chunks // n_steps)
    assert cps <= 2
    qw = qpt * blk
    seq = pl.BlockSpec((1, t, HEAD_DIM), lambda bi, h, qi, pt: (bi, 0, h))
    tile = pl.BlockSpec((1, qw, HEAD_DIM), lambda bi, h, qi, pt: (bi, qi, h))
    return pl.pallas_call(
        functools.partial(_moba_prompt_kernel, nblk=nblk, qpt=qpt, kpi=kpi, n_steps=n_steps, n_chunks=n_chunks,
                          cps=cps, pps=pps, n_pages=n_pages),
        grid_spec=pltpu.PrefetchScalarGridSpec(
            num_scalar_prefetch=1,
            grid=(b, N_HEADS, nblk // qpt),
            in_specs=[
                pl.BlockSpec(memory_space=pltpu.SMEM),
                seq, seq, seq,
                pl.BlockSpec((1, HEAD_DIM), lambda bi, h, qi, pt: (0, h)),
                pl.BlockSpec(memory_space=pl.ANY),
            ],
            out_specs=[tile, pl.BlockSpec(memory_space=pl.ANY)],
            scratch_shapes=[
                pltpu.VMEM((t, HEAD_DIM), BF16),
                pltpu.VMEM((t, HEAD_DIM), BF16),
                pltpu.VMEM((HEAD_DIM, t), BF16),
                pltpu.VMEM((nblk, SUBLANES, t), F32),
                pltpu.VMEM((kpi * blk, qw), F32),
                pltpu.VMEM((qw, qw), F32),
                pltpu.VMEM((2, pps, page, nh, hd), F32),
                pltpu.VMEM((db, n_pages // ppb, nh, hd), F32),
                pltpu.SemaphoreType.DMA((2,)),
                pltpu.SemaphoreType.DMA((1,)),
            ],
        ),
        out_shape=[jax.ShapeDtypeStruct((b, t, D_ATTN), BF16),
                   jax.ShapeDtypeStruct((db, n_pages // ppb, nh, hd), F32)],
        compiler_params=_cparams(("arbitrary", "arbitrary", "arbitrary")),
        name="moba_prompt",
    )(page_table.reshape(-1), slopes, q, k, v, attn_norm, cache_k)


def _sample_topk_kernel(q_ref, km_ref, idx_ref):
    nbf = km_ref.shape[1]
    s_len = q_ref.shape[2]
    gates = []
    for h in range(N_HEADS):
        km = jnp.concatenate([km_ref[0, :, h, :], jnp.zeros((LANES - nbf, HEAD_DIM), F32)], axis=0)
        gates.append(_nt_dot(q_ref[0, h], km, precision=lax.Precision.HIGHEST))
    gate = jnp.concatenate(gates, axis=0)
    lane = lax.broadcasted_iota(jnp.int32, gate.shape, 1)
    g = jnp.where(lane < nbf, gate, -jnp.inf)
    out = jnp.zeros(gate.shape, jnp.int32)
    for r in range(MOBA_TOPK):
        m = jnp.max(g, axis=-1, keepdims=True)
        idx = jnp.min(jnp.where(g == m, lane, LANES), axis=-1, keepdims=True)
        out = jnp.where(lane == r, idx, out)
        g = jnp.where(lane == idx, -jnp.inf, g)
    idx_ref[0] = out.reshape(N_HEADS, s_len, LANES)


def _sample_topk(q_t, kmean):
    db, nh, s, hd = q_t.shape
    nbf = kmean.shape[1]
    return pl.pallas_call(
        _sample_topk_kernel,
        grid=(db,),
        in_specs=[
            pl.BlockSpec((1, nh, s, hd), lambda i: (i, 0, 0, 0)),
            pl.BlockSpec((1, nbf, nh, hd), lambda i: (i, 0, 0, 0)),
        ],
        out_specs=pl.BlockSpec((1, nh, s, LANES), lambda i: (i, 0, 0, 0)),
        out_shape=jax.ShapeDtypeStruct((db, nh, s, LANES), jnp.int32),
        compiler_params=_cparams(("arbitrary",)),
        name="sample_topk",
    )(q_t, kmean)


def _moba_sample_kernel(pt_ref, blk_ref, slopes_ref, q_ref, kn_ref, vn_ref, nrm_ref,
                        glu_ref, w_ref, b_ref, gg_ref, gb_ref, ck_ref, cv_ref, o_ref, c_ref,
                        kbuf, vbuf, xs_ref, sem, *, n_units, n_sel, ppb, past, tt, rc, nt, cpu):
    page = ck_ref.shape[1]
    n_slab = n_sel * ppb
    n_keys = n_slab * page
    n_pages = past // page
    s_len = q_ref.shape[2]
    rows = 16

    def copies(u, slot, i, h, pg):
        dst = pl.ds(pl.multiple_of(i * page, page), page)
        return (pltpu.make_async_copy(ck_ref.at[pg, :, h, :], kbuf.at[slot, dst], sem.at[0, slot]),
                pltpu.make_async_copy(cv_ref.at[pg, :, h, :], vbuf.at[slot, dst], sem.at[1, slot]))

    def start(u, slot):
        h = u % N_HEADS
        row0 = (u // N_HEADS) * n_pages

        def sel(n, carry):
            logical = blk_ref[u * n_sel + n] * ppb
            for p in range(ppb):
                ck, cv = copies(u, slot, n * ppb + p, h, pt_ref[row0 + logical + p])
                ck.start()
                cv.start()
            return carry

        lax.fori_loop(0, n_sel, sel, 0)

    u = pl.program_id(0)

    @pl.when(u == 0)
    def _():
        start(0, 0)

    slot = u % 2
    bi = u // N_HEADS
    h = u % N_HEADS

    @pl.when(u + 1 < n_units)
    def _():
        start(u + 1, 1 - slot)

    for k in range(cpu):
        v = u * cpu + k
        cb = v % CONV_GROUPS
        tile = v // CONV_GROUPS

        @pl.when(cb == 0)
        def _():
            @pl.when(tile % nt == 0)
            def _():
                xs_ref[0, 0:HALO, :] = jnp.zeros((HALO, xs_ref.shape[2]), F32)

            xs_ref[0, HALO:HALO + tt, :] = glu_ref[0]
            _conv_preshift(xs_ref, tt)

        _conv_lane_block(xs_ref, w_ref, b_ref, gg_ref, gb_ref, c_ref, 0,
                         pl.ds(pl.multiple_of(cb * LANES, LANES), LANES), tt=tt, rc=rc, preshift=True)

        @pl.when(cb == CONV_GROUPS - 1)
        def _():
            xs_ref[0, 0:HALO, :] = xs_ref[0, tt:tt + HALO, :]

    def slab_done(i, carry):
        ck, cv = copies(u, slot, i, 0, 0)
        ck.wait()
        cv.wait()
        return carry

    lax.fori_loop(0, n_slab, slab_done, 0)

    slope = slopes_ref[h]
    q = q_ref[bi, h]
    qs = jnp.concatenate([q * (HEAD_DIM ** -0.5), jnp.zeros((rows - s_len, HEAD_DIM), F32)], axis=0)
    qs = qs.astype(BF16)
    kb = kbuf[slot].astype(BF16)
    vb = vbuf[slot].astype(BF16)
    s = _nt_dot(qs, kb)

    per_blk = page * ppb
    offs = lax.broadcasted_iota(jnp.int32, (1, per_blk), 1)
    kpos = jnp.concatenate(
        [(blk_ref[u * n_sel + i] * per_blk + offs).astype(F32) for i in range(n_sel)], axis=1)
    row = lax.broadcasted_iota(jnp.int32, (rows, 1), 0)
    qpos = (past + row).astype(F32)
    s = s - slope * (qpos - kpos)
    col = lax.broadcasted_iota(jnp.int32, (rows, n_keys), 1)
    per_q = MOBA_TOPK * per_blk
    mine = jnp.logical_and(col >= row * per_q, col < (row + 1) * per_q)
    s = jnp.where(mine, s, NEG_INF)

    kn = jnp.concatenate([kn_ref[bi, h], jnp.zeros((rows - s_len, HEAD_DIM), F32)], axis=0).astype(BF16)
    vn = jnp.concatenate([vn_ref[bi, h], jnp.zeros((rows - s_len, HEAD_DIM), F32)], axis=0).astype(BF16)
    ri = lax.broadcasted_iota(jnp.int32, (rows, rows), 0)
    ci = lax.broadcasted_iota(jnp.int32, (rows, rows), 1)
    so = _nt_dot(qs, kn) - slope * (ri - ci).astype(F32)
    so = jnp.where(jnp.logical_and(ri >= ci, ci < s_len), so, NEG_INF)

    m = jnp.maximum(jnp.max(s, axis=-1, keepdims=True), jnp.max(so, axis=-1, keepdims=True))
    p = jnp.exp(s - m)
    po = jnp.exp(so - m)
    l = jnp.sum(p, axis=-1, keepdims=True) + jnp.sum(po, axis=-1, keepdims=True)
    acc = (jnp.dot(p.astype(BF16), vb, preferred_element_type=F32)
           + jnp.dot(po.astype(BF16), vn, preferred_element_type=F32))
    o = (acc / l)[0:s_len]
    o = (o * lax.rsqrt(jnp.mean(o * o, axis=-1, keepdims=True) + EPS)) * nrm_ref[pl.ds(h, 1), :]
    o_ref[bi, h] = o


def _moba_sample(page_table_flat, blks, slopes, q_t, kn_t, vn_t, attn_norm_hd, cache_k, cache_v, past,
                 glu, w_dw, b_dw, gn_g, gn_b, tt):
    db, nh, s, hd = q_t.shape
    _, page, _, _ = cache_k.shape
    ppb = MOBA_BLOCK // page
    n_sel = s * MOBA_TOPK
    n_units = db * nh
    b, t, dc = glu.shape
    assert dc == CONV_GROUPS * LANES and t % tt == 0 and tt >= 4 * SUBLANES
    nt = t // tt
    n_conv_units = b * nt * CONV_GROUPS
    cpu = -(-n_conv_units // n_units)
    assert CONV_GROUPS % cpu == 0 and cpu * n_units == n_conv_units
    whole = lambda shp: pl.BlockSpec(shp, lambda i, *_: (0,) * len(shp))
    tile = pl.BlockSpec((1, tt, dc), lambda i, *_: ((i * cpu // CONV_GROUPS) // nt, (i * cpu // CONV_GROUPS) % nt, 0))
    return pl.pallas_call(
        functools.partial(_moba_sample_kernel, n_units=n_units, n_sel=n_sel, ppb=ppb, past=past,
                          tt=tt, rc=min(tt, 64), nt=nt, cpu=cpu),
        grid_spec=pltpu.PrefetchScalarGridSpec(
            num_scalar_prefetch=3,
            grid=(n_units,),
            in_specs=[
                whole(q_t.shape), whole(kn_t.shape), whole(vn_t.shape), whole(attn_norm_hd.shape),
                tile, whole(w_dw.shape), whole(b_dw.shape), whole(gn_g.shape), whole(gn_b.shape),
                pl.BlockSpec(memory_space=pl.ANY),
                pl.BlockSpec(memory_space=pl.ANY),
            ],
            out_specs=[whole(q_t.shape), tile],
            scratch_shapes=[
                pltpu.VMEM((2, n_sel * ppb * page, hd), F32),
                pltpu.VMEM((2, n_sel * ppb * page, hd), F32),
                pltpu.VMEM((SUBLANES, HALO + tt, dc), F32),
                pltpu.SemaphoreType.DMA((2, 2)),
            ],
        ),
        out_shape=[jax.ShapeDtypeStruct(q_t.shape, F32), jax.ShapeDtypeStruct((b, t, dc), BF16)],
        compiler_params=_cparams(("arbitrary",)),
        name="moba_sample_conv",
    )(page_table_flat, blks, slopes, q_t, kn_t, vn_t, attn_norm_hd, glu, w_dw, b_dw, gn_g, gn_b, cache_k, cache_v)


def _outproj_kernel(op_ref, cp_ref, xp_ref, os_ref, cs_ref, xs_ref, wo_ref, wc_ref, g_ref, wr_ref, br_ref,
                    x2_ref, h2_ref, route_ref, *, nb_p):
    i = pl.program_id(0)
    tail = (wo_ref, wc_ref, g_ref, wr_ref, br_ref, x2_ref, h2_ref, route_ref)

    @pl.when(i < nb_p)
    def _():
        _outproj_tile(op_ref, cp_ref, xp_ref, *tail)

    @pl.when(i >= nb_p)
    def _():
        _outproj_tile(os_ref, cs_ref, xs_ref, *tail)


def _outproj_tile(o_ref, c_ref, x_ref, wo_ref, wc_ref, g_ref, wr_ref, br_ref, x2_ref, h2_ref, route_ref):
    mix = (jnp.dot(o_ref[...].astype(BF16), wo_ref[...], preferred_element_type=F32)
           + jnp.dot(c_ref[...].astype(BF16), wc_ref[...], preferred_element_type=F32))
    x2 = x_ref[...] + mix
    x2_ref[...] = x2
    r = lax.rsqrt(jnp.mean(x2 * x2, axis=-1, keepdims=True) + EPS)
    h2 = (x2 * r) * g_ref[...]
    h2_ref[...] = h2

    h_hi = h2.astype(BF16)
    h_lo = (h2 - h_hi.astype(F32)).astype(BF16)
    r1 = jnp.dot(h_hi, wr_ref[...], preferred_element_type=F32)
    r2 = jnp.dot(h_lo, wr_ref[:, 0:LANES], preferred_element_type=F32)
    logits = r1[:, 0:LANES] + r1[:, LANES:2 * LANES] + r2 + br_ref[...]

    lane = lax.broadcasted_iota(jnp.int32, logits.shape, 1)
    is_g = lane < N_EXPERT_GROUPS
    gl = jnp.where(is_g, logits, -jnp.inf)
    gmax = jnp.max(gl, axis=-1, keepdims=True)
    gidx = jnp.min(jnp.where(gl == gmax, lane, LANES), axis=-1, keepdims=True)
    gsum = jnp.sum(jnp.where(is_g, jnp.exp(gl - gmax), 0.0), axis=-1, keepdims=True)
    g_gate = 1.0 / gsum

    lo = N_EXPERT_GROUPS + gidx * EXPERTS_PER_GROUP
    el = jnp.where(jnp.logical_and(lane >= lo, lane < lo + EXPERTS_PER_GROUP), logits, -jnp.inf)
    v0 = jnp.max(el, axis=-1, keepdims=True)
    i0 = jnp.min(jnp.where(el == v0, lane, LANES), axis=-1, keepdims=True)
    el = jnp.where(lane == i0, -jnp.inf, el)
    v1 = jnp.max(el, axis=-1, keepdims=True)
    i1 = jnp.min(jnp.where(el == v1, lane, LANES), axis=-1, keepdims=True)
    e = jnp.exp(v1 - v0)
    w0 = g_gate / (1.0 + e)
    w1 = g_gate * e / (1.0 + e)

    out = jnp.where(lane == 0, (i0 - N_EXPERT_GROUPS).astype(F32), 0.0)
    out = jnp.where(lane == 1, (i1 - N_EXPERT_GROUPS).astype(F32), out)
    out = jnp.where(lane == 2, w0, out)
    out = jnp.where(lane == 3, w1, out)
    route_ref[...] = out


def _out_projection(o_p, c_p, x_p, o_s, c_s, x_s, wo, wc, norm_g, wr, br, tm):
    np_, d = x_p.shape
    ns_ = x_s.shape[0]
    assert np_ % tm == 0 and ns_ % tm == 0
    nb_p, nb_s = np_ // tm, ns_ // tm
    n = np_ + ns_
    prow = lambda w: pl.BlockSpec((tm, w), lambda i: (jnp.minimum(i, nb_p - 1), 0))
    srow = lambda w: pl.BlockSpec((tm, w), lambda i: (jnp.maximum(i - nb_p, 0), 0))
    row = lambda w: pl.BlockSpec((tm, w), lambda i: (i, 0))
    full = lambda a: pl.BlockSpec(a.shape, lambda i: (0, 0))
    return pl.pallas_call(
        functools.partial(_outproj_kernel, nb_p=nb_p),
        grid=(nb_p + nb_s,),
        in_specs=[prow(o_p.shape[1]), prow(c_p.shape[1]), prow(d), srow(o_s.shape[1]), srow(c_s.shape[1]), srow(d),
                  full(wo), full(wc), full(norm_g), full(wr), full(br)],
        out_specs=[row(d), row(d), row(LANES)],
        out_shape=[jax.ShapeDtypeStruct((n, d), F32), jax.ShapeDtypeStruct((n, d), F32),
                   jax.ShapeDtypeStruct((n, LANES), F32)],
        compiler_params=_cparams(("arbitrary",)),
        name="out_projection_router",
    )(o_p, c_p, x_p, o_s, c_s, x_s, wo, wc, norm_g, wr, br)


DMA_UNROLL = SUBLANES


def _for_rows(n, fn):
    def group(gi, carry):
        for k in range(DMA_UNROLL):
            fn(gi * DMA_UNROLL + k, gi, k)
        return carry

    def single(i, carry):
        fn(i, lax.div(i, DMA_UNROLL), lax.rem(i, DMA_UNROLL))
        return carry

    n_grp = lax.div(n, DMA_UNROLL)
    lax.fori_loop(0, n_grp, group, 0)
    lax.fori_loop(n_grp * DMA_UNROLL, n, single, 0)


def _moe_kernel(order_ref, cexp_ref, cbase_ref, cnv_ref, h_ref, wg_ref, wu_ref, wd_ref, out_ref,
                xbuf, obuf, wgb, wub, wdb, gsem, ssem, *, n_tok, n_chunks):
    g = pl.program_id(0)
    slot = g % 2
    _, n_grp, grp, d = xbuf.shape

    def gather(chunk, buf_slot):
        base = cbase_ref[chunk]

        def desc(i, gi, k):
            tok = lax.shift_right_logical(order_ref[base + i], 1)
            return pltpu.make_async_copy(h_ref.at[pl.ds(tok, 1)], xbuf.at[buf_slot, gi, pl.ds(k, 1)],
                                         gsem.at[buf_slot])
        return desc

    def scatter(chunk, buf_slot):
        base = cbase_ref[chunk]

        def desc(i, gi, k):
            pair = order_ref[base + i]
            row = (pair & 1) * n_tok + lax.shift_right_logical(pair, 1)
            return pltpu.make_async_copy(obuf.at[buf_slot, gi, pl.ds(k, 1)], out_ref.at[pl.ds(row, 1)],
                                         ssem.at[buf_slot])
        return desc

    def gather_done(buf_slot):
        pltpu.make_async_copy(h_ref.at[pl.ds(0, 1)], xbuf.at[buf_slot, 0, pl.ds(0, 1)], gsem.at[buf_slot]).wait()

    def scatter_done(buf_slot):
        pltpu.make_async_copy(obuf.at[buf_slot, 0, pl.ds(0, 1)], out_ref.at[pl.ds(0, 1)], ssem.at[buf_slot]).wait()

    @pl.when(g == 0)
    def _():
        xbuf[...] = jnp.zeros(xbuf.shape, xbuf.dtype)
        _for_rows(cnv_ref[0], lambda i, gi, k: gather(0, 0)(i, gi, k).start())

    @pl.when(jnp.logical_or(g == 0, cexp_ref[g] != cexp_ref[jnp.maximum(g - 1, 0)]))
    def _():
        wgb[...] = wg_ref[0].astype(BF16)
        wub[...] = wu_ref[0].astype(BF16)
        wdb[...] = wd_ref[0].astype(BF16)

    @pl.when(g + 1 < n_chunks)
    def _():
        _for_rows(cnv_ref[g + 1], lambda i, gi, k: gather(g + 1, 1 - slot)(i, gi, k).start())

    nv = cnv_ref[g]
    _for_rows(nv, lambda i, gi, k: gather_done(slot))

    @pl.when(g >= 2)
    def _():
        _for_rows(cnv_ref[g - 2], lambda i, gi, k: scatter_done(slot))

    def expert_mlp(n_g):
        x = xbuf[slot, 0:n_g].reshape(n_g * grp, d).astype(BF16)
        gt = jnp.dot(x, wgb[...], preferred_element_type=F32)
        up = jnp.dot(x, wub[...], preferred_element_type=F32)
        hid = (gt * (1.0 / (1.0 + jnp.exp(-gt)))) * up
        y = jnp.dot(hid.astype(BF16), wdb[...], preferred_element_type=F32)
        obuf[slot, 0:n_g] = y.reshape(n_g, grp, d)

    sizes = sorted({n_grp, n_grp // 2, max(n_grp // 6, 1)})
    lo = 0
    for n_g in sizes:
        pl.when(jnp.logical_and(nv > lo * grp, nv <= n_g * grp))(functools.partial(expert_mlp, n_g))
        lo = n_g

    _for_rows(nv, lambda i, gi, k: scatter(g, slot)(i, gi, k).start())

    @pl.when(g == n_chunks - 1)
    def _():
        if n_chunks >= 2:
            _for_rows(cnv_ref[g - 1], lambda i, gi, k: scatter_done(1 - slot))
        _for_rows(nv, lambda i, gi, k: scatter_done(slot))


def _moe(order, cexp, cbase, cnv, h2, w_gate, w_up, w_down, ch):
    n, d = h2.shape
    ne, _, de = w_gate.shape
    n_chunks = cexp.shape[0]
    wspec = lambda shp: pl.BlockSpec((1,) + shp, lambda g, order, cexp, cbase, cnv: (cexp[g], 0, 0))
    return pl.pallas_call(
        functools.partial(_moe_kernel, n_tok=n, n_chunks=n_chunks),
        grid_spec=pltpu.PrefetchScalarGridSpec(
            num_scalar_prefetch=4,
            grid=(n_chunks,),
            in_specs=[
                pl.BlockSpec(memory_space=pl.ANY),
                wspec((d, de)), wspec((d, de)), wspec((de, d)),
            ],
            out_specs=pl.BlockSpec(memory_space=pl.ANY),
            scratch_shapes=[
                pltpu.VMEM((2, ch // DMA_UNROLL, DMA_UNROLL, d), F32),
                pltpu.VMEM((2, ch // DMA_UNROLL, DMA_UNROLL, d), F32),
                pltpu.VMEM((d, de), BF16),
                pltpu.VMEM((d, de), BF16),
                pltpu.VMEM((de, d), BF16),
                pltpu.SemaphoreType.DMA((2,)),
                pltpu.SemaphoreType.DMA((2,)),
            ],
        ),
        out_shape=jax.ShapeDtypeStruct((2 * n, d), F32),
        compiler_params=_cparams(("arbitrary",)),
        name="expert_mlp",
    )(order, cexp, cbase, cnv, h2, w_gate, w_up, w_down)


def _expert_chunks(eid, ch):
    n_pairs = eid.shape[0]
    n_chunks = N_EXPERTS + n_pairs // ch
    order = jnp.argsort(eid, stable=True).astype(jnp.int32)
    counts = jnp.sum(eid[:, None] == jnp.arange(N_EXPERTS, dtype=jnp.int32)[None, :], axis=0, dtype=jnp.int32)
    offs = jnp.cumsum(counts, dtype=jnp.int32) - counts
    nch = (counts + ch - 1) // ch
    cum = jnp.cumsum(nch, dtype=jnp.int32)
    gidx = jnp.arange(n_chunks, dtype=jnp.int32)
    cexp_raw = jnp.sum(gidx[:, None] >= cum[None, :], axis=1, dtype=jnp.int32)
    last = jnp.maximum(jnp.max(jnp.where(nch > 0, jnp.arange(N_EXPERTS, dtype=jnp.int32), 0)), 0)
    used = gidx < cum[-1]
    cexp = jnp.where(used, jnp.minimum(cexp_raw, N_EXPERTS - 1), last)
    local = gidx - (cum[cexp] - nch[cexp])
    cbase = jnp.where(used, offs[cexp] + local * ch, 0)
    cnv = jnp.where(used, jnp.clip(counts[cexp] - local * ch, 0, ch), 0)
    return order, cexp.astype(jnp.int32), cbase.astype(jnp.int32), cnv.astype(jnp.int32)


def _final_kernel(x2_ref, ya_ref, yb_ref, route_ref, g_ref, yp_ref, ys_ref, *, nb_p):
    i = pl.program_id(0)
    rt = route_ref[...]
    x3 = x2_ref[...] + rt[:, 2:3] * ya_ref[...] + rt[:, 3:4] * yb_ref[...]
    r = lax.rsqrt(jnp.mean(x3 * x3, axis=-1, keepdims=True) + EPS)
    y = (x3 * r) * g_ref[...]

    @pl.when(i < nb_p)
    def _():
        yp_ref[...] = y

    @pl.when(i >= nb_p)
    def _():
        ys_ref[...] = y


def _final(x2, y2, route, norm_g, np_, tm):
    n, d = x2.shape
    assert np_ % tm == 0 and n % tm == 0
    nb, nb_p = n // tm, np_ // tm
    return pl.pallas_call(
        functools.partial(_final_kernel, nb_p=nb_p),
        grid=(nb,),
        in_specs=[
            pl.BlockSpec((tm, d), lambda i: (i, 0)),
            pl.BlockSpec((tm, d), lambda i: (i, 0)),
            pl.BlockSpec((tm, d), lambda i: (i + nb, 0)),
            pl.BlockSpec((tm, LANES), lambda i: (i, 0)),
            pl.BlockSpec((1, d), lambda i: (0, 0)),
        ],
        out_specs=[
            pl.BlockSpec((tm, d), lambda i: (jnp.minimum(i, nb_p - 1), 0)),
            pl.BlockSpec((tm, d), lambda i: (jnp.maximum(i - nb_p, 0), 0)),
        ],
        out_shape=[jax.ShapeDtypeStruct((np_, d), F32), jax.ShapeDtypeStruct((n - np_, d), F32)],
        compiler_params=_cparams(("arbitrary",)),
        name="combine_final_norm",
    )(x2, y2, y2, route, norm_g)


def _pick_tile(n, pref):
    t = min(n, pref)
    while n % t:
        t //= 2
    return t


def kernel(x_prompt, x_sample, cache_k, cache_v, state_conv, page_table, norm_mix, w_in, w_dw, b_dw,
           conv_norm_g, conv_norm_b, attn_out_norm, w_out, norm_ffn, w_router_group, b_router_group,
           w_router_expert, b_router_expert, w_gate, w_up, w_down, norm_final):
    depth = norm_mix.shape[0]
    assert depth == 1
    b, t, d = x_prompt.shape
    db, s, _ = x_sample.shape
    n_pages = page_table.shape[1]
    page = cache_k.shape[2]
    past = n_pages * page
    nbf = past // MOBA_BLOCK
    assert past % MOBA_BLOCK == 0 and nbf >= MOBA_TOPK and nbf <= LANES
    d_conv = d - D_ATTN
    assert d_conv == D_ATTN
    np_, ns_ = b * t, db * s

    slopes = jnp.exp2(-8.0 * jnp.arange(1, N_HEADS + 1, dtype=F32) / N_HEADS)
    w_in_bf = w_in[0].astype(BF16)
    wo_bf = w_out[0, :D_ATTN].astype(BF16)
    wc_bf = w_out[0, D_ATTN:].astype(BF16)
    wr = jnp.zeros((d, LANES), F32)
    wr = wr.at[:, :N_EXPERT_GROUPS].set(w_router_group[0])
    wr = wr.at[:, N_EXPERT_GROUPS:N_EXPERT_GROUPS + N_EXPERTS].set(w_router_expert[0])
    wr_hi = wr.astype(BF16)
    wr_lo = (wr - wr_hi.astype(F32)).astype(BF16)
    wr_cat = jnp.concatenate([wr_hi, wr_lo], axis=1)
    br = jnp.zeros((1, LANES), F32)
    br = br.at[0, :N_EXPERT_GROUPS].set(b_router_group[0])
    br = br.at[0, N_EXPERT_GROUPS:N_EXPERT_GROUPS + N_EXPERTS].set(b_router_expert[0])
    g_mix = norm_mix[0].reshape(1, d)
    g_ffn = norm_ffn[0].reshape(1, d)
    g_fin = norm_final.reshape(1, d)
    attn_norm = attn_out_norm[0].reshape(1, D_ATTN)
    conv_vec = lambda a: a[0].reshape(1, d_conv)

    xp = x_prompt.reshape(np_, d)
    xs = x_sample.reshape(ns_, d)
    qp, kp, vp, glu_p = _in_projection(xp, g_mix, w_in_bf, _pick_tile(np_, 512))
    qs, ks, vs, glu_s = _in_projection(xs, g_mix, w_in_bf, _pick_tile(ns_, 256))

    glu_p3 = glu_p.reshape(b, t, d_conv)
    glu_s3 = glu_s.reshape(db, s, d_conv)
    conv_args = (w_dw[0], conv_vec(b_dw), conv_vec(conv_norm_g), conv_vec(conv_norm_b))
    c_s = _conformer_conv(glu_s3, state_conv[0], *conv_args, tt=s, nseq=db, out_dtype=F32)
    conv_prompt = glu_p3[:, t - (CONV_WIDTH - 1):][None]
    conv_sample = jnp.concatenate([state_conv[0], glu_s3], axis=1)[:, -(CONV_WIDTH - 1):][None]

    ck = cache_k[0]
    cv = cache_v[0]
    o_p, kmean = _moba_prompt(qp.reshape(b, t, D_ATTN), kp.reshape(b, t, D_ATTN), vp.reshape(b, t, D_ATTN),
                              slopes, attn_norm, page_table, ck)

    to_heads = lambda a: a.reshape(db, s, N_HEADS, HEAD_DIM).transpose(0, 2, 1, 3)
    q_t, kn_t, vn_t = to_heads(qs), to_heads(ks), to_heads(vs)
    idx = _sample_topk(q_t, kmean)[..., :MOBA_TOPK]
    o_t, c_p = _moba_sample(page_table.reshape(-1), idx.reshape(-1), slopes, q_t, kn_t, vn_t,
                            attn_norm.reshape(N_HEADS, HEAD_DIM), ck, cv, past,
                            glu_p3, *conv_args, tt=_pick_tile(t, 256))
    o_s = o_t.transpose(0, 2, 1, 3).reshape(ns_, D_ATTN)

    tm = _pick_tile(ns_, 256)
    x2, h2, route = _out_projection(o_p.reshape(np_, D_ATTN), c_p.reshape(np_, d_conv), xp,
                                    o_s, c_s.reshape(ns_, d_conv), xs,
                                    wo_bf, wc_bf, g_ffn, wr_cat, br, tm)

    ch = 384
    eid = route[:, 0:2].astype(jnp.int32).reshape(-1)
    order, cexp, cbase, cnv = _expert_chunks(eid, ch)
    y2 = _moe(order, cexp, cbase, cnv, h2, w_gate[0], w_up[0], w_down[0], ch)

    y_p, y_s = _final(x2, y2, route, g_fin, np_, tm)

    page_shape = (depth, b, t // page, page, N_HEADS, HEAD_DIM)
    return (y_p.reshape(b, t, d), y_s.reshape(db, s, d),
            kp.reshape(page_shape), vp.reshape(page_shape), conv_prompt,
            ks.reshape(depth, db, s, N_HEADS, HEAD_DIM), vs.reshape(depth, db, s, N_HEADS, HEAD_DIM),
            conv_sample)
```

```python
import functools

import jax
import jax.numpy as jnp
from jax import lax
from jax.experimental import pallas as pl
from jax.experimental.pallas import tpu as pltpu

F32 = jnp.float32
BF16 = jnp.bfloat16

N_HEADS = 8
HEAD_DIM = 128
D_ATTN = N_HEADS * HEAD_DIM
CONV_GROUPS = 8
CONV_WIDTH = 31
MOBA_BLOCK = 256
MOBA_TOPK = 3
N_EXPERT_GROUPS = 4
EXPERTS_PER_GROUP = 8
N_EXPERTS = N_EXPERT_GROUPS * EXPERTS_PER_GROUP
EPS = 1e-6
NEG_INF = -1e30

LANES = 128
SUBLANES = 8
HALO = 32
VMEM_LIMIT = 56 * 1024 * 1024
STREAM_PAGES = 16


def _cparams(sem=None):
    return pltpu.CompilerParams(dimension_semantics=sem, vmem_limit_bytes=VMEM_LIMIT)


def _nt_dot(a, b, precision=None):
    return lax.dot_general(a, b, (((1,), (1,)), ((), ())), precision=precision,
                           preferred_element_type=F32)


def _inproj_kernel(x_ref, g_ref, w_ref, wgate_ref, q_ref, k_ref, v_ref, glu_ref, h_ref):
    j = pl.program_id(1)

    @pl.when(j == 0)
    def _():
        x = x_ref[...]
        r = lax.rsqrt(jnp.mean(x * x, axis=-1, keepdims=True) + EPS)
        h_ref[...] = ((x * r) * g_ref[...]).astype(BF16)

    z = jnp.dot(h_ref[...], w_ref[...], preferred_element_type=F32)

    @pl.when(j == 0)
    def _():
        q_ref[...] = z

    @pl.when(j == 1)
    def _():
        k_ref[...] = z

    @pl.when(j == 2)
    def _():
        v_ref[...] = z

    @pl.when(j == 3)
    def _():
        gate = jnp.dot(h_ref[...], wgate_ref[...], preferred_element_type=F32)
        glu_ref[...] = z * (1.0 / (1.0 + jnp.exp(-gate)))


def _in_projection(x, norm_g, w_in_bf, tm):
    n, d = x.shape
    dc = D_ATTN
    out = jax.ShapeDtypeStruct((n, dc), F32)
    row = pl.BlockSpec((tm, dc), lambda i, j: (i, 0))
    return pl.pallas_call(
        _inproj_kernel,
        grid=(n // tm, 4),
        in_specs=[
            pl.BlockSpec((tm, d), lambda i, j: (i, 0)),
            pl.BlockSpec((1, d), lambda i, j: (0, 0)),
            pl.BlockSpec((d, dc), lambda i, j: (0, j)),
            pl.BlockSpec((d, dc), lambda i, j: (0, 4)),
        ],
        out_specs=[row, row, row, row],
        out_shape=[out, out, out, out],
        scratch_shapes=[pltpu.VMEM((tm, d), BF16)],
        compiler_params=_cparams(("arbitrary", "arbitrary")),
        name="in_projection",
    )(x, norm_g, w_in_bf, w_in_bf)


def _conv_lane_block(xs_ref, w_ref, b_ref, gg_ref, gb_ref, c_ref, sq, cs, *, tt, rc, preshift):
    off = HALO - (CONV_WIDTH - 1)
    for r0 in range(0, tt, rc):
        acc = jnp.broadcast_to(b_ref[:, cs], (rc, LANES))
        for j in range(CONV_WIDTH):
            o = off + j
            if preshift:
                lo = r0 + (o // SUBLANES) * SUBLANES
                x = xs_ref[o % SUBLANES, lo:lo + rc, cs]
            else:
                x = xs_ref[0, r0 + o:r0 + o + rc, cs]
            acc = acc + x * w_ref[j:j + 1, cs]
        mu = jnp.mean(acc, axis=-1, keepdims=True)
        dlt = acc - mu
        var = jnp.mean(dlt * dlt, axis=-1, keepdims=True)
        yn = dlt * lax.rsqrt(var + EPS)
        yn = yn * gg_ref[:, cs] + gb_ref[:, cs]
        c_ref[sq, r0:r0 + rc, cs] = (yn * (1.0 / (1.0 + jnp.exp(-yn)))).astype(c_ref.dtype)


def _conv_preshift(xs_ref, tt):
    span = HALO + tt - SUBLANES
    for r in range(1, SUBLANES):
        xs_ref[r, 0:span, :] = xs_ref[0, r:r + span, :]


def _conv_kernel(glu_ref, st_ref, w_ref, b_ref, gg_ref, gb_ref, c_ref, xs_ref, *, tt, rc, nseq, carry, preshift):
    t = pl.program_id(1)

    def lane_block(sq, cs):
        _conv_lane_block(xs_ref, w_ref, b_ref, gg_ref, gb_ref, c_ref, sq, cs, tt=tt, rc=rc, preshift=preshift)

    hist = HALO - (CONV_WIDTH - 1)

    def load_state(sq):
        xs_ref[0, 0:hist, :] = jnp.zeros((hist, xs_ref.shape[2]), F32)
        xs_ref[0, hist:HALO, :] = st_ref[sq]

    def one_seq(sq):
        if carry:
            pl.when(t == 0)(functools.partial(load_state, sq))
        else:
            load_state(sq)
        xs_ref[0, HALO:HALO + tt, :] = glu_ref[sq]
        if preshift:
            _conv_preshift(xs_ref, tt)

            def group(cb, carry_):
                lane_block(sq, pl.ds(pl.multiple_of(cb * LANES, LANES), LANES))
                return carry_

            lax.fori_loop(0, CONV_GROUPS, group, 0)
        else:
            for cb in range(CONV_GROUPS):
                lane_block(sq, pl.ds(cb * LANES, LANES))
        if carry:
            xs_ref[0, 0:HALO, :] = xs_ref[0, tt:tt + HALO, :]

    if nseq == 1:
        one_seq(0)
    else:
        def seq(sq, carry_):
            one_seq(sq)
            return carry_

        lax.fori_loop(0, nseq, seq, 0)


def _conformer_conv(glu, state, w_dw, b_dw, gn_g, gn_b, tt, nseq, out_dtype):
    b, t, dc = glu.shape
    assert dc == CONV_GROUPS * LANES and b % nseq == 0 and t % tt == 0 and state.shape[1] == CONV_WIDTH - 1
    nt = t // tt
    assert nt == 1 or nseq == 1
    preshift = tt >= 4 * SUBLANES
    rc = min(tt, 64)
    vec = pl.BlockSpec((1, dc), lambda i, j: (0, 0))
    return pl.pallas_call(
        functools.partial(_conv_kernel, tt=tt, rc=rc, nseq=nseq, carry=nt > 1, preshift=preshift),
        grid=(b // nseq, nt),
        in_specs=[
            pl.BlockSpec((nseq, tt, dc), lambda i, j: (i, j, 0)),
            pl.BlockSpec((nseq, CONV_WIDTH - 1, dc), lambda i, j: (i, 0, 0)),
            pl.BlockSpec((CONV_WIDTH, dc), lambda i, j: (0, 0)),
            vec, vec, vec,
        ],
        out_specs=pl.BlockSpec((nseq, tt, dc), lambda i, j: (i, j, 0)),
        out_shape=jax.ShapeDtypeStruct((b, t, dc), out_dtype),
        scratch_shapes=[pltpu.VMEM((SUBLANES if preshift else 1, HALO + tt, dc), F32)],
        compiler_params=_cparams(("arbitrary", "arbitrary")),
        name="conformer_conv",
    )(glu, state, w_dw, b_dw, gn_g, gn_b)


def _moba_prompt_kernel(pt_ref, slopes_ref, q_ref, k_ref, v_ref, nrm_ref, ck_ref, o_ref, kmean_ref,
                        qb_ref, kb_ref, vt_ref, g_ref, biasp_ref, biasd_ref, pbuf, kmbuf, psem, osem,
                        *, nblk, qpt, kpi, n_steps, n_chunks, cps, pps, n_pages):
    h = pl.program_id(1)
    qi = pl.program_id(2)
    blk = MOBA_BLOCK
    qw = qpt * blk
    t = q_ref.shape[1]
    nb8 = -(-nblk // SUBLANES) * SUBLANES
    slope = slopes_ref[h]

    step = (pl.program_id(0) * N_HEADS + h) * (nblk // qpt) + qi
    c0 = step * cps
    exact = n_chunks == cps * n_steps
    ppb = blk // pbuf.shape[2]

    def start_chunk(chunk):
        slot = chunk % 2
        for i in range(pps):
            pltpu.make_async_copy(ck_ref.at[pt_ref[chunk * pps + i]], pbuf.at[slot, i], psem.at[slot]).start()

    def finish_chunk(chunk):
        slot = chunk % 2
        for i in range(pps):
            pltpu.make_async_copy(ck_ref.at[0], pbuf.at[slot, i], psem.at[slot]).wait()
        bi = (chunk * pps) // n_pages
        blk0 = ((chunk * pps) % n_pages) // ppb
        for i in range(pps // ppb):
            s = jnp.sum(pbuf[slot, ppb * i], axis=0)
            for p in range(1, ppb):
                s = s + jnp.sum(pbuf[slot, ppb * i + p], axis=0)
            kmbuf[bi, blk0 + i] = s * (1.0 / blk)

    def if_real(chunk, fn, always):
        if always:
            fn(chunk)
        else:
            pl.when(chunk < n_chunks)(functools.partial(fn, chunk))

    @pl.when(step == 0)
    def _():
        start_chunk(0)

    if_real(c0 + 1, start_chunk, exact and cps == 2)

    @pl.when(qi == 0)
    def _():
        q = q_ref[0]
        kf = k_ref[0]
        qb_ref[...] = (q * (HEAD_DIM ** -0.5)).astype(BF16)
        kb_ref[...] = kf.astype(BF16)
        vt_ref[...] = v_ref[0].T.astype(BF16)
        km = jnp.sum(kf.reshape(nblk, blk, HEAD_DIM), axis=1) * (1.0 / blk)
        if nb8 > nblk:
            km = jnp.concatenate([km, jnp.zeros((nb8 - nblk, HEAD_DIM), F32)], axis=0)
        gate = _nt_dot(km, q, precision=lax.Precision.HIGHEST)
        j_io = lax.broadcasted_iota(jnp.int32, gate.shape, 0)
        t_io = lax.broadcasted_iota(jnp.int32, gate.shape, 1)
        valid = j_io * blk + (blk - 1) < t_io - (t_io % blk)
        g = jnp.where(valid, gate, NEG_INF)
        sel = jnp.zeros(gate.shape, jnp.bool_)
        for _ in range(MOBA_TOPK):
            m = jnp.max(g, axis=0, keepdims=True)
            idx = jnp.min(jnp.where(g == m, j_io, nb8), axis=0, keepdims=True)
            pick = j_io == idx
            sel = jnp.logical_or(sel, pick)
            g = jnp.where(pick, -jnp.inf, g)
        gb = (slope * blk) * j_io.astype(F32) + jnp.where(jnp.logical_and(sel, valid), 0.0, NEG_INF)
        for j in range(nblk):
            g_ref[j] = jnp.broadcast_to(gb[j:j + 1, :], (SUBLANES, t))
        c = lax.broadcasted_iota(jnp.int32, (kpi * blk, qw), 0)
        biasp_ref[...] = slope * (c % blk).astype(F32)
        c = lax.broadcasted_iota(jnp.int32, (qw, qw), 0)
        r = lax.broadcasted_iota(jnp.int32, (qw, qw), 1)
        hidden = jnp.logical_or(c // blk > r // blk, jnp.logical_and(c // blk == r // blk, c % blk > r % blk))
        biasd_ref[...] = slope * (c % blk).astype(F32) + jnp.where(hidden, NEG_INF, 0.0)

    own = pl.ds(pl.multiple_of(qi * qw, qw), qw)
    qs = qb_ref[own, :]
    lane_blk = lax.broadcasted_iota(jnp.int32, (1, qw), 1) // blk

    def softmax_terms(st, gs, m_prev):
        segs = [st[i * blk:(i + 1) * blk] for i in range(len(gs))]
        m_new = m_prev
        for seg, g in zip(segs, gs):
            cand = jnp.max(seg, axis=0, keepdims=True) + g
            m_new = cand if m_new is None else jnp.maximum(m_new, cand)
        p = jnp.concatenate([jnp.exp(seg - (m_new - g)) for seg, g in zip(segs, gs)], axis=0)
        return m_new, p

    st = _nt_dot(kb_ref[own, :], qs) + biasd_ref[...]
    gs = []
    for i in range(qpt):
        g_blk = (slope * blk) * (qi * qpt + i).astype(F32)
        gs.append(jnp.where(lane_blk <= i, g_blk, g_ref[qi * qpt + i, 0:1, own]))
    m0, p = softmax_terms(st, gs, None)
    l0 = jnp.sum(p, axis=0, keepdims=True)
    acc0 = jnp.dot(vt_ref[:, own], p.astype(BF16), preferred_element_type=F32)

    n_past = qi * qpt

    def past_blocks(jp, carry):
        m, l, acc = carry
        rows = pl.ds(pl.multiple_of(jp * (kpi * blk), kpi * blk), kpi * blk)
        st = _nt_dot(kb_ref[rows, :], qs) + biasp_ref[...]
        gs = [jnp.where(jp * kpi + i < n_past, g_ref[jp * kpi + i, 0:1, own], NEG_INF) for i in range(kpi)]
        m_new, p = softmax_terms(st, gs, m)
        alpha = jnp.exp(m - m_new)
        l = alpha * l + jnp.sum(p, axis=0, keepdims=True)
        acc = alpha * acc + jnp.dot(vt_ref[:, rows], p.astype(BF16), preferred_element_type=F32)
        return m_new, l, acc

    _, l, acc = lax.fori_loop(0, (n_past + kpi - 1) // kpi, past_blocks, (m0, l0, acc0))
    ot = acc * (1.0 / l)
    ot = ot * lax.rsqrt(jnp.mean(ot * ot, axis=0, keepdims=True) + EPS)
    o_ref[0] = (ot.T * nrm_ref[...]).astype(o_ref.dtype)

    if_real(c0, finish_chunk, exact)
    if cps == 2:
        if_real(c0 + 2, start_chunk, False)
        if_real(c0 + 1, finish_chunk, exact)

    @pl.when(step == n_steps - 1)
    def _():
        out = pltpu.make_async_copy(kmbuf, kmean_ref, osem.at[0])
        out.start()
        out.wait()


def _moba_prompt(q, k, v, slopes, attn_norm, page_table, cache_k):
    b, t, _ = q.shape
    blk = MOBA_BLOCK
    nblk = t // blk
    assert t % blk == 0
    qpt = 2 if nblk % 2 == 0 else 1
    kpi = max(kk for kk in (4, 2, 1) if nblk % kk == 0)
    db, n_pages = page_table.shape
    _, page, nh, hd = cache_k.shape
    ppb = blk // page
    n_steps = b * N_HEADS * (nblk // qpt)
    pps = max(p for p in range(ppb, STREAM_PAGES + 1, ppb) if n_pages % p == 0)
    n_chunks = db * n_pages // pps
    cps = -(-n_chunks // n_steps)
    assert cps <= 2
    qw = qpt * blk
    seq = pl.BlockSpec((1, t, HEAD_DIM), lambda bi, h, qi, pt: (bi, 0, h))
    tile = pl.BlockSpec((1, qw, HEAD_DIM), lambda bi, h, qi, pt: (bi, qi, h))
    return pl.pallas_call(
        functools.partial(_moba_prompt_kernel, nblk=nblk, qpt=qpt, kpi=kpi, n_steps=n_steps, n_chunks=n_chunks,
                          cps=cps, pps=pps, n_pages=n_pages),
        grid_spec=pltpu.PrefetchScalarGridSpec(
            num_scalar_prefetch=1,
            grid=(b, N_HEADS, nblk // qpt),
            in_specs=[
                pl.BlockSpec(memory_space=pltpu.SMEM),
                seq, seq, seq,
                pl.BlockSpec((1, HEAD_DIM), lambda bi, h, qi, pt: (0, h)),
                pl.BlockSpec(memory_space=pl.ANY),
            ],
            out_specs=[tile, pl.BlockSpec(memory_space=pl.ANY)],
            scratch_shapes=[
                pltpu.VMEM((t, HEAD_DIM), BF16),
                pltpu.VMEM((t, HEAD_DIM), BF16),
                pltpu.VMEM((HEAD_DIM, t), BF16),
                pltpu.VMEM((nblk, SUBLANES, t), F32),
                pltpu.VMEM((kpi * blk, qw), F32),
                pltpu.VMEM((qw, qw), F32),
                pltpu.VMEM((2, pps, page, nh, hd), F32),
                pltpu.VMEM((db, n_pages // ppb, nh, hd), F32),
                pltpu.SemaphoreType.DMA((2,)),
                pltpu.SemaphoreType.DMA((1,)),
            ],
        ),
        out_shape=[jax.ShapeDtypeStruct((b, t, D_ATTN), BF16),
                   jax.ShapeDtypeStruct((db, n_pages // ppb, nh, hd), F32)],
        compiler_params=_cparams(("arbitrary", "arbitrary", "arbitrary")),
        name="moba_prompt",
    )(page_table.reshape(-1), slopes, q, k, v, attn_norm, cache_k)


def _sample_topk_kernel(q_ref, km_ref, idx_ref):
    nbf = km_ref.shape[1]
    s_len = q_ref.shape[2]
    gates = []
    for h in range(N_HEADS):
        km = jnp.concatenate([km_ref[0, :, h, :], jnp.zeros((LANES - nbf, HEAD_DIM), F32)], axis=0)
        gates.append(_nt_dot(q_ref[0, h], km, precision=lax.Precision.HIGHEST))
    gate = jnp.concatenate(gates, axis=0)
    lane = lax.broadcasted_iota(jnp.int32, gate.shape, 1)
    g = jnp.where(lane < nbf, gate, -jnp.inf)
    out = jnp.zeros(gate.shape, jnp.int32)
    for r in range(MOBA_TOPK):
        m = jnp.max(g, axis=-1, keepdims=True)
        idx = jnp.min(jnp.where(g == m, lane, LANES), axis=-1, keepdims=True)
        out = jnp.where(lane == r, idx, out)
        g = jnp.where(lane == idx, -jnp.inf, g)
    idx_ref[0] = out.reshape(N_HEADS, s_len, LANES)


def _sample_topk(q_t, kmean):
    db, nh, s, hd = q_t.shape
    nbf = kmean.shape[1]
    return pl.pallas_call(
        _sample_topk_kernel,
        grid=(db,),
        in_specs=[
            pl.BlockSpec((1, nh, s, hd), lambda i: (i, 0, 0, 0)),
            pl.BlockSpec((1, nbf, nh, hd), lambda i: (i, 0, 0, 0)),
        ],
        out_specs=pl.BlockSpec((1, nh, s, LANES), lambda i: (i, 0, 0, 0)),
        out_shape=jax.ShapeDtypeStruct((db, nh, s, LANES), jnp.int32),
        compiler_params=_cparams(("arbitrary",)),
        name="sample_topk",
    )(q_t, kmean)


def _moba_sample_kernel(pt_ref, blk_ref, slopes_ref, q_ref, kn_ref, vn_ref, nrm_ref,
                        glu_ref, w_ref, b_ref, gg_ref, gb_ref, ck_ref, cv_ref, o_ref, c_ref,
                        kbuf, vbuf, xs_ref, sem, *, n_units, n_sel, ppb, past, tt, rc, nt, cpu):
    page = ck_ref.shape[1]
    n_slab = n_sel * ppb
    n_keys = n_slab * page
    n_pages = past // page
    s_len = q_ref.shape[2]
    rows = 16

    def copies(u, slot, i, h, pg):
        dst = pl.ds(pl.multiple_of(i * page, page), page)
        return (pltpu.make_async_copy(ck_ref.at[pg, :, h, :], kbuf.at[slot, dst], sem.at[0, slot]),
                pltpu.make_async_copy(cv_ref.at[pg, :, h, :], vbuf.at[slot, dst], sem.at[1, slot]))

    def start(u, slot):
        h = u % N_HEADS
        row0 = (u // N_HEADS) * n_pages

        def sel(n, carry):
            logical = blk_ref[u * n_sel + n] * ppb
            for p in range(ppb):
                ck, cv = copies(u, slot, n * ppb + p, h, pt_ref[row0 + logical + p])
                ck.start()
                cv.start()
            return carry

        lax.fori_loop(0, n_sel, sel, 0)

    u = pl.program_id(0)

    @pl.when(u == 0)
    def _():
        start(0, 0)

    slot = u % 2
    bi = u // N_HEADS
    h = u % N_HEADS

    @pl.when(u + 1 < n_units)
    def _():
        start(u + 1, 1 - slot)

    for k in range(cpu):
        v = u * cpu + k
        cb = v % CONV_GROUPS
        tile = v // CONV_GROUPS

        @pl.when(cb == 0)
        def _():
            @pl.when(tile % nt == 0)
            def _():
                xs_ref[0, 0:HALO, :] = jnp.zeros((HALO, xs_ref.shape[2]), F32)

            xs_ref[0, HALO:HALO + tt, :] = glu_ref[0]
            _conv_preshift(xs_ref, tt)

        _conv_lane_block(xs_ref, w_ref, b_ref, gg_ref, gb_ref, c_ref, 0,
                         pl.ds(pl.multiple_of(cb * LANES, LANES), LANES), tt=tt, rc=rc, preshift=True)

        @pl.when(cb == CONV_GROUPS - 1)
        def _():
            xs_ref[0, 0:HALO, :] = xs_ref[0, tt:tt + HALO, :]

    def slab_done(i, carry):
        ck, cv = copies(u, slot, i, 0, 0)
        ck.wait()
        cv.wait()
        return carry

    lax.fori_loop(0, n_slab, slab_done, 0)

    slope = slopes_ref[h]
    q = q_ref[bi, h]
    qs = jnp.concatenate([q * (HEAD_DIM ** -0.5), jnp.zeros((rows - s_len, HEAD_DIM), F32)], axis=0)
    qs = qs.astype(BF16)
    kb = kbuf[slot].astype(BF16)
    vb = vbuf[slot].astype(BF16)
    s = _nt_dot(qs, kb)

    per_blk = page * ppb
    offs = lax.broadcasted_iota(jnp.int32, (1, per_blk), 1)
    kpos = jnp.concatenate(
        [(blk_ref[u * n_sel + i] * per_blk + offs).astype(F32) for i in range(n_sel)], axis=1)
    row = lax.broadcasted_iota(jnp.int32, (rows, 1), 0)
    qpos = (past + row).astype(F32)
    s = s - slope * (qpos - kpos)
    col = lax.broadcasted_iota(jnp.int32, (rows, n_keys), 1)
    per_q = MOBA_TOPK * per_blk
    mine = jnp.logical_and(col >= row * per_q, col < (row + 1) * per_q)
    s = jnp.where(mine, s, NEG_INF)

    kn = jnp.concatenate([kn_ref[bi, h], jnp.zeros((rows - s_len, HEAD_DIM), F32)], axis=0).astype(BF16)
    vn = jnp.concatenate([vn_ref[bi, h], jnp.zeros((rows - s_len, HEAD_DIM), F32)], axis=0).astype(BF16)
    ri = lax.broadcasted_iota(jnp.int32, (rows, rows), 0)
    ci = lax.broadcasted_iota(jnp.int32, (rows, rows), 1)
    so = _nt_dot(qs, kn) - slope * (ri - ci).astype(F32)
    so = jnp.where(jnp.logical_and(ri >= ci, ci < s_len), so, NEG_INF)

    m = jnp.maximum(jnp.max(s, axis=-1, keepdims=True), jnp.max(so, axis=-1, keepdims=True))
    p = jnp.exp(s - m)
    po = jnp.exp(so - m)
    l = jnp.sum(p, axis=-1, keepdims=True) + jnp.sum(po, axis=-1, keepdims=True)
    acc = (jnp.dot(p.astype(BF16), vb, preferred_element_type=F32)
           + jnp.dot(po.astype(BF16), vn, preferred_element_type=F32))
    o = (acc / l)[0:s_len]
    o = (o * lax.rsqrt(jnp.mean(o * o, axis=-1, keepdims=True) + EPS)) * nrm_ref[pl.ds(h, 1), :]
    o_ref[bi, h] = o


def _moba_sample(page_table_flat, blks, slopes, q_t, kn_t, vn_t, attn_norm_hd, cache_k, cache_v, past,
                 glu, w_dw, b_dw, gn_g, gn_b, tt):
    db, nh, s, hd = q_t.shape
    _, page, _, _ = cache_k.shape
    ppb = MOBA_BLOCK // page
    n_sel = s * MOBA_TOPK
    n_units = db * nh
    b, t, dc = glu.shape
    assert dc == CONV_GROUPS * LANES and t % tt == 0 and tt >= 4 * SUBLANES
    nt = t // tt
    n_conv_units = b * nt * CONV_GROUPS
    cpu = -(-n_conv_units // n_units)
    assert CONV_GROUPS % cpu == 0 and cpu * n_units == n_conv_units
    whole = lambda shp: pl.BlockSpec(shp, lambda i, *_: (0,) * len(shp))
    tile = pl.BlockSpec((1, tt, dc), lambda i, *_: ((i * cpu // CONV_GROUPS) // nt, (i * cpu // CONV_GROUPS) % nt, 0))
    return pl.pallas_call(
        functools.partial(_moba_sample_kernel, n_units=n_units, n_sel=n_sel, ppb=ppb, past=past,
                          tt=tt, rc=min(tt, 64), nt=nt, cpu=cpu),
        grid_spec=pltpu.PrefetchScalarGridSpec(
            num_scalar_prefetch=3,
            grid=(n_units,),
            in_specs=[
                whole(q_t.shape), whole(kn_t.shape), whole(vn_t.shape), whole(attn_norm_hd.shape),
                tile, whole(w_dw.shape), whole(b_dw.shape), whole(gn_g.shape), whole(gn_b.shape),
                pl.BlockSpec(memory_space=pl.ANY),
                pl.BlockSpec(memory_space=pl.ANY),
            ],
            out_specs=[whole(q_t.shape), tile],
            scratch_shapes=[
                pltpu.VMEM((2, n_sel * ppb * page, hd), F32),
                pltpu.VMEM((2, n_sel * ppb * page, hd), F32),
                pltpu.VMEM((SUBLANES, HALO + tt, dc), F32),
                pltpu.SemaphoreType.DMA((2, 2)),
            ],
        ),
        out_shape=[jax.ShapeDtypeStruct(q_t.shape, F32), jax.ShapeDtypeStruct((b, t, dc), BF16)],
        compiler_params=_cparams(("arbitrary",)),
        name="moba_sample_conv",
    )(page_table_flat, blks, slopes, q_t, kn_t, vn_t, attn_norm_hd, glu, w_dw, b_dw, gn_g, gn_b, cache_k, cache_v)


def _outproj_kernel(op_ref, cp_ref, xp_ref, os_ref, cs_ref, xs_ref, w_ref, g_ref, wr_ref, br_ref,
                    x2_ref, h2_ref, route_ref, *, nb_p):
    i = pl.program_id(0)
    tail = (w_ref, g_ref, wr_ref, br_ref, x2_ref, h2_ref, route_ref)

    @pl.when(i < nb_p)
    def _():
        _outproj_tile(op_ref, cp_ref, xp_ref, *tail)

    @pl.when(i >= nb_p)
    def _():
        _outproj_tile(os_ref, cs_ref, xs_ref, *tail)


def _outproj_tile(o_ref, c_ref, x_ref, w_ref, g_ref, wr_ref, br_ref, x2_ref, h2_ref, route_ref):
    oc = jnp.concatenate([o_ref[...].astype(BF16), c_ref[...].astype(BF16)], axis=1)
    x2 = x_ref[...] + jnp.dot(oc, w_ref[...], preferred_element_type=F32)
    x2_ref[...] = x2
    r = lax.rsqrt(jnp.mean(x2 * x2, axis=-1, keepdims=True) + EPS)
    h2 = (x2 * r) * g_ref[...]
    h2_ref[...] = h2

    h_hi = h2.astype(BF16)
    h_lo = (h2 - h_hi.astype(F32)).astype(BF16)
    r1 = jnp.dot(h_hi, wr_ref[...], preferred_element_type=F32)
    r2 = jnp.dot(h_lo, wr_ref[:, 0:LANES], preferred_element_type=F32)
    logits = r1[:, 0:LANES] + r1[:, LANES:2 * LANES] + r2 + br_ref[...]

    lane = lax.broadcasted_iota(jnp.int32, logits.shape, 1)
    is_g = lane < N_EXPERT_GROUPS
    gl = jnp.where(is_g, logits, -jnp.inf)
    gmax = jnp.max(gl, axis=-1, keepdims=True)
    gidx = jnp.min(jnp.where(gl == gmax, lane, LANES), axis=-1, keepdims=True)
    gsum = jnp.sum(jnp.where(is_g, jnp.exp(gl - gmax), 0.0), axis=-1, keepdims=True)
    g_gate = 1.0 / gsum

    lo = N_EXPERT_GROUPS + gidx * EXPERTS_PER_GROUP
    el = jnp.where(jnp.logical_and(lane >= lo, lane < lo + EXPERTS_PER_GROUP), logits, -jnp.inf)
    v0 = jnp.max(el, axis=-1, keepdims=True)
    i0 = jnp.min(jnp.where(el == v0, lane, LANES), axis=-1, keepdims=True)
    el = jnp.where(lane == i0, -jnp.inf, el)
    v1 = jnp.max(el, axis=-1, keepdims=True)
    i1 = jnp.min(jnp.where(el == v1, lane, LANES), axis=-1, keepdims=True)
    e = jnp.exp(v1 - v0)
    w0 = g_gate / (1.0 + e)
    w1 = g_gate * e / (1.0 + e)

    out = jnp.where(lane == 0, (i0 - N_EXPERT_GROUPS).astype(F32), 0.0)
    out = jnp.where(lane == 1, (i1 - N_EXPERT_GROUPS).astype(F32), out)
    out = jnp.where(lane == 2, w0, out)
    out = jnp.where(lane == 3, w1, out)
    route_ref[...] = out


def _out_projection(o_p, c_p, x_p, o_s, c_s, x_s, w_out, norm_g, wr, br, tm):
    np_, d = x_p.shape
    ns_ = x_s.shape[0]
    assert np_ % tm == 0 and ns_ % tm == 0
    nb_p, nb_s = np_ // tm, ns_ // tm
    n = np_ + ns_
    prow = lambda w: pl.BlockSpec((tm, w), lambda i: (jnp.minimum(i, nb_p - 1), 0))
    srow = lambda w: pl.BlockSpec((tm, w), lambda i: (jnp.maximum(i - nb_p, 0), 0))
    row = lambda w: pl.BlockSpec((tm, w), lambda i: (i, 0))
    full = lambda a: pl.BlockSpec(a.shape, lambda i: (0, 0))
    return pl.pallas_call(
        functools.partial(_outproj_kernel, nb_p=nb_p),
        grid=(nb_p + nb_s,),
        in_specs=[prow(o_p.shape[1]), prow(c_p.shape[1]), prow(d), srow(o_s.shape[1]), srow(c_s.shape[1]), srow(d),
                  full(w_out), full(norm_g), full(wr), full(br)],
        out_specs=[row(d), row(d), row(LANES)],
        out_shape=[jax.ShapeDtypeStruct((n, d), F32), jax.ShapeDtypeStruct((n, d), F32),
                   jax.ShapeDtypeStruct((n, LANES), F32)],
        compiler_params=_cparams(("arbitrary",)),
        name="out_projection_router",
    )(o_p, c_p, x_p, o_s, c_s, x_s, w_out, norm_g, wr, br)


DMA_UNROLL = SUBLANES


def _for_rows(n, fn):
    def group(gi, carry):
        for k in range(DMA_UNROLL):
            fn(gi * DMA_UNROLL + k, gi, k)
        return carry

    def single(i, carry):
        fn(i, lax.div(i, DMA_UNROLL), lax.rem(i, DMA_UNROLL))
        return carry

    n_grp = lax.div(n, DMA_UNROLL)
    lax.fori_loop(0, n_grp, group, 0)
    lax.fori_loop(n_grp * DMA_UNROLL, n, single, 0)


def _moe_kernel(order_ref, cexp_ref, cbase_ref, cnv_ref, h_ref, wg_ref, wu_ref, wd_ref, out_ref,
                xbuf, obuf, wgb, wub, wdb, gsem, ssem, *, n_tok, n_chunks):
    g = pl.program_id(0)
    slot = g % 2
    _, n_grp, grp, d = xbuf.shape

    def gather(chunk, buf_slot):
        base = cbase_ref[chunk]

        def desc(i, gi, k):
            tok = lax.shift_right_logical(order_ref[base + i], 1)
            return pltpu.make_async_copy(h_ref.at[pl.ds(tok, 1)], xbuf.at[buf_slot, gi, pl.ds(k, 1)],
                                         gsem.at[buf_slot])
        return desc

    def scatter(chunk, buf_slot):
        base = cbase_ref[chunk]

        def desc(i, gi, k):
            pair = order_ref[base + i]
            row = (pair & 1) * n_tok + lax.shift_right_logical(pair, 1)
            return pltpu.make_async_copy(obuf.at[buf_slot, gi, pl.ds(k, 1)], out_ref.at[pl.ds(row, 1)],
                                         ssem.at[buf_slot])
        return desc

    def gather_done(buf_slot):
        pltpu.make_async_copy(h_ref.at[pl.ds(0, 1)], xbuf.at[buf_slot, 0, pl.ds(0, 1)], gsem.at[buf_slot]).wait()

    def scatter_done(buf_slot):
        pltpu.make_async_copy(obuf.at[buf_slot, 0, pl.ds(0, 1)], out_ref.at[pl.ds(0, 1)], ssem.at[buf_slot]).wait()

    @pl.when(g == 0)
    def _():
        xbuf[...] = jnp.zeros(xbuf.shape, xbuf.dtype)
        _for_rows(cnv_ref[0], lambda i, gi, k: gather(0, 0)(i, gi, k).start())

    @pl.when(jnp.logical_or(g == 0, cexp_ref[g] != cexp_ref[jnp.maximum(g - 1, 0)]))
    def _():
        wgb[...] = wg_ref[0].astype(BF16)
        wub[...] = wu_ref[0].astype(BF16)
        wdb[...] = wd_ref[0].astype(BF16)

    @pl.when(g + 1 < n_chunks)
    def _():
        _for_rows(cnv_ref[g + 1], lambda i, gi, k: gather(g + 1, 1 - slot)(i, gi, k).start())

    nv = cnv_ref[g]
    _for_rows(nv, lambda i, gi, k: gather_done(slot))

    @pl.when(g >= 2)
    def _():
        _for_rows(cnv_ref[g - 2], lambda i, gi, k: scatter_done(slot))

    def expert_mlp(n_g):
        x = xbuf[slot, 0:n_g].reshape(n_g * grp, d).astype(BF16)
        gt = jnp.dot(x, wgb[...], preferred_element_type=F32)
        up = jnp.dot(x, wub[...], preferred_element_type=F32)
        hid = (gt * (1.0 / (1.0 + jnp.exp(-gt)))) * up
        y = jnp.dot(hid.astype(BF16), wdb[...], preferred_element_type=F32)
        obuf[slot, 0:n_g] = y.reshape(n_g, grp, d)

    sizes = sorted({n_grp, n_grp // 2, max(n_grp // 6, 1)})
    lo = 0
    for n_g in sizes:
        pl.when(jnp.logical_and(nv > lo * grp, nv <= n_g * grp))(functools.partial(expert_mlp, n_g))
        lo = n_g

    _for_rows(nv, lambda i, gi, k: scatter(g, slot)(i, gi, k).start())

    @pl.when(g == n_chunks - 1)
    def _():
        if n_chunks >= 2:
            _for_rows(cnv_ref[g - 1], lambda i, gi, k: scatter_done(1 - slot))
        _for_rows(nv, lambda i, gi, k: scatter_done(slot))


def _moe(order, cexp, cbase, cnv, h2, w_gate, w_up, w_down, ch):
    n, d = h2.shape
    ne, _, de = w_gate.shape
    n_chunks = cexp.shape[0]
    wspec = lambda shp: pl.BlockSpec((1,) + shp, lambda g, order, cexp, cbase, cnv: (cexp[g], 0, 0))
    return pl.pallas_call(
        functools.partial(_moe_kernel, n_tok=n, n_chunks=n_chunks),
        grid_spec=pltpu.PrefetchScalarGridSpec(
            num_scalar_prefetch=4,
            grid=(n_chunks,),
            in_specs=[
                pl.BlockSpec(memory_space=pl.ANY),
                wspec((d, de)), wspec((d, de)), wspec((de, d)),
            ],
            out_specs=pl.BlockSpec(memory_space=pl.ANY),
            scratch_shapes=[
                pltpu.VMEM((2, ch // DMA_UNROLL, DMA_UNROLL, d), F32),
                pltpu.VMEM((2, ch // DMA_UNROLL, DMA_UNROLL, d), F32),
                pltpu.VMEM((d, de), BF16),
                pltpu.VMEM((d, de), BF16),
                pltpu.VMEM((de, d), BF16),
                pltpu.SemaphoreType.DMA((2,)),
                pltpu.SemaphoreType.DMA((2,)),
            ],
        ),
        out_shape=jax.ShapeDtypeStruct((2 * n, d), F32),
        compiler_params=_cparams(("arbitrary",)),
        name="expert_mlp",
    )(order, cexp, cbase, cnv, h2, w_gate, w_up, w_down)


def _expert_chunks(eid, ch):
    n_pairs = eid.shape[0]
    n_chunks = N_EXPERTS + n_pairs // ch
    order = jnp.argsort(eid, stable=True).astype(jnp.int32)
    counts = jnp.sum(eid[:, None] == jnp.arange(N_EXPERTS, dtype=jnp.int32)[None, :], axis=0, dtype=jnp.int32)
    offs = jnp.cumsum(counts, dtype=jnp.int32) - counts
    nch = (counts + ch - 1) // ch
    cum = jnp.cumsum(nch, dtype=jnp.int32)
    gidx = jnp.arange(n_chunks, dtype=jnp.int32)
    cexp_raw = jnp.sum(gidx[:, None] >= cum[None, :], axis=1, dtype=jnp.int32)
    last = jnp.maximum(jnp.max(jnp.where(nch > 0, jnp.arange(N_EXPERTS, dtype=jnp.int32), 0)), 0)
    used = gidx < cum[-1]
    cexp = jnp.where(used, jnp.minimum(cexp_raw, N_EXPERTS - 1), last)
    local = gidx - (cum[cexp] - nch[cexp])
    cbase = jnp.where(used, offs[cexp] + local * ch, 0)
    cnv = jnp.where(used, jnp.clip(counts[cexp] - local * ch, 0, ch), 0)
    return order, cexp.astype(jnp.int32), cbase.astype(jnp.int32), cnv.astype(jnp.int32)


def _final_kernel(x2_ref, ya_ref, yb_ref, route_ref, g_ref, yp_ref, ys_ref, *, nb_p):
    i = pl.program_id(0)
    rt = route_ref[...]
    x3 = x2_ref[...] + rt[:, 2:3] * ya_ref[...] + rt[:, 3:4] * yb_ref[...]
    r = lax.rsqrt(jnp.mean(x3 * x3, axis=-1, keepdims=True) + EPS)
    y = (x3 * r) * g_ref[...]

    @pl.when(i < nb_p)
    def _():
        yp_ref[...] = y

    @pl.when(i >= nb_p)
    def _():
        ys_ref[...] = y


def _final(x2, y2, route, norm_g, np_, tm):
    n, d = x2.shape
    assert np_ % tm == 0 and n % tm == 0
    nb, nb_p = n // tm, np_ // tm
    return pl.pallas_call(
        functools.partial(_final_kernel, nb_p=nb_p),
        grid=(nb,),
        in_specs=[
            pl.BlockSpec((tm, d), lambda i: (i, 0)),
            pl.BlockSpec((tm, d), lambda i: (i, 0)),
            pl.BlockSpec((tm, d), lambda i: (i + nb, 0)),
            pl.BlockSpec((tm, LANES), lambda i: (i, 0)),
            pl.BlockSpec((1, d), lambda i: (0, 0)),
        ],
        out_specs=[
            pl.BlockSpec((tm, d), lambda i: (jnp.minimum(i, nb_p - 1), 0)),
            pl.BlockSpec((tm, d), lambda i: (jnp.maximum(i - nb_p, 0), 0)),
        ],
        out_shape=[jax.ShapeDtypeStruct((np_, d), F32), jax.ShapeDtypeStruct((n - np_, d), F32)],
        compiler_params=_cparams(("arbitrary",)),
        name="combine_final_norm",
    )(x2, y2, y2, route, norm_g)


def _pick_tile(n, pref):
    t = min(n, pref)
    while n % t:
        t //= 2
    return t


def kernel(x_prompt, x_sample, cache_k, cache_v, state_conv, page_table, norm_mix, w_in, w_dw, b_dw,
           conv_norm_g, conv_norm_b, attn_out_norm, w_out, norm_ffn, w_router_group, b_router_group,
           w_router_expert, b_router_expert, w_gate, w_up, w_down, norm_final):
    depth = norm_mix.shape[0]
    assert depth == 1
    b, t, d = x_prompt.shape
    db, s, _ = x_sample.shape
    n_pages = page_table.shape[1]
    page = cache_k.shape[2]
    past = n_pages * page
    nbf = past // MOBA_BLOCK
    assert past % MOBA_BLOCK == 0 and nbf >= MOBA_TOPK and nbf <= LANES
    d_conv = d - D_ATTN
    assert d_conv == D_ATTN
    np_, ns_ = b * t, db * s

    slopes = jnp.exp2(-8.0 * jnp.arange(1, N_HEADS + 1, dtype=F32) / N_HEADS)
    w_in_bf = w_in[0].astype(BF16)
    w_out_bf = w_out[0].astype(BF16)
    wr = jnp.zeros((d, LANES), F32)
    wr = wr.at[:, :N_EXPERT_GROUPS].set(w_router_group[0])
    wr = wr.at[:, N_EXPERT_GROUPS:N_EXPERT_GROUPS + N_EXPERTS].set(w_router_expert[0])
    wr_hi = wr.astype(BF16)
    wr_lo = (wr - wr_hi.astype(F32)).astype(BF16)
    wr_cat = jnp.concatenate([wr_hi, wr_lo], axis=1)
    br = jnp.zeros((1, LANES), F32)
    br = br.at[0, :N_EXPERT_GROUPS].set(b_router_group[0])
    br = br.at[0, N_EXPERT_GROUPS:N_EXPERT_GROUPS + N_EXPERTS].set(b_router_expert[0])
    g_mix = norm_mix[0].reshape(1, d)
    g_ffn = norm_ffn[0].reshape(1, d)
    g_fin = norm_final.reshape(1, d)
    attn_norm = attn_out_norm[0].reshape(1, D_ATTN)
    conv_vec = lambda a: a[0].reshape(1, d_conv)

    xp = x_prompt.reshape(np_, d)
    xs = x_sample.reshape(ns_, d)
    qp, kp, vp, glu_p = _in_projection(xp, g_mix, w_in_bf, _pick_tile(np_, 512))
    qs, ks, vs, glu_s = _in_projection(xs, g_mix, w_in_bf, _pick_tile(ns_, 256))

    glu_p3 = glu_p.reshape(b, t, d_conv)
    glu_s3 = glu_s.reshape(db, s, d_conv)
    conv_args = (w_dw[0], conv_vec(b_dw), conv_vec(conv_norm_g), conv_vec(conv_norm_b))
    c_s = _conformer_conv(glu_s3, state_conv[0], *conv_args, tt=s, nseq=db, out_dtype=F32)
    conv_prompt = glu_p3[:, t - (CONV_WIDTH - 1):][None]
    conv_sample = jnp.concatenate([state_conv[0], glu_s3], axis=1)[:, -(CONV_WIDTH - 1):][None]

    ck = cache_k[0]
    cv = cache_v[0]
    o_p, kmean = _moba_prompt(qp.reshape(b, t, D_ATTN), kp.reshape(b, t, D_ATTN), vp.reshape(b, t, D_ATTN),
                              slopes, attn_norm, page_table, ck)

    to_heads = lambda a: a.reshape(db, s, N_HEADS, HEAD_DIM).transpose(0, 2, 1, 3)
    q_t, kn_t, vn_t = to_heads(qs), to_heads(ks), to_heads(vs)
    idx = _sample_topk(q_t, kmean)[..., :MOBA_TOPK]
    o_t, c_p = _moba_sample(page_table.reshape(-1), idx.reshape(-1), slopes, q_t, kn_t, vn_t,
                            attn_norm.reshape(N_HEADS, HEAD_DIM), ck, cv, past,
                            glu_p3, *conv_args, tt=_pick_tile(t, 256))
    o_s = o_t.transpose(0, 2, 1, 3).reshape(ns_, D_ATTN)

    tm = _pick_tile(ns_, 256)
    x2, h2, route = _out_projection(o_p.reshape(np_, D_ATTN), c_p.reshape(np_, d_conv), xp,
                                    o_s, c_s.reshape(ns_, d_conv), xs,
                                    w_out_bf, g_ffn, wr_cat, br, tm)

    ch = 384
    eid = route[:, 0:2].astype(jnp.int32).reshape(-1)
    order, cexp, cbase, cnv = _expert_chunks(eid, ch)
    y2 = _moe(order, cexp, cbase, cnv, h2, w_gate[0], w_up[0], w_down[0], ch)

    y_p, y_s = _final(x2, y2, route, g_fin, np_, tm)

    page_shape = (depth, b, t // page, page, N_HEADS, HEAD_DIM)
    return (y_p.reshape(b, t, d), y_s.reshape(db, s, d),
            kp.reshape(page_shape), vp.reshape(page_shape), conv_prompt,
            ks.reshape(depth, db, s, N_HEADS, HEAD_DIM), vs.reshape(depth, db, s, N_HEADS, HEAD_DIM),
            conv_sample)
```

```python
import functools

import jax
import jax.numpy as jnp
from jax import lax
from jax.experimental import pallas as pl
from jax.experimental.pallas import tpu as pltpu

F32 = jnp.float32
BF16 = jnp.bfloat16

N_HEADS = 8
HEAD_DIM = 128
D_ATTN = N_HEADS * HEAD_DIM
CONV_GROUPS = 8
CONV_WIDTH = 31
MOBA_BLOCK = 256
MOBA_TOPK = 3
N_EXPERT_GROUPS = 4
EXPERTS_PER_GROUP = 8
N_EXPERTS = N_EXPERT_GROUPS * EXPERTS_PER_GROUP
EPS = 1e-6
NEG_INF = -1e30

LANES = 128
SUBLANES = 8
HALO = 32
VMEM_LIMIT = 56 * 1024 * 1024
STREAM_PAGES = 16


def _cparams(sem=None):
    return pltpu.CompilerParams(dimension_semantics=sem, vmem_limit_bytes=VMEM_LIMIT)


def _nt_dot(a, b, precision=None):
    return lax.dot_general(a, b, (((1,), (1,)), ((), ())), precision=precision,
                           preferred_element_type=F32)


def _inproj_kernel(x_ref, g_ref, w_ref, wgate_ref, q_ref, k_ref, v_ref, glu_ref, h_ref):
    j = pl.program_id(1)

    @pl.when(j == 0)
    def _():
        x = x_ref[...]
        r = lax.rsqrt(jnp.mean(x * x, axis=-1, keepdims=True) + EPS)
        h_ref[...] = ((x * r) * g_ref[...]).astype(BF16)

    z = jnp.dot(h_ref[...], w_ref[...], preferred_element_type=F32)

    @pl.when(j == 0)
    def _():
        q_ref[...] = z

    @pl.when(j == 1)
    def _():
        k_ref[...] = z

    @pl.when(j == 2)
    def _():
        v_ref[...] = z

    @pl.when(j == 3)
    def _():
        gate = jnp.dot(h_ref[...], wgate_ref[...], preferred_element_type=F32)
        glu_ref[...] = z * (1.0 / (1.0 + jnp.exp(-gate)))


def _in_projection(x, norm_g, w_in_bf, tm):
    n, d = x.shape
    dc = D_ATTN
    out = jax.ShapeDtypeStruct((n, dc), F32)
    row = pl.BlockSpec((tm, dc), lambda i, j: (i, 0))
    return pl.pallas_call(
        _inproj_kernel,
        grid=(n // tm, 4),
        in_specs=[
            pl.BlockSpec((tm, d), lambda i, j: (i, 0)),
            pl.BlockSpec((1, d), lambda i, j: (0, 0)),
            pl.BlockSpec((d, dc), lambda i, j: (0, j)),
            pl.BlockSpec((d, dc), lambda i, j: (0, 4)),
        ],
        out_specs=[row, row, row, row],
        out_shape=[out, out, out, out],
        scratch_shapes=[pltpu.VMEM((tm, d), BF16)],
        compiler_params=_cparams(("arbitrary", "arbitrary")),
        name="in_projection",
    )(x, norm_g, w_in_bf, w_in_bf)


def _conv_lane_block(xs_ref, w_ref, b_ref, gg_ref, gb_ref, c_ref, sq, cs, *, tt, rc, preshift):
    off = HALO - (CONV_WIDTH - 1)
    for r0 in range(0, tt, rc):
        acc = jnp.broadcast_to(b_ref[:, cs], (rc, LANES))
        for j in range(CONV_WIDTH):
            o = off + j
            if preshift:
                lo = r0 + (o // SUBLANES) * SUBLANES
                x = xs_ref[o % SUBLANES, lo:lo + rc, cs]
            else:
                x = xs_ref[0, r0 + o:r0 + o + rc, cs]
            acc = acc + x * w_ref[j:j + 1, cs]
        mu = jnp.mean(acc, axis=-1, keepdims=True)
        dlt = acc - mu
        var = jnp.mean(dlt * dlt, axis=-1, keepdims=True)
        yn = dlt * lax.rsqrt(var + EPS)
        yn = yn * gg_ref[:, cs] + gb_ref[:, cs]
        c_ref[sq, r0:r0 + rc, cs] = (yn * (1.0 / (1.0 + jnp.exp(-yn)))).astype(c_ref.dtype)


def _conv_preshift(xs_ref, tt):
    span = HALO + tt - SUBLANES
    for r in range(1, SUBLANES):
        xs_ref[r, 0:span, :] = xs_ref[0, r:r + span, :]


def _conv_kernel(glu_ref, st_ref, w_ref, b_ref, gg_ref, gb_ref, c_ref, xs_ref, *, tt, rc, nseq, carry, preshift):
    t = pl.program_id(1)

    def lane_block(sq, cs):
        _conv_lane_block(xs_ref, w_ref, b_ref, gg_ref, gb_ref, c_ref, sq, cs, tt=tt, rc=rc, preshift=preshift)

    hist = HALO - (CONV_WIDTH - 1)

    def load_state(sq):
        xs_ref[0, 0:hist, :] = jnp.zeros((hist, xs_ref.shape[2]), F32)
        xs_ref[0, hist:HALO, :] = st_ref[sq]

    def one_seq(sq):
        if carry:
            pl.when(t == 0)(functools.partial(load_state, sq))
        else:
            load_state(sq)
        xs_ref[0, HALO:HALO + tt, :] = glu_ref[sq]
        if preshift:
            _conv_preshift(xs_ref, tt)

            def group(cb, carry_):
                lane_block(sq, pl.ds(pl.multiple_of(cb * LANES, LANES), LANES))
                return carry_

            lax.fori_loop(0, CONV_GROUPS, group, 0)
        else:
            for cb in range(CONV_GROUPS):
                lane_block(sq, pl.ds(cb * LANES, LANES))
        if carry:
            xs_ref[0, 0:HALO, :] = xs_ref[0, tt:tt + HALO, :]

    if nseq == 1:
        one_seq(0)
    else:
        def seq(sq, carry_):
            one_seq(sq)
            return carry_

        lax.fori_loop(0, nseq, seq, 0)


def _conformer_conv(glu, state, w_dw, b_dw, gn_g, gn_b, tt, nseq, out_dtype):
    b, t, dc = glu.shape
    assert dc == CONV_GROUPS * LANES and b % nseq == 0 and t % tt == 0 and state.shape[1] == CONV_WIDTH - 1
    nt = t // tt
    assert nt == 1 or nseq == 1
    preshift = tt >= 4 * SUBLANES
    rc = min(tt, 64)
    vec = pl.BlockSpec((1, dc), lambda i, j: (0, 0))
    return pl.pallas_call(
        functools.partial(_conv_kernel, tt=tt, rc=rc, nseq=nseq, carry=nt > 1, preshift=preshift),
        grid=(b // nseq, nt),
        in_specs=[
            pl.BlockSpec((nseq, tt, dc), lambda i, j: (i, j, 0)),
            pl.BlockSpec((nseq, CONV_WIDTH - 1, dc), lambda i, j: (i, 0, 0)),
            pl.BlockSpec((CONV_WIDTH, dc), lambda i, j: (0, 0)),
            vec, vec, vec,
        ],
        out_specs=pl.BlockSpec((nseq, tt, dc), lambda i, j: (i, j, 0)),
        out_shape=jax.ShapeDtypeStruct((b, t, dc), out_dtype),
        scratch_shapes=[pltpu.VMEM((SUBLANES if preshift else 1, HALO + tt, dc), F32)],
        compiler_params=_cparams(("arbitrary", "arbitrary")),
        name="conformer_conv",
    )(glu, state, w_dw, b_dw, gn_g, gn_b)


def _moba_prompt_kernel(pt_ref, slopes_ref, q_ref, k_ref, v_ref, nrm_ref, ck_ref, o_ref, kmean_ref,
                        qb_ref, kb_ref, vt_ref, g_ref, biasp_ref, biasd_ref, pbuf, kmbuf, psem, osem,
                        *, nblk, qpt, kpi, n_steps, n_chunks, cps, pps, n_pages):
    h = pl.program_id(1)
    qi = pl.program_id(2)
    blk = MOBA_BLOCK
    qw = qpt * blk
    t = q_ref.shape[1]
    nb8 = -(-nblk // SUBLANES) * SUBLANES
    slope = slopes_ref[h]

    step = (pl.program_id(0) * N_HEADS + h) * (nblk // qpt) + qi
    c0 = step * cps
    exact = n_chunks == cps * n_steps
    ppb = blk // pbuf.shape[2]

    def start_chunk(chunk):
        slot = chunk % 2
        for i in range(pps):
            pltpu.make_async_copy(ck_ref.at[pt_ref[chunk * pps + i]], pbuf.at[slot, i], psem.at[slot]).start()

    def finish_chunk(chunk):
        slot = chunk % 2
        for i in range(pps):
            pltpu.make_async_copy(ck_ref.at[0], pbuf.at[slot, i], psem.at[slot]).wait()
        bi = (chunk * pps) // n_pages
        blk0 = ((chunk * pps) % n_pages) // ppb
        for i in range(pps // ppb):
            s = jnp.sum(pbuf[slot, ppb * i], axis=0)
            for p in range(1, ppb):
                s = s + jnp.sum(pbuf[slot, ppb * i + p], axis=0)
            kmbuf[bi, blk0 + i] = s * (1.0 / blk)

    def if_real(chunk, fn, always):
        if always:
            fn(chunk)
        else:
            pl.when(chunk < n_chunks)(functools.partial(fn, chunk))

    @pl.when(step == 0)
    def _():
        start_chunk(0)

    if_real(c0 + 1, start_chunk, exact and cps == 2)

    @pl.when(qi == 0)
    def _():
        q = q_ref[0]
        kf = k_ref[0]
        qb_ref[...] = (q * (HEAD_DIM ** -0.5)).astype(BF16)
        kb_ref[...] = kf.astype(BF16)
        vt_ref[...] = v_ref[0].T.astype(BF16)
        km = jnp.sum(kf.reshape(nblk, blk, HEAD_DIM), axis=1) * (1.0 / blk)
        if nb8 > nblk:
            km = jnp.concatenate([km, jnp.zeros((nb8 - nblk, HEAD_DIM), F32)], axis=0)
        gate = _nt_dot(km, q, precision=lax.Precision.HIGHEST)
        j_io = lax.broadcasted_iota(jnp.int32, gate.shape, 0)
        t_io = lax.broadcasted_iota(jnp.int32, gate.shape, 1)
        valid = j_io * blk + (blk - 1) < t_io - (t_io % blk)
        g = jnp.where(valid, gate, NEG_INF)
        sel = jnp.zeros(gate.shape, jnp.bool_)
        for _ in range(MOBA_TOPK):
            m = jnp.max(g, axis=0, keepdims=True)
            idx = jnp.min(jnp.where(g == m, j_io, nb8), axis=0, keepdims=True)
            pick = j_io == idx
            sel = jnp.logical_or(sel, pick)
            g = jnp.where(pick, -jnp.inf, g)
        gb = (slope * blk) * j_io.astype(F32) + jnp.where(jnp.logical_and(sel, valid), 0.0, NEG_INF)
        for j in range(nblk):
            g_ref[j] = jnp.broadcast_to(gb[j:j + 1, :], (SUBLANES, t))
        c = lax.broadcasted_iota(jnp.int32, (kpi * blk, qw), 0)
        biasp_ref[...] = slope * (c % blk).astype(F32)
        c = lax.broadcasted_iota(jnp.int32, (qw, qw), 0)
        r = lax.broadcasted_iota(jnp.int32, (qw, qw), 1)
        hidden = jnp.logical_or(c // blk > r // blk, jnp.logical_and(c // blk == r // blk, c % blk > r % blk))
        biasd_ref[...] = slope * (c % blk).astype(F32) + jnp.where(hidden, NEG_INF, 0.0)

    own = pl.ds(pl.multiple_of(qi * qw, qw), qw)
    qs = qb_ref[own, :]
    lane_blk = lax.broadcasted_iota(jnp.int32, (1, qw), 1) // blk

    def softmax_terms(st, gs, m_prev):
        segs = [st[i * blk:(i + 1) * blk] for i in range(len(gs))]
        m_new = m_prev
        for seg, g in zip(segs, gs):
            cand = jnp.max(seg, axis=0, keepdims=True) + g
            m_new = cand if m_new is None else jnp.maximum(m_new, cand)
        p = jnp.concatenate([jnp.exp(seg - (m_new - g)) for seg, g in zip(segs, gs)], axis=0)
        return m_new, p

    st = _nt_dot(kb_ref[own, :], qs) + biasd_ref[...]
    gs = []
    for i in range(qpt):
        g_blk = (slope * blk) * (qi * qpt + i).astype(F32)
        gs.append(jnp.where(lane_blk <= i, g_blk, g_ref[qi * qpt + i, 0:1, own]))
    m0, p = softmax_terms(st, gs, None)
    l0 = jnp.sum(p, axis=0, keepdims=True)
    acc0 = jnp.dot(vt_ref[:, own], p.astype(BF16), preferred_element_type=F32)

    n_past = qi * qpt

    def past_blocks(jp, carry):
        m, l, acc = carry
        rows = pl.ds(pl.multiple_of(jp * (kpi * blk), kpi * blk), kpi * blk)
        st = _nt_dot(kb_ref[rows, :], qs) + biasp_ref[...]
        gs = [jnp.where(jp * kpi + i < n_past, g_ref[jp * kpi + i, 0:1, own], NEG_INF) for i in range(kpi)]
        m_new, p = softmax_terms(st, gs, m)
        alpha = jnp.exp(m - m_new)
        l = alpha * l + jnp.sum(p, axis=0, keepdims=True)
        acc = alpha * acc + jnp.dot(vt_ref[:, rows], p.astype(BF16), preferred_element_type=F32)
        return m_new, l, acc

    _, l, acc = lax.fori_loop(0, (n_past + kpi - 1) // kpi, past_blocks, (m0, l0, acc0))
    ot = acc * (1.0 / l)
    ot = ot * lax.rsqrt(jnp.mean(ot * ot, axis=0, keepdims=True) + EPS)
    o_ref[0] = (ot.T * nrm_ref[...]).astype(o_ref.dtype)

    if_real(c0, finish_chunk, exact)
    if cps == 2:
        if_real(c0 + 2, start_chunk, False)
        if_real(c0 + 1, finish_chunk, exact)

    @pl.when(step == n_steps - 1)
    def _():
        out = pltpu.make_async_copy(kmbuf, kmean_ref, osem.at[0])
        out.start()
        out.wait()


def _moba_prompt(q, k, v, slopes, attn_norm, page_table, cache_k):
    b, t, _ = q.shape
    blk = MOBA_BLOCK
    nblk = t // blk
    assert t % blk == 0
    qpt = 2 if nblk % 2 == 0 else 1
    kpi = max(kk for kk in (4, 2, 1) if nblk % kk == 0)
    db, n_pages = page_table.shape
    _, page, nh, hd = cache_k.shape
    ppb = blk // page
    n_steps = b * N_HEADS * (nblk // qpt)
    pps = max(p for p in range(ppb, STREAM_PAGES + 1, ppb) if n_pages % p == 0)
    n_chunks = db * n_pages // pps
    cps = -(-n_chunks // n_steps)
    assert cps <= 2
    qw = qpt * blk
    seq = pl.BlockSpec((1, t, HEAD_DIM), lambda bi, h, qi, pt: (bi, 0, h))
    tile = pl.BlockSpec((1, qw, HEAD_DIM), lambda bi, h, qi, pt: (bi, qi, h))
    return pl.pallas_call(
        functools.partial(_moba_prompt_kernel, nblk=nblk, qpt=qpt, kpi=kpi, n_steps=n_steps, n_chunks=n_chunks,
                          cps=cps, pps=pps, n_pages=n_pages),
        grid_spec=pltpu.PrefetchScalarGridSpec(
            num_scalar_prefetch=1,
            grid=(b, N_HEADS, nblk // qpt),
            in_specs=[
                pl.BlockSpec(memory_space=pltpu.SMEM),
                seq, seq, seq,
                pl.BlockSpec((1, HEAD_DIM), lambda bi, h, qi, pt: (0, h)),
                pl.BlockSpec(memory_space=pl.ANY),
            ],
            out_specs=[tile, pl.BlockSpec(memory_space=pl.ANY)],
            scratch_shapes=[
                pltpu.VMEM((t, HEAD_DIM), BF16),
                pltpu.VMEM((t, HEAD_DIM), BF16),
                pltpu.VMEM((HEAD_DIM, t), BF16),
                pltpu.VMEM((nblk, SUBLANES, t), F32),
                pltpu.VMEM((kpi * blk, qw), F32),
                pltpu.VMEM((qw, qw), F32),
                pltpu.VMEM((2, pps, page, nh, hd), F32),
                pltpu.VMEM((db, n_pages // ppb, nh, hd), F32),
                pltpu.SemaphoreType.DMA((2,)),
                pltpu.SemaphoreType.DMA((1,)),
            ],
        ),
        out_shape=[jax.ShapeDtypeStruct((b, t, D_ATTN), BF16),
                   jax.ShapeDtypeStruct((db, n_pages // ppb, nh, hd), F32)],
        compiler_params=_cparams(("arbitrary", "arbitrary", "arbitrary")),
        name="moba_prompt",
    )(page_table.reshape(-1), slopes, q, k, v, attn_norm, cache_k)


def _sample_topk_kernel(q_ref, km_ref, idx_ref):
    nbf = km_ref.shape[1]
    s_len = q_ref.shape[2]
    gates = []
    for h in range(N_HEADS):
        km = jnp.concatenate([km_ref[0, :, h, :], jnp.zeros((LANES - nbf, HEAD_DIM), F32)], axis=0)
        gates.append(_nt_dot(q_ref[0, h], km, precision=lax.Precision.HIGHEST))
    gate = jnp.concatenate(gates, axis=0)
    lane = lax.broadcasted_iota(jnp.int32, gate.shape, 1)
    g = jnp.where(lane < nbf, gate, -jnp.inf)
    out = jnp.zeros(gate.shape, jnp.int32)
    for r in range(MOBA_TOPK):
        m = jnp.max(g, axis=-1, keepdims=True)
        idx = jnp.min(jnp.where(g == m, lane, LANES), axis=-1, keepdims=True)
        out = jnp.where(lane == r, idx, out)
        g = jnp.where(lane == idx, -jnp.inf, g)
    idx_ref[0] = out.reshape(N_HEADS, s_len, LANES)


def _sample_topk(q_t, kmean):
    db, nh, s, hd = q_t.shape
    nbf = kmean.shape[1]
    return pl.pallas_call(
        _sample_topk_kernel,
        grid=(db,),
        in_specs=[
            pl.BlockSpec((1, nh, s, hd), lambda i: (i, 0, 0, 0)),
            pl.BlockSpec((1, nbf, nh, hd), lambda i: (i, 0, 0, 0)),
        ],
        out_specs=pl.BlockSpec((1, nh, s, LANES), lambda i: (i, 0, 0, 0)),
        out_shape=jax.ShapeDtypeStruct((db, nh, s, LANES), jnp.int32),
        compiler_params=_cparams(("arbitrary",)),
        name="sample_topk",
    )(q_t, kmean)


def _moba_sample_kernel(pt_ref, blk_ref, slopes_ref, q_ref, kn_ref, vn_ref, nrm_ref,
                        glu_ref, w_ref, b_ref, gg_ref, gb_ref, ck_ref, cv_ref, o_ref, c_ref,
                        kbuf, vbuf, xs_ref, sem, *, n_units, n_sel, ppb, past, tt, rc, nt, cpu):
    page = ck_ref.shape[1]
    n_slab = n_sel * ppb
    n_keys = n_slab * page
    n_pages = past // page
    s_len = q_ref.shape[2]
    rows = 16

    def copies(u, slot, i, h, pg):
        dst = pl.ds(pl.multiple_of(i * page, page), page)
        return (pltpu.make_async_copy(ck_ref.at[pg, :, h, :], kbuf.at[slot, dst], sem.at[0, slot]),
                pltpu.make_async_copy(cv_ref.at[pg, :, h, :], vbuf.at[slot, dst], sem.at[1, slot]))

    def start(u, slot):
        h = u % N_HEADS
        row0 = (u // N_HEADS) * n_pages

        def sel(n, carry):
            logical = blk_ref[u * n_sel + n] * ppb
            for p in range(ppb):
                ck, cv = copies(u, slot, n * ppb + p, h, pt_ref[row0 + logical + p])
                ck.start()
                cv.start()
            return carry

        lax.fori_loop(0, n_sel, sel, 0)

    u = pl.program_id(0)

    @pl.when(u == 0)
    def _():
        start(0, 0)

    slot = u % 2
    bi = u // N_HEADS
    h = u % N_HEADS

    @pl.when(u + 1 < n_units)
    def _():
        start(u + 1, 1 - slot)

    for k in range(cpu):
        v = u * cpu + k
        cb = v % CONV_GROUPS
        tile = v // CONV_GROUPS

        @pl.when(cb == 0)
        def _():
            @pl.when(tile % nt == 0)
            def _():
                xs_ref[0, 0:HALO, :] = jnp.zeros((HALO, xs_ref.shape[2]), F32)

            xs_ref[0, HALO:HALO + tt, :] = glu_ref[0]
            _conv_preshift(xs_ref, tt)

        _conv_lane_block(xs_ref, w_ref, b_ref, gg_ref, gb_ref, c_ref, 0,
                         pl.ds(pl.multiple_of(cb * LANES, LANES), LANES), tt=tt, rc=rc, preshift=True)

        @pl.when(cb == CONV_GROUPS - 1)
        def _():
            xs_ref[0, 0:HALO, :] = xs_ref[0, tt:tt + HALO, :]

    def slab_done(i, carry):
        ck, cv = copies(u, slot, i, 0, 0)
        ck.wait()
        cv.wait()
        return carry

    lax.fori_loop(0, n_slab, slab_done, 0)

    slope = slopes_ref[h]
    q = q_ref[bi, h]
    qs = jnp.concatenate([q * (HEAD_DIM ** -0.5), jnp.zeros((rows - s_len, HEAD_DIM), F32)], axis=0)
    qs = qs.astype(BF16)
    kb = kbuf[slot].astype(BF16)
    vb = vbuf[slot].astype(BF16)
    s = _nt_dot(qs, kb)

    per_blk = page * ppb
    offs = lax.broadcasted_iota(jnp.int32, (1, per_blk), 1)
    kpos = jnp.concatenate(
        [(blk_ref[u * n_sel + i] * per_blk + offs).astype(F32) for i in range(n_sel)], axis=1)
    row = lax.broadcasted_iota(jnp.int32, (rows, 1), 0)
    qpos = (past + row).astype(F32)
    s = s - slope * (qpos - kpos)
    col = lax.broadcasted_iota(jnp.int32, (rows, n_keys), 1)
    per_q = MOBA_TOPK * per_blk
    mine = jnp.logical_and(col >= row * per_q, col < (row + 1) * per_q)
    s = jnp.where(mine, s, NEG_INF)

    kn = jnp.concatenate([kn_ref[bi, h], jnp.zeros((rows - s_len, HEAD_DIM), F32)], axis=0).astype(BF16)
    vn = jnp.concatenate([vn_ref[bi, h], jnp.zeros((rows - s_len, HEAD_DIM), F32)], axis=0).astype(BF16)
    ri = lax.broadcasted_iota(jnp.int32, (rows, rows), 0)
    ci = lax.broadcasted_iota(jnp.int32, (rows, rows), 1)
    so = _nt_dot(qs, kn) - slope * (ri - ci).astype(F32)
    so = jnp.where(jnp.logical_and(ri >= ci, ci < s_len), so, NEG_INF)

    m = jnp.maximum(jnp.max(s, axis=-1, keepdims=True), jnp.max(so, axis=-1, keepdims=True))
    p = jnp.exp(s - m)
    po = jnp.exp(so - m)
    l = jnp.sum(p, axis=-1, keepdims=True) + jnp.sum(po, axis=-1, keepdims=True)
    acc = (jnp.dot(p.astype(BF16), vb, preferred_element_type=F32)
           + jnp.dot(po.astype(BF16), vn, preferred_element_type=F32))
    o = (acc / l)[0:s_len]
    o = (o * lax.rsqrt(jnp.mean(o * o, axis=-1, keepdims=True) + EPS)) * nrm_ref[pl.ds(h, 1), :]
    o_ref[bi, h] = o


def _moba_sample(page_table_flat, blks, slopes, q_t, kn_t, vn_t, attn_norm_hd, cache_k, cache_v, past,
                 glu, w_dw, b_dw, gn_g, gn_b, tt):
    db, nh, s, hd = q_t.shape
    _, page, _, _ = cache_k.shape
    ppb = MOBA_BLOCK // page
    n_sel = s * MOBA_TOPK
    n_units = db * nh
    b, t, dc = glu.shape
    assert dc == CONV_GROUPS * LANES and t % tt == 0 and tt >= 4 * SUBLANES
    nt = t // tt
    n_conv_units = b * nt * CONV_GROUPS
    cpu = -(-n_conv_units // n_units)
    assert CONV_GROUPS % cpu == 0 and cpu * n_units == n_conv_units
    whole = lambda shp: pl.BlockSpec(shp, lambda i, *_: (0,) * len(shp))
    tile = pl.BlockSpec((1, tt, dc), lambda i, *_: ((i * cpu // CONV_GROUPS) // nt, (i * cpu // CONV_GROUPS) % nt, 0))
    return pl.pallas_call(
        functools.partial(_moba_sample_kernel, n_units=n_units, n_sel=n_sel, ppb=ppb, past=past,
                          tt=tt, rc=min(tt, 64), nt=nt, cpu=cpu),
        grid_spec=pltpu.PrefetchScalarGridSpec(
            num_scalar_prefetch=3,
            grid=(n_units,),
            in_specs=[
                whole(q_t.shape), whole(kn_t.shape), whole(vn_t.shape), whole(attn_norm_hd.shape),
                tile, whole(w_dw.shape), whole(b_dw.shape), whole(gn_g.shape), whole(gn_b.shape),
                pl.BlockSpec(memory_space=pl.ANY),
                pl.BlockSpec(memory_space=pl.ANY),
            ],
            out_specs=[whole(q_t.shape), tile],
            scratch_shapes=[
                pltpu.VMEM((2, n_sel * ppb * page, hd), F32),
                pltpu.VMEM((2, n_sel * ppb * page, hd), F32),
                pltpu.VMEM((SUBLANES, HALO + tt, dc), F32),
                pltpu.SemaphoreType.DMA((2, 2)),
            ],
        ),
        out_shape=[jax.ShapeDtypeStruct(q_t.shape, F32), jax.ShapeDtypeStruct((b, t, dc), BF16)],
        compiler_params=_cparams(("arbitrary",)),
        name="moba_sample_conv",
    )(page_table_flat, blks, slopes, q_t, kn_t, vn_t, attn_norm_hd, glu, w_dw, b_dw, gn_g, gn_b, cache_k, cache_v)


def _outproj_kernel(op_ref, cp_ref, xp_ref, os_ref, cs_ref, xs_ref, w_ref, g_ref, wr_ref, br_ref,
                    x2_ref, h2_ref, route_ref, *, nb_p):
    i = pl.program_id(0)
    tail = (w_ref, g_ref, wr_ref, br_ref, x2_ref, h2_ref, route_ref)

    @pl.when(i < nb_p)
    def _():
        _outproj_tile(op_ref, cp_ref, xp_ref, *tail)

    @pl.when(i >= nb_p)
    def _():
        _outproj_tile(os_ref, cs_ref, xs_ref, *tail)


def _outproj_tile(o_ref, c_ref, x_ref, w_ref, g_ref, wr_ref, br_ref, x2_ref, h2_ref, route_ref):
    oc = jnp.concatenate([o_ref[...].astype(BF16), c_ref[...].astype(BF16)], axis=1)
    x2 = x_ref[...] + jnp.dot(oc, w_ref[...], preferred_element_type=F32)
    x2_ref[...] = x2
    r = lax.rsqrt(jnp.mean(x2 * x2, axis=-1, keepdims=True) + EPS)
    h2 = (x2 * r) * g_ref[...]
    h2_ref[...] = h2

    h_hi = h2.astype(BF16)
    h_lo = (h2 - h_hi.astype(F32)).astype(BF16)
    r1 = jnp.dot(h_hi, wr_ref[...], preferred_element_type=F32)
    r2 = jnp.dot(h_lo, wr_ref[:, 0:LANES], preferred_element_type=F32)
    logits = r1[:, 0:LANES] + r1[:, LANES:2 * LANES] + r2 + br_ref[...]

    lane = lax.broadcasted_iota(jnp.int32, logits.shape, 1)
    is_g = lane < N_EXPERT_GROUPS
    gl = jnp.where(is_g, logits, -jnp.inf)
    gmax = jnp.max(gl, axis=-1, keepdims=True)
    gidx = jnp.min(jnp.where(gl == gmax, lane, LANES), axis=-1, keepdims=True)
    gsum = jnp.sum(jnp.where(is_g, jnp.exp(gl - gmax), 0.0), axis=-1, keepdims=True)
    g_gate = 1.0 / gsum

    lo = N_EXPERT_GROUPS + gidx * EXPERTS_PER_GROUP
    el = jnp.where(jnp.logical_and(lane >= lo, lane < lo + EXPERTS_PER_GROUP), logits, -jnp.inf)
    v0 = jnp.max(el, axis=-1, keepdims=True)
    i0 = jnp.min(jnp.where(el == v0, lane, LANES), axis=-1, keepdims=True)
    el = jnp.where(lane == i0, -jnp.inf, el)
    v1 = jnp.max(el, axis=-1, keepdims=True)
    i1 = jnp.min(jnp.where(el == v1, lane, LANES), axis=-1, keepdims=True)
    e = jnp.exp(v1 - v0)
    w0 = g_gate / (1.0 + e)
    w1 = g_gate * e / (1.0 + e)

    out = jnp.where(lane == 0, (i0 - N_EXPERT_GROUPS).astype(F32), 0.0)
    out = jnp.where(lane == 1, (i1 - N_EXPERT_GROUPS).astype(F32), out)
    out = jnp.where(lane == 2, w0, out)
    out = jnp.where(lane == 3, w1, out)
    route_ref[...] = out


def _out_projection(o_p, c_p, x_p, o_s, c_s, x_s, w_out, norm_g, wr, br, tm):
    np_, d = x_p.shape
    ns_ = x_s.shape[0]
    assert np_ % tm == 0 and ns_ % tm == 0
    nb_p, nb_s = np_ // tm, ns_ // tm
    n = np_ + ns_
    prow = lambda w: pl.BlockSpec((tm, w), lambda i: (jnp.minimum(i, nb_p - 1), 0))
    srow = lambda w: pl.BlockSpec((tm, w), lambda i: (jnp.maximum(i - nb_p, 0), 0))
    row = lambda w: pl.BlockSpec((tm, w), lambda i: (i, 0))
    full = lambda a: pl.BlockSpec(a.shape, lambda i: (0, 0))
    return pl.pallas_call(
        functools.partial(_outproj_kernel, nb_p=nb_p),
        grid=(nb_p + nb_s,),
        in_specs=[prow(o_p.shape[1]), prow(c_p.shape[1]), prow(d), srow(o_s.shape[1]), srow(c_s.shape[1]), srow(d),
                  full(w_out), full(norm_g), full(wr), full(br)],
        out_specs=[row(d), row(d), row(LANES)],
        out_shape=[jax.ShapeDtypeStruct((n, d), F32), jax.ShapeDtypeStruct((n, d), F32),
                   jax.ShapeDtypeStruct((n, LANES), F32)],
        compiler_params=_cparams(("arbitrary",)),
        name="out_projection_router",
    )(o_p, c_p, x_p, o_s, c_s, x_s, w_out, norm_g, wr, br)


DMA_UNROLL = SUBLANES


def _for_rows(n, fn):
    def group(gi, carry):
        for k in range(DMA_UNROLL):
            fn(gi * DMA_UNROLL + k, gi, k)
        return carry

    def single(i, carry):
        fn(i, lax.div(i, DMA_UNROLL), lax.rem(i, DMA_UNROLL))
        return carry

    n_grp = lax.div(n, DMA_UNROLL)
    lax.fori_loop(0, n_grp, group, 0)
    lax.fori_loop(n_grp * DMA_UNROLL, n, single, 0)


def _moe_kernel(order_ref, cexp_ref, cbase_ref, cnv_ref, h_ref, wg_ref, wu_ref, wd_ref, out_ref,
                xbuf, obuf, wgb, wub, wdb, gsem, ssem, *, n_tok, n_chunks):
    g = pl.program_id(0)
    slot = g % 2
    _, n_grp, grp, d = xbuf.shape

    def gather(chunk, buf_slot):
        base = cbase_ref[chunk]

        def desc(i, gi, k):
            tok = lax.shift_right_logical(order_ref[base + i], 1)
            return pltpu.make_async_copy(h_ref.at[pl.ds(tok, 1)], xbuf.at[buf_slot, gi, pl.ds(k, 1)],
                                         gsem.at[buf_slot])
        return desc

    def scatter(chunk, buf_slot):
        base = cbase_ref[chunk]

        def desc(i, gi, k):
            pair = order_ref[base + i]
            row = (pair & 1) * n_tok + lax.shift_right_logical(pair, 1)
            return pltpu.make_async_copy(obuf.at[buf_slot, gi, pl.ds(k, 1)], out_ref.at[pl.ds(row, 1)],
                                         ssem.at[buf_slot])
        return desc

    def gather_done(buf_slot):
        pltpu.make_async_copy(h_ref.at[pl.ds(0, 1)], xbuf.at[buf_slot, 0, pl.ds(0, 1)], gsem.at[buf_slot]).wait()

    def scatter_done(buf_slot):
        pltpu.make_async_copy(obuf.at[buf_slot, 0, pl.ds(0, 1)], out_ref.at[pl.ds(0, 1)], ssem.at[buf_slot]).wait()

    @pl.when(g == 0)
    def _():
        xbuf[...] = jnp.zeros(xbuf.shape, xbuf.dtype)
        _for_rows(cnv_ref[0], lambda i, gi, k: gather(0, 0)(i, gi, k).start())

    @pl.when(jnp.logical_or(g == 0, cexp_ref[g] != cexp_ref[jnp.maximum(g - 1, 0)]))
    def _():
        wgb[...] = wg_ref[0].astype(BF16)
        wub[...] = wu_ref[0].astype(BF16)
        wdb[...] = wd_ref[0].astype(BF16)

    @pl.when(g + 1 < n_chunks)
    def _():
        _for_rows(cnv_ref[g + 1], lambda i, gi, k: gather(g + 1, 1 - slot)(i, gi, k).start())

    nv = cnv_ref[g]
    _for_rows(nv, lambda i, gi, k: gather_done(slot))

    @pl.when(g >= 2)
    def _():
        _for_rows(cnv_ref[g - 2], lambda i, gi, k: scatter_done(slot))

    def expert_mlp(n_g):
        x = xbuf[slot, 0:n_g].reshape(n_g * grp, d).astype(BF16)
        gt = jnp.dot(x, wgb[...], preferred_element_type=F32)
        up = jnp.dot(x, wub[...], preferred_element_type=F32)
        hid = (gt * (1.0 / (1.0 + jnp.exp(-gt)))) * up
        y = jnp.dot(hid.astype(BF16), wdb[...], preferred_element_type=F32)
        obuf[slot, 0:n_g] = y.reshape(n_g, grp, d)

    sizes = sorted({n_grp, n_grp // 2, max(n_grp // 6, 1)})
    lo = 0
    for n_g in sizes:
        pl.when(jnp.logical_and(nv > lo * grp, nv <= n_g * grp))(functools.partial(expert_mlp, n_g))
        lo = n_g

    _for_rows(nv, lambda i, gi, k: scatter(g, slot)(i, gi, k).start())

    @pl.when(g == n_chunks - 1)
    def _():
        if n_chunks >= 2:
            _for_rows(cnv_ref[g - 1], lambda i, gi, k: scatter_done(1 - slot))
        _for_rows(nv, lambda i, gi, k: scatter_done(slot))


def _moe(order, cexp, cbase, cnv, h2, w_gate, w_up, w_down, ch):
    n, d = h2.shape
    ne, _, de = w_gate.shape
    n_chunks = cexp.shape[0]
    wspec = lambda shp: pl.BlockSpec((1,) + shp, lambda g, order, cexp, cbase, cnv: (cexp[g], 0, 0))
    return pl.pallas_call(
        functools.partial(_moe_kernel, n_tok=n, n_chunks=n_chunks),
        grid_spec=pltpu.PrefetchScalarGridSpec(
            num_scalar_prefetch=4,
            grid=(n_chunks,),
            in_specs=[
                pl.BlockSpec(memory_space=pl.ANY),
                wspec((d, de)), wspec((d, de)), wspec((de, d)),
            ],
            out_specs=pl.BlockSpec(memory_space=pl.ANY),
            scratch_shapes=[
                pltpu.VMEM((2, ch // DMA_UNROLL, DMA_UNROLL, d), F32),
                pltpu.VMEM((2, ch // DMA_UNROLL, DMA_UNROLL, d), F32),
                pltpu.VMEM((d, de), BF16),
                pltpu.VMEM((d, de), BF16),
                pltpu.VMEM((de, d), BF16),
                pltpu.SemaphoreType.DMA((2,)),
                pltpu.SemaphoreType.DMA((2,)),
            ],
        ),
        out_shape=jax.ShapeDtypeStruct((2 * n, d), F32),
        compiler_params=_cparams(("arbitrary",)),
        name="expert_mlp",
    )(order, cexp, cbase, cnv, h2, w_gate, w_up, w_down)


def _expert_chunks(eid, ch):
    n_pairs = eid.shape[0]
    n_chunks = N_EXPERTS + n_pairs // ch
    order = jnp.argsort(eid, stable=True).astype(jnp.int32)
    counts = jnp.sum(eid[:, None] == jnp.arange(N_EXPERTS, dtype=jnp.int32)[None, :], axis=0, dtype=jnp.int32)
    offs = jnp.cumsum(counts, dtype=jnp.int32) - counts
    nch = (counts + ch - 1) // ch
    cum = jnp.cumsum(nch, dtype=jnp.int32)
    gidx = jnp.arange(n_chunks, dtype=jnp.int32)
    cexp_raw = jnp.sum(gidx[:, None] >= cum[None, :], axis=1, dtype=jnp.int32)
    last = jnp.maximum(jnp.max(jnp.where(nch > 0, jnp.arange(N_EXPERTS, dtype=jnp.int32), 0)), 0)
    used = gidx < cum[-1]
    cexp = jnp.where(used, jnp.minimum(cexp_raw, N_EXPERTS - 1), last)
    local = gidx - (cum[cexp] - nch[cexp])
    cbase = jnp.where(used, offs[cexp] + local * ch, 0)
    cnv = jnp.where(used, jnp.clip(counts[cexp] - local * ch, 0, ch), 0)
    return order, cexp.astype(jnp.int32), cbase.astype(jnp.int32), cnv.astype(jnp.int32)


def _final_kernel(x2_ref, ya_ref, yb_ref, route_ref, g_ref, yp_ref, ys_ref, *, nb_p):
    i = pl.program_id(0)
    rt = route_ref[...]
    x3 = x2_ref[...] + rt[:, 2:3] * ya_ref[...] + rt[:, 3:4] * yb_ref[...]
    r = lax.rsqrt(jnp.mean(x3 * x3, axis=-1, keepdims=True) + EPS)
    y = (x3 * r) * g_ref[...]

    @pl.when(i < nb_p)
    def _():
        yp_ref[...] = y

    @pl.when(i >= nb_p)
    def _():
        ys_ref[...] = y


def _final(x2, y2, route, norm_g, np_, tm):
    n, d = x2.shape
    assert np_ % tm == 0 and n % tm == 0
    nb, nb_p = n // tm, np_ // tm
    return pl.pallas_call(
        functools.partial(_final_kernel, nb_p=nb_p),
        grid=(nb,),
        in_specs=[
            pl.BlockSpec((tm, d), lambda i: (i, 0)),
            pl.BlockSpec((tm, d), lambda i: (i, 0)),
            pl.BlockSpec((tm, d), lambda i: (i + nb, 0)),
            pl.BlockSpec((tm, LANES), lambda i: (i, 0)),
            pl.BlockSpec((1, d), lambda i: (0, 0)),
        ],
        out_specs=[
            pl.BlockSpec((tm, d), lambda i: (jnp.minimum(i, nb_p - 1), 0)),
            pl.BlockSpec((tm, d), lambda i: (jnp.maximum(i - nb_p, 0), 0)),
        ],
        out_shape=[jax.ShapeDtypeStruct((np_, d), F32), jax.ShapeDtypeStruct((n - np_, d), F32)],
        compiler_params=_cparams(("arbitrary",)),
        name="combine_final_norm",
    )(x2, y2, y2, route, norm_g)


def _pick_tile(n, pref):
    t = min(n, pref)
    while n % t:
        t //= 2
    return t


def kernel(x_prompt, x_sample, cache_k, cache_v, state_conv, page_table, norm_mix, w_in, w_dw, b_dw,
           conv_norm_g, conv_norm_b, attn_out_norm, w_out, norm_ffn, w_router_group, b_router_group,
           w_router_expert, b_router_expert, w_gate, w_up, w_down, norm_final):
    depth = norm_mix.shape[0]
    assert depth == 1
    b, t, d = x_prompt.shape
    db, s, _ = x_sample.shape
    n_pages = page_table.shape[1]
    page = cache_k.shape[2]
    past = n_pages * page
    nbf = past // MOBA_BLOCK
    assert past % MOBA_BLOCK == 0 and nbf >= MOBA_TOPK and nbf <= LANES
    d_conv = d - D_ATTN
    assert d_conv == D_ATTN
    np_, ns_ = b * t, db * s

    slopes = jnp.exp2(-8.0 * jnp.arange(1, N_HEADS + 1, dtype=F32) / N_HEADS)
    w_in_bf = w_in[0].astype(BF16)
    w_out_bf = w_out[0].astype(BF16)
    n_route = N_EXPERT_GROUPS + N_EXPERTS
    wr = jnp.concatenate([w_router_group[0], w_router_expert[0], jnp.zeros((d, LANES - n_route), F32)], axis=1)
    wr_hi = wr.astype(BF16)
    wr_lo = (wr - wr_hi.astype(F32)).astype(BF16)
    wr_cat = jnp.concatenate([wr_hi, wr_lo], axis=1)
    br = jnp.concatenate([b_router_group[0], b_router_expert[0], jnp.zeros((LANES - n_route,), F32)])[None]
    g_mix = norm_mix[0].reshape(1, d)
    g_ffn = norm_ffn[0].reshape(1, d)
    g_fin = norm_final.reshape(1, d)
    attn_norm = attn_out_norm[0].reshape(1, D_ATTN)
    conv_vec = lambda a: a[0].reshape(1, d_conv)

    xp = x_prompt.reshape(np_, d)
    xs = x_sample.reshape(ns_, d)
    qp, kp, vp, glu_p = _in_projection(xp, g_mix, w_in_bf, _pick_tile(np_, 512))
    qs, ks, vs, glu_s = _in_projection(xs, g_mix, w_in_bf, _pick_tile(ns_, 256))

    glu_p3 = glu_p.reshape(b, t, d_conv)
    glu_s3 = glu_s.reshape(db, s, d_conv)
    conv_args = (w_dw[0], conv_vec(b_dw), conv_vec(conv_norm_g), conv_vec(conv_norm_b))
    c_s = _conformer_conv(glu_s3, state_conv[0], *conv_args, tt=s, nseq=db, out_dtype=F32)
    conv_prompt = glu_p3[:, t - (CONV_WIDTH - 1):][None]
    conv_sample = jnp.concatenate([state_conv[0], glu_s3], axis=1)[:, -(CONV_WIDTH - 1):][None]

    ck = cache_k[0]
    cv = cache_v[0]
    o_p, kmean = _moba_prompt(qp.reshape(b, t, D_ATTN), kp.reshape(b, t, D_ATTN), vp.reshape(b, t, D_ATTN),
                              slopes, attn_norm, page_table, ck)

    to_heads = lambda a: a.reshape(db, s, N_HEADS, HEAD_DIM).transpose(0, 2, 1, 3)
    q_t, kn_t, vn_t = to_heads(qs), to_heads(ks), to_heads(vs)
    idx = _sample_topk(q_t, kmean)[..., :MOBA_TOPK]
    o_t, c_p = _moba_sample(page_table.reshape(-1), idx.reshape(-1), slopes, q_t, kn_t, vn_t,
                            attn_norm.reshape(N_HEADS, HEAD_DIM), ck, cv, past,
                            glu_p3, *conv_args, tt=_pick_tile(t, 256))
    o_s = o_t.transpose(0, 2, 1, 3).reshape(ns_, D_ATTN)

    tm = _pick_tile(ns_, 256)
    x2, h2, route = _out_projection(o_p.reshape(np_, D_ATTN), c_p.reshape(np_, d_conv), xp,
                                    o_s, c_s.reshape(ns_, d_conv), xs,
                                    w_out_bf, g_ffn, wr_cat, br, tm)

    ch = 384
    eid = route[:, 0:2].astype(jnp.int32).reshape(-1)
    order, cexp, cbase, cnv = _expert_chunks(eid, ch)
    y2 = _moe(order, cexp, cbase, cnv, h2, w_gate[0], w_up[0], w_down[0], ch)

    y_p, y_s = _final(x2, y2, route, g_fin, np_, tm)

    page_shape = (depth, b, t // page, page, N_HEADS, HEAD_DIM)
    return (y_p.reshape(b, t, d), y_s.reshape(db, s, d),
            kp.reshape(page_shape), vp.reshape(page_shape), conv_prompt,
            ks.reshape(depth, db, s, N_HEADS, HEAD_DIM), vs.reshape(depth, db, s, N_HEADS, HEAD_DIM),
            conv_sample)
```
